```python
import math
import jax, jax.numpy as jnp
from jax import lax
import numpy as np

D_MODEL = 2048
BATCH = 4
SEQ = 2048
DEPTH = 4
DEC_BATCH = 128
DEC_SEQ = 4
PAST_LEN = 16384
PAGE_SIZE = 128

N_MIXERS = 2
N_A = (DEPTH + 1) // 2
N_B = DEPTH // 2
HEAD_DIM = 128
MIX_WIDTH = D_MODEL
XA_HEADS = 4
XA_WIDTH = XA_HEADS * HEAD_DIM
TM_WIDTH = MIX_WIDTH - XA_WIDTH
N_MEM = 256
CONV_W = 4
EPS = 1e-6
LRU_WIDTH = TM_WIDTH
LRU_BLOCKS = TM_WIDTH // HEAD_DIM
LRU_BLOCK = LRU_WIDTH // LRU_BLOCKS
LRU_C = 8.0
GDN_HEADS = TM_WIDTH // HEAD_DIM
GDN_QKV = 3 * GDN_HEADS * HEAD_DIM
GDN_CHUNK = 64
PEER_HEADS = 8
PEER_NKEYS = 128
PEER_NEXP = PEER_NKEYS * PEER_NKEYS
PEER_DQ = 256
PEER_TOPK = 16
PEER_BLOCK = 256
A_IN = 2 * LRU_WIDTH + XA_WIDTH
B_IN = GDN_QKV + TM_WIDTH + 2 * GDN_HEADS + XA_WIDTH

kernel_name = 'hybrid_rglru_gdn_peer_memxattn_step'

F32 = jnp.float32


def rmsnorm(x, g):
    xf = x.astype(F32)
    y = xf * lax.rsqrt(jnp.mean(xf * xf, axis=-1, keepdims=True) + EPS)
    return (y * g.astype(F32)).astype(x.dtype)


def causal_conv(x, buf, w, b):
    L = x.shape[1]
    xp = jnp.concatenate([buf.astype(x.dtype), x], axis=1)
    y = b
    for j in range(CONV_W):
        y = y + xp[:, j:j + L] * w[j]
    return y, xp[:, -(CONV_W - 1):]


def rglru_mixer(u, h0, c0, conv_w, conv_b, wr, br, wi, bi, lam):
    gate, xr = u[..., :LRU_WIDTH], u[..., LRU_WIDTH:]
    xc, c_new = causal_conv(xr, c0, conv_w, conv_b)
    Bn, L, _ = xc.shape
    xb = xc.reshape(Bn, L, LRU_BLOCKS, LRU_BLOCK)
    r = jax.nn.sigmoid(jnp.einsum('blhi,hij->blhj', xb, wr) + br).reshape(Bn, L, LRU_WIDTH)
    i = jax.nn.sigmoid(jnp.einsum('blhi,hij->blhj', xb, wi) + bi).reshape(Bn, L, LRU_WIDTH)
    log_a = -LRU_C * r.astype(F32) * jax.nn.softplus(-lam.astype(F32))
    a = jnp.exp(log_a)
    b = jnp.sqrt(-jnp.expm1(2.0 * log_a)) * (i * xc).astype(F32)
    b = b.at[:, 0].add(a[:, 0] * h0.astype(F32))

    def combine(p, q):
        return p[0] * q[0], q[0] * p[1] + q[1]

    _, h = lax.associative_scan(combine, (a, b), axis=1)
    y = h.astype(u.dtype) * jax.nn.gelu(gate)
    return y, h[:, -1].astype(u.dtype), c_new


def gdn_mixer(u, S0, c0, conv_w, conv_b, a_log, dt_bias, o_norm):
    Bn, L, _ = u.shape
    H, D = GDN_HEADS, HEAD_DIM
    qkv = u[..., :GDN_QKV]
    z = u[..., GDN_QKV:GDN_QKV + TM_WIDTH]
    beta_in = u[..., GDN_QKV + TM_WIDTH:GDN_QKV + TM_WIDTH + H]
    alpha_in = u[..., GDN_QKV + TM_WIDTH + H:]
    qkv_c, c_new = causal_conv(qkv, c0, conv_w, conv_b)
    t = jax.nn.silu(qkv_c).astype(F32).reshape(Bn, L, 3, H, D)
    q, k, v = t[:, :, 0], t[:, :, 1], t[:, :, 2]
    q = q * lax.rsqrt(jnp.sum(q * q, -1, keepdims=True) + EPS) * (D ** -0.5)
    k = k * lax.rsqrt(jnp.sum(k * k, -1, keepdims=True) + EPS)
    beta = jax.nn.sigmoid(beta_in.astype(F32))
    g = -jnp.exp(a_log.astype(F32)) * jax.nn.softplus(alpha_in.astype(F32) + dt_bias.astype(F32))
    C = min(GDN_CHUNK, L)
    n = -(-L // C)
    P = n * C - L
    pad4 = ((0, 0), (0, P), (0, 0), (0, 0))
    pad3 = ((0, 0), (0, P), (0, 0))
    def blk4(a):
        return jnp.pad(a, pad4).reshape(Bn, n, C, H, D).transpose(1, 0, 3, 2, 4)
    def blk3(a):
        return jnp.pad(a, pad3).reshape(Bn, n, C, H).transpose(1, 0, 3, 2)
    qc, kc, vc = blk4(q), blk4(k), blk4(v)
    bc, gc = blk3(beta), jnp.cumsum(blk3(g), axis=-1)
    incl = jnp.tril(jnp.ones((C, C), bool))
    strict = jnp.tril(jnp.ones((C, C), bool), k=-1)
    diff = gc[..., :, None] - gc[..., None, :]
    decay = jnp.where(incl, jnp.exp(jnp.where(incl, diff, 0.0)), 0.0)
    kk = jnp.einsum('nbhid,nbhjd->nbhij', kc, kc)
    Amat = jnp.eye(C, dtype=F32) + jnp.where(strict, bc[..., :, None] * kk * decay, 0.0)
    value = lax.linalg.triangular_solve(Amat, bc[..., None] * vc, left_side=True, lower=True, unit_diagonal=True)
    kcum = lax.linalg.triangular_solve(Amat, (bc * jnp.exp(gc))[..., None] * kc, left_side=True, lower=True, unit_diagonal=True)
    qk = jnp.where(incl, jnp.einsum('nbhid,nbhjd->nbhij', qc, kc) * decay, 0.0)

    def step(S, inp):
        q_i, k_i, val_i, kcd_i, qk_i, g_i = inp
        u_new = val_i - jnp.einsum('bhcd,bhde->bhce', kcd_i, S)
        o = jnp.einsum('bhcd,bhde->bhce', q_i * jnp.exp(g_i)[..., None], S) + jnp.einsum('bhij,bhje->bhie', qk_i, u_new)
        g_last = g_i[..., -1]
        S = S * jnp.exp(g_last)[..., None, None] + jnp.einsum('bhcd,bhce->bhde', k_i * jnp.exp(g_last[..., None] - g_i)[..., None], u_new)
        return S, o

    S_fin, o = lax.scan(step, S0.astype(F32), (qc, kc, value, kcum, qk, gc))
    o = o.transpose(1, 0, 3, 2, 4).reshape(Bn, n * C, H, D)[:, :L]
    o = rmsnorm(o, o_norm) * jax.nn.silu(z.astype(F32).reshape(Bn, L, H, D))
    return o.reshape(Bn, L, TM_WIDTH).astype(u.dtype), S_fin.astype(u.dtype), c_new


def mem_kv(mem, g, w):
    kv = rmsnorm(mem, g) @ w
    Bn = mem.shape[0]
    k = kv[..., :XA_WIDTH].reshape(Bn, N_MEM, XA_HEADS, HEAD_DIM)
    v = kv[..., XA_WIDTH:].reshape(Bn, N_MEM, XA_HEADS, HEAD_DIM)
    return k, v


def cross_attn(q, mk, mv):
    Bn, L, _ = q.shape
    qh = q.reshape(Bn, L, XA_HEADS, HEAD_DIM)
    s = jnp.einsum('blhd,bmhd->bhlm', qh, mk).astype(F32) * (HEAD_DIM ** -0.5)
    p = jax.nn.softmax(s, axis=-1).astype(mv.dtype)
    o = jnp.einsum('bhlm,bmhd->blhd', p, mv)
    return o.reshape(Bn, L, XA_WIDTH)


def peer(x, wq, sub_keys, u_tab, v_tab):
    Bn, L, D = x.shape
    xt = x.reshape(-1, D)
    T = xt.shape[0]
    q = (xt @ wq).reshape(T, PEER_HEADS, 2, PEER_DQ // 2).astype(F32)
    s = jnp.einsum('thpd,hpnd->thpn', q, sub_keys.astype(F32))
    sv, si = lax.top_k(s, PEER_TOPK)
    cand = (sv[:, :, 0, :, None] + sv[:, :, 1, None, :]).reshape(T, PEER_HEADS, PEER_TOPK * PEER_TOPK)
    cidx = (si[:, :, 0, :, None] * PEER_NKEYS + si[:, :, 1, None, :]).reshape(T, PEER_HEADS, PEER_TOPK * PEER_TOPK)
    top, pos = lax.top_k(cand, PEER_TOPK)
    eidx = jnp.take_along_axis(cidx, pos, axis=-1)
    gate = jax.nn.softmax(top, axis=-1).astype(x.dtype)
    blk = min(PEER_BLOCK, T)
    nb = -(-T // blk)
    pad = nb * blk - T
    xp = jnp.pad(xt, ((0, pad), (0, 0))).reshape(nb, blk, D)
    ip = jnp.pad(eidx, ((0, pad), (0, 0), (0, 0))).reshape(nb, blk, PEER_HEADS, PEER_TOPK)
    gp = jnp.pad(gate, ((0, pad), (0, 0), (0, 0))).reshape(nb, blk, PEER_HEADS, PEER_TOPK)

    def expert_block(args):
        xb, ib, gb = args
        hb = jax.nn.gelu(jnp.einsum('td,thkd->thk', xb, u_tab[ib]))
        return jnp.einsum('thk,thkd->td', gb * hb, v_tab[ib])

    y = lax.map(expert_block, (xp, ip, gp)).reshape(nb * blk, D)[:T]
    return y.reshape(Bn, L, D)


def trunk(x, mem_k, mem_v, lru_h0, lru_c0, gdn_S0, gdn_c0, w):
    lru_h, lru_c, gdn_S, gdn_c = [], [], [], []
    for l in range(DEPTH):
        j = l // N_MIXERS
        xn = rmsnorm(x, w['norm_mix'][l])
        if l % N_MIXERS == 0:
            u = xn @ w['w_in_a'][j]
            tm, h, c = rglru_mixer(u[..., :2 * LRU_WIDTH], lru_h0[j], lru_c0[j], w['lru_conv_w'][j], w['lru_conv_b'][j],
                                   w['lru_wr'][j], w['lru_br'][j], w['lru_wi'][j], w['lru_bi'][j], w['lru_lambda'][j])
            lru_h.append(h)
            lru_c.append(c)
        else:
            u = xn @ w['w_in_b'][j]
            tm, S, c = gdn_mixer(u[..., :B_IN - XA_WIDTH], gdn_S0[j], gdn_c0[j], w['gdn_conv_w'][j], w['gdn_conv_b'][j],
                                 w['gdn_a_log'][j], w['gdn_dt_bias'][j], w['gdn_o_norm'][j])
            gdn_S.append(S)
            gdn_c.append(c)
        xa = cross_attn(u[..., -XA_WIDTH:], mem_k[l], mem_v[l])
        x = x + jnp.concatenate([tm, xa], axis=-1) @ w['w_out'][l]
        x = x + peer(rmsnorm(x, w['norm_ffn'][l]), w['peer_wq'][l], w['peer_subkeys'][l], w['peer_u'][l], w['peer_v'][l])
    y = rmsnorm(x, w['norm_final'])
    return y, jnp.stack(lru_h), jnp.stack(lru_c), jnp.stack(gdn_S), jnp.stack(gdn_c)


def setup_inputs(seed: int = 0) -> dict:
    key = jax.random.key(seed)
    ks = iter(jax.random.split(key, 48))
    def nrm(shape, scale):
        return jax.random.normal(next(ks), shape, F32) * scale
    def unif(shape, lo, hi):
        return jax.random.uniform(next(ks), shape, F32, lo, hi)
    D = D_MODEL
    p = unif((N_A, LRU_WIDTH), 0.9, 0.999) ** (1.0 / LRU_C)
    dt = jnp.exp(unif((N_B, GDN_HEADS), math.log(1e-3), math.log(0.1)))
    return {
        'x_prompt': nrm((BATCH, SEQ, D), 1.0),
        'x_sample': nrm((DEC_BATCH, DEC_SEQ, D), 1.0),
        'state_rglru_h': nrm((N_A, DEC_BATCH, LRU_WIDTH), 0.5),
        'state_rglru_conv': nrm((N_A, DEC_BATCH, CONV_W - 1, LRU_WIDTH), 1.0),
        'state_gdn_S': nrm((N_B, DEC_BATCH, GDN_HEADS, HEAD_DIM, HEAD_DIM), 0.1),
        'state_gdn_conv': nrm((N_B, DEC_BATCH, CONV_W - 1, GDN_QKV), 1.0),
        'cache_mem_k': nrm((DEPTH, DEC_BATCH, N_MEM, XA_HEADS, HEAD_DIM), 1.0),
        'cache_mem_v': nrm((DEPTH, DEC_BATCH, N_MEM, XA_HEADS, HEAD_DIM), 1.0),
        'mem_prompt': nrm((BATCH, N_MEM, D), 1.0),
        'norm_mix': 1.0 + nrm((DEPTH, D), 0.02),
        'norm_ffn': 1.0 + nrm((DEPTH, D), 0.02),
        'norm_final': 1.0 + nrm((D,), 0.02),
        'w_in_a': nrm((N_A, D, A_IN), D ** -0.5),
        'w_in_b': nrm((N_B, D, B_IN), D ** -0.5),
        'w_out': nrm((DEPTH, MIX_WIDTH, D), (2.0 * MIX_WIDTH) ** -0.5),
        'lru_conv_w': nrm((N_A, CONV_W, LRU_WIDTH), 0.5),
        'lru_conv_b': nrm((N_A, LRU_WIDTH), 0.02),
        'lru_wr': nrm((N_A, LRU_BLOCKS, LRU_BLOCK, LRU_BLOCK), LRU_BLOCK ** -0.5),
        'lru_br': nrm((N_A, LRU_BLOCKS, LRU_BLOCK), 0.02),
        'lru_wi': nrm((N_A, LRU_BLOCKS, LRU_BLOCK, LRU_BLOCK), LRU_BLOCK ** -0.5),
        'lru_bi': nrm((N_A, LRU_BLOCKS, LRU_BLOCK), 0.02),
        'lru_lambda': jnp.log(p) - jnp.log1p(-p),
        'gdn_conv_w': nrm((N_B, CONV_W, GDN_QKV), 0.5),
        'gdn_conv_b': nrm((N_B, GDN_QKV), 0.02),
        'gdn_a_log': jnp.log(unif((N_B, GDN_HEADS), 1.0, 16.0)),
        'gdn_dt_bias': dt + jnp.log(-jnp.expm1(-dt)),
        'gdn_o_norm': 1.0 + nrm((N_B, HEAD_DIM), 0.02),
        'norm_mem': 1.0 + nrm((DEPTH, D), 0.02),
        'w_mem_kv': nrm((DEPTH, D, 2 * XA_WIDTH), D ** -0.5),
        'peer_wq': nrm((DEPTH, D, PEER_HEADS * PEER_DQ), D ** -0.5),
        'peer_subkeys': nrm((DEPTH, PEER_HEADS, 2, PEER_NKEYS, PEER_DQ // 2), (PEER_DQ // 2) ** -0.5),
        'peer_u': nrm((DEPTH, PEER_NEXP, D), D ** -0.5),
        'peer_v': nrm((DEPTH, PEER_NEXP, D), 0.5 / math.sqrt(PEER_HEADS)),
    }


def reference(x_prompt, x_sample, state_rglru_h, state_rglru_conv, state_gdn_S, state_gdn_conv,
              cache_mem_k, cache_mem_v, mem_prompt,
              norm_mix, norm_ffn, norm_final, w_in_a, w_in_b, w_out,
              lru_conv_w, lru_conv_b, lru_wr, lru_br, lru_wi, lru_bi, lru_lambda,
              gdn_conv_w, gdn_conv_b, gdn_a_log, gdn_dt_bias, gdn_o_norm,
              norm_mem, w_mem_kv, peer_wq, peer_subkeys, peer_u, peer_v):
    w = dict(norm_mix=norm_mix, norm_ffn=norm_ffn, norm_final=norm_final, w_in_a=w_in_a, w_in_b=w_in_b,
             w_out=w_out, lru_conv_w=lru_conv_w, lru_conv_b=lru_conv_b, lru_wr=lru_wr, lru_br=lru_br,
             lru_wi=lru_wi, lru_bi=lru_bi, lru_lambda=lru_lambda, gdn_conv_w=gdn_conv_w, gdn_conv_b=gdn_conv_b,
             gdn_a_log=gdn_a_log, gdn_dt_bias=gdn_dt_bias, gdn_o_norm=gdn_o_norm,
             peer_wq=peer_wq, peer_subkeys=peer_subkeys, peer_u=peer_u, peer_v=peer_v)
    dt = x_prompt.dtype
    pk, pv = [], []
    for l in range(DEPTH):
        k_l, v_l = mem_kv(mem_prompt, norm_mem[l], w_mem_kv[l])
        pk.append(k_l)
        pv.append(v_l)
    y_prompt, p_h, p_ca, p_S, p_cb = trunk(
        x_prompt, pk, pv,
        jnp.zeros((N_A, BATCH, LRU_WIDTH), dt), jnp.zeros((N_A, BATCH, CONV_W - 1, LRU_WIDTH), dt),
        jnp.zeros((N_B, BATCH, GDN_HEADS, HEAD_DIM, HEAD_DIM), dt), jnp.zeros((N_B, BATCH, CONV_W - 1, GDN_QKV), dt), w)
    p_mem_k = jnp.stack(pk)
    p_mem_v = jnp.stack(pv)
    y_sample, s_h, s_ca, s_S, s_cb = trunk(
        x_sample, [cache_mem_k[l] for l in range(DEPTH)], [cache_mem_v[l] for l in range(DEPTH)],
        state_rglru_h, state_rglru_conv, state_gdn_S, state_gdn_conv, w)
    return (y_prompt, y_sample, p_h, p_ca, p_S, p_cb, p_mem_k, p_mem_v, s_h, s_ca, s_S, s_cb)
```

```python
import functools
import math

import jax
import jax.numpy as jnp
from jax import lax
from jax.experimental import pallas as pl
from jax.experimental.pallas import tpu as pltpu

F32 = jnp.float32
BF16 = jnp.bfloat16

D_MODEL = 2048
DEPTH = 4
HEAD_DIM = 128
XA_HEADS = 4
XA_WIDTH = XA_HEADS * HEAD_DIM
TM_WIDTH = D_MODEL - XA_WIDTH
N_MEM = 256
CONV_W = 4
EPS = 1e-6
LRU_BLOCKS = TM_WIDTH // HEAD_DIM
LRU_C = 8.0
GDN_HEADS = TM_WIDTH // HEAD_DIM
GDN_QKV = 3 * TM_WIDTH
GDN_CHUNK = 64
PEER_HEADS = 8
PEER_NKEYS = 128
PEER_TOPK = 16
B_IN_PAD = 7168

SUBLANES = 8
LANES = 128
VMEM_LIMIT = 52 * 1024 * 1024

NEG_INF = float("-inf")


def _cparams(sem):
    return pltpu.CompilerParams(dimension_semantics=sem, vmem_limit_bytes=VMEM_LIMIT)


def _dot(a, b, precision=None):
    return jnp.dot(a, b, preferred_element_type=F32, precision=precision)


def _mm(a, b):
    return jnp.dot(a.astype(BF16), b.astype(BF16), preferred_element_type=F32)


_F32_DOT = lax.Precision.HIGHEST


def _dot_nt(a, b, precision=None):
    return lax.dot_general(a, b, (((1,), (1,)), ((), ())), preferred_element_type=F32, precision=precision)


def _dot_tn(a, b, precision=None):
    return lax.dot_general(a, b, (((0,), (0,)), ((), ())), preferred_element_type=F32, precision=precision)


def _sigmoid(x):
    return 1.0 / (1.0 + jnp.exp(-x))


def _silu(x):
    return x * _sigmoid(x)


def _gelu_tanh(x):
    c = math.sqrt(2.0 / math.pi)
    return 0.5 * x * (1.0 + jnp.tanh(c * (x + 0.044715 * (x * x * x))))


def _softplus(x):
    return jnp.maximum(x, 0.0) + jnp.log(1.0 + jnp.exp(-jnp.abs(x)))


def _rms(x, g):
    ms = jnp.mean(x * x, axis=-1, keepdims=True)
    return x * lax.rsqrt(ms + EPS) * g


def _norm_matmul_kernel(x_ref, g_ref, w_ref, o_ref, xn_ref):
    @pl.when(pl.program_id(1) == 0)
    def _():
        xn_ref[...] = _rms(x_ref[...], g_ref[...]).astype(BF16)

    o_ref[...] = _dot(xn_ref[...], w_ref[...])


def norm_matmul(x, g, w, *, tm=512, tn=512):
    m, k = x.shape
    n = w.shape[1]
    tm = min(tm, m)
    assert m % tm == 0 and n % tn == 0
    return pl.pallas_call(
        _norm_matmul_kernel,
        grid=(m // tm, n // tn),
        in_specs=[
            pl.BlockSpec((tm, k), lambda i, j: (i, 0)),
            pl.BlockSpec((1, k), lambda i, j: (0, 0)),
            pl.BlockSpec((k, tn), lambda i, j: (0, j)),
        ],
        out_specs=pl.BlockSpec((tm, tn), lambda i, j: (i, j)),
        out_shape=jax.ShapeDtypeStruct((m, n), F32),
        scratch_shapes=[pltpu.VMEM((tm, k), BF16)],
        compiler_params=_cparams(("parallel", "arbitrary")),
        name="norm_matmul",
    )(x, g.reshape(1, k), w)


def _out_proj_kernel(a_ref, b_ref, wa_ref, wb_ref, r_ref, o_ref):
    acc = _dot(a_ref[...], wa_ref[...]) + _dot(b_ref[...], wb_ref[...])
    o_ref[...] = r_ref[...] + acc


def out_proj(a, b, wa, wb, r, *, tm=512, tn=512):
    m, ka = a.shape
    kb = b.shape[1]
    n = wa.shape[1]
    tm = min(tm, m)
    assert m % tm == 0 and n % tn == 0
    return pl.pallas_call(
        _out_proj_kernel,
        grid=(m // tm, n // tn),
        in_specs=[
            pl.BlockSpec((tm, ka), lambda i, j: (i, 0)),
            pl.BlockSpec((tm, kb), lambda i, j: (i, 0)),
            pl.BlockSpec((ka, tn), lambda i, j: (0, j)),
            pl.BlockSpec((kb, tn), lambda i, j: (0, j)),
            pl.BlockSpec((tm, tn), lambda i, j: (i, j)),
        ],
        out_specs=pl.BlockSpec((tm, tn), lambda i, j: (i, j)),
        out_shape=jax.ShapeDtypeStruct((m, n), F32),
        compiler_params=_cparams(("parallel", "arbitrary")),
        name="out_proj",
    )(a, b, wa, wb, r)


def _rmsnorm_kernel(x_ref, g_ref, o_ref):
    o_ref[...] = _rms(x_ref[...], g_ref[...])


def rmsnorm_rows(x, g, *, tm=512):
    m, k = x.shape
    tm = min(tm, m)
    assert m % tm == 0
    return pl.pallas_call(
        _rmsnorm_kernel,
        grid=(m // tm,),
        in_specs=[pl.BlockSpec((tm, k), lambda i: (i, 0)), pl.BlockSpec((1, k), lambda i: (0, 0))],
        out_specs=pl.BlockSpec((tm, k), lambda i: (i, 0)),
        out_shape=jax.ShapeDtypeStruct((m, k), F32),
        compiler_params=_cparams(("parallel",)),
        name="final_rmsnorm",
    )(x, g.reshape(1, k))


def _rglru_kernel(gate_ref, xr_ref, cw_ref, cb_ref, c0_ref, h0_ref, wr_ref, br_ref, wi_ref, bi_ref, lam_ref,
                  y_ref, hout_ref, hist_ref, a_ref, b_ref, hs_ref, h_ref, *, tl, R):
    n = pl.program_id(1)
    rows = tl * R
    hist_rows = (CONV_W - 1) * R
    off0 = -(-hist_rows // SUBLANES) * SUBLANES

    @pl.when(n == 0)
    def _():
        hist_ref[off0 - hist_rows:off0, :] = c0_ref[0]
        h_ref[...] = h0_ref[0]

    x = xr_ref[...]
    hist_ref[off0:off0 + rows, :] = x
    xc = cb_ref[...]
    for j in range(CONV_W):
        s = off0 - hist_rows + j * R
        xc = xc + hist_ref[s:s + rows, :] * cw_ref[j:j + 1, :]
    tail = hist_ref[off0 + rows - hist_rows:off0 + rows, :]
    hist_ref[off0 - hist_rows:off0, :] = tail

    sp = _softplus(-lam_ref[...])
    for blk in range(LRU_BLOCKS):
        cs = slice(blk * HEAD_DIM, (blk + 1) * HEAD_DIM)
        xb = xc[:, cs]
        xb16 = xb.astype(BF16)
        r = _sigmoid(_dot(xb16, wr_ref[blk]) + br_ref[blk])
        i = _sigmoid(_dot(xb16, wi_ref[blk]) + bi_ref[blk])
        log_a = (-LRU_C) * r * sp[:, cs]
        a = jnp.exp(log_a)
        a_ref[:, cs] = a
        b_ref[:, cs] = jnp.sqrt(1.0 - a * a) * (i * xb)

    def step(l, carry):
        if R % SUBLANES == 0:
            rs = pl.ds(pl.multiple_of(l * R, SUBLANES), R)
        else:
            rs = pl.ds(l * R, R)
        h = a_ref[rs, :] * h_ref[...] + b_ref[rs, :]
        h_ref[...] = h
        hs_ref[rs, :] = h
        return carry

    lax.fori_loop(0, tl, step, 0)
    y_ref[...] = (hs_ref[...] * _gelu_tanh(gate_ref[...])).astype(y_ref.dtype)
    hout_ref[0] = h_ref[...]


def rglru(u, c0, h0, cw, cb, wr, br, wi, bi, lam, *, G, L, R, tl):
    W = TM_WIDTH
    assert L % tl == 0
    rows = tl * R
    nt = L // tl
    hist_rows = (CONV_W - 1) * R
    off0 = -(-hist_rows // SUBLANES) * SUBLANES
    kern = functools.partial(_rglru_kernel, tl=tl, R=R)
    const2 = lambda g, n: (0, 0)
    const3 = lambda g, n: (0, 0, 0)
    y, hout = pl.pallas_call(
        kern,
        grid=(G, nt),
        in_specs=[
            pl.BlockSpec((rows, W), lambda g, n: (g * nt + n, 0)),
            pl.BlockSpec((rows, W), lambda g, n: (g * nt + n, 1)),
            pl.BlockSpec((CONV_W, W), const2),
            pl.BlockSpec((1, W), const2),
            pl.BlockSpec((1, hist_rows, W), lambda g, n: (g, 0, 0)),
            pl.BlockSpec((1, R, W), lambda g, n: (g, 0, 0)),
            pl.BlockSpec((LRU_BLOCKS, HEAD_DIM, HEAD_DIM), const3),
            pl.BlockSpec((LRU_BLOCKS, 1, HEAD_DIM), const3),
            pl.BlockSpec((LRU_BLOCKS, HEAD_DIM, HEAD_DIM), const3),
            pl.BlockSpec((LRU_BLOCKS, 1, HEAD_DIM), const3),
            pl.BlockSpec((1, W), const2),
        ],
        out_specs=[
            pl.BlockSpec((rows, W), lambda g, n: (g * nt + n, 0)),
            pl.BlockSpec((1, R, W), lambda g, n: (g, 0, 0)),
        ],
        out_shape=[
            jax.ShapeDtypeStruct((G * L * R, W), BF16),
            jax.ShapeDtypeStruct((G, R, W), F32),
        ],
        scratch_shapes=[
            pltpu.VMEM((off0 + rows, W), F32),
            pltpu.VMEM((rows, W), F32),
            pltpu.VMEM((rows, W), F32),
            pltpu.VMEM((rows, W), F32),
            pltpu.VMEM((R, W), F32),
        ],
        compiler_params=_cparams(("parallel", "arbitrary")),
        name="rglru",
    )(u, u, cw, cb.reshape(1, W), c0, h0, wr, br.reshape(LRU_BLOCKS, 1, HEAD_DIM), wi,
      bi.reshape(LRU_BLOCKS, 1, HEAD_DIM), lam.reshape(1, W))
    return y, hout


def _inv_unit_lower(A, C, row, col):
    eye = (row == col).astype(F32)
    base = min(SUBLANES, C)
    sh = int(math.log2(base))
    Ad = jnp.where((row >> sh) == (col >> sh), A, 0.0)
    T = eye - Ad
    P = Ad
    span = 2
    while span < base:
        P = _dot(P, P, _F32_DOT)
        T = _dot(T, eye + P, _F32_DOT)
        span *= 2
    s = base
    while s < C:
        sh = int(math.log2(s))
        off = ((row >> (sh + 1)) == (col >> (sh + 1))) & ((row >> sh) != (col >> sh))
        Aoff = jnp.where(off, A, 0.0)
        T = T - _dot(T, _dot(Aoff, T, _F32_DOT), _F32_DOT)
        s *= 2
    return T


def _gdn_kernel(qkv_ref, z_ref, ba_ref, cw_ref, cb_ref, c0_ref, S0_ref, alog_ref, dtb_ref, onorm_ref,
                o_ref, S_ref, hist_ref, *, C, n_valid):
    n = pl.program_id(1)
    H, Dh = GDN_HEADS, HEAD_DIM
    hist_rows = CONV_W - 1
    off0 = SUBLANES

    @pl.when(n == 0)
    def _():
        hist_ref[off0 - hist_rows:off0, :] = c0_ref[0]
        S_ref[0] = S0_ref[0]

    x = qkv_ref[...]
    hist_ref[off0:off0 + C, :] = x
    xc = cb_ref[...]
    for j in range(CONV_W):
        s = off0 - hist_rows + j
        xc = xc + hist_ref[s:s + C, :] * cw_ref[j:j + 1, :]
    tail = hist_ref[off0 + C - hist_rows:off0 + C, :]
    hist_ref[off0 - hist_rows:off0, :] = tail
    t = _silu(xc)

    ba = ba_ref[...]
    beta_all = _sigmoid(ba)
    g_all = -jnp.exp(alog_ref[...]) * _softplus(ba + dtb_ref[...])
    if n_valid < C:
        rmask = lax.broadcasted_iota(jnp.int32, (C, LANES), 0) < n_valid
        beta_all = jnp.where(rmask, beta_all, 0.0)
        g_all = jnp.where(rmask, g_all, 0.0)
    if C < LANES:
        g_pad = jnp.concatenate([g_all, jnp.zeros((LANES - C, LANES), F32)], axis=0)
    else:
        g_pad = g_all
    r128 = lax.broadcasted_iota(jnp.int32, (LANES, LANES), 0)
    c128 = lax.broadcasted_iota(jnp.int32, (LANES, LANES), 1)
    tril = (c128 <= r128).astype(F32)
    gc_pad = _dot(tril, g_pad, precision=lax.Precision.HIGHEST)
    gcT = gc_pad.T

    row = lax.broadcasted_iota(jnp.int32, (C, C), 0)
    col = lax.broadcasted_iota(jnp.int32, (C, C), 1)
    incl = col <= row
    strict = col < row

    for h in range(H):
        cs = slice(h * Dh, (h + 1) * Dh)
        qh = t[:, cs]
        kh = t[:, TM_WIDTH + h * Dh:TM_WIDTH + (h + 1) * Dh]
        vh = t[:, 2 * TM_WIDTH + h * Dh:2 * TM_WIDTH + (h + 1) * Dh]
        qn = qh * lax.rsqrt(jnp.sum(qh * qh, -1, keepdims=True) + EPS) * (Dh ** -0.5)
        kn = kh * lax.rsqrt(jnp.sum(kh * kh, -1, keepdims=True) + EPS)
        bcol = beta_all[:, h:h + 1]
        gcol = gc_pad[0:C, H + h:H + h + 1]
        grow = gcT[H + h:H + h + 1, 0:C]
        diff = gcol - grow
        decay = jnp.where(incl, jnp.exp(jnp.where(incl, diff, 0.0)), 0.0)
        kn16 = kn.astype(BF16)
        kk = _dot_nt(kn16, kn16)
        A = jnp.where(strict, bcol * kk * decay, 0.0)
        T = _inv_unit_lower(A, C, row, col)
        egc = jnp.exp(gcol)
        rhs = jnp.concatenate([bcol * vh, (bcol * egc) * kn], axis=1)
        sol = _dot(T, rhs, _F32_DOT)
        value = sol[:, :Dh]
        kcum = sol[:, Dh:]
        qk = jnp.where(incl, _dot_nt(qn.astype(BF16), kn16) * decay, 0.0)
        S = S_ref[0, h]
        u_new = value - _mm(kcum, S)
        o = _mm(qn * egc, S) + _mm(qk, u_new)
        g_last = gcol[C - 1:C, :]
        kdec = (kn * jnp.exp(g_last - gcol)).astype(BF16)
        S_ref[0, h] = S * jnp.exp(g_last) + _dot_tn(kdec, u_new.astype(BF16))
        on = o * lax.rsqrt(jnp.mean(o * o, -1, keepdims=True) + EPS) * onorm_ref[...]
        o_ref[:, cs] = (on * _silu(z_ref[:, cs])).astype(o_ref.dtype)


def gdn(u, c0, S0, cw, cb, a_log, dt_bias, o_norm, *, B, L, C, n_valid, qkv_blk, z_blk, ba_blk):
    H, Dh = GDN_HEADS, HEAD_DIM
    nc = L // C
    alog = jnp.zeros((1, LANES), F32).at[0, H:2 * H].set(a_log)
    dtb = jnp.zeros((1, LANES), F32).at[0, H:2 * H].set(dt_bias)
    kern = functools.partial(_gdn_kernel, C=C, n_valid=n_valid)
    const2 = lambda b, n: (0, 0)
    o, S = pl.pallas_call(
        kern,
        grid=(B, nc),
        in_specs=[
            pl.BlockSpec((C, GDN_QKV), lambda b, n: (b * nc + n, qkv_blk)),
            pl.BlockSpec((C, TM_WIDTH), lambda b, n: (b * nc + n, z_blk)),
            pl.BlockSpec((C, LANES), lambda b, n: (b * nc + n, ba_blk)),
            pl.BlockSpec((CONV_W, GDN_QKV), const2),
            pl.BlockSpec((1, GDN_QKV), const2),
            pl.BlockSpec((1, CONV_W - 1, GDN_QKV), lambda b, n: (b, 0, 0)),
            pl.BlockSpec((1, H, Dh, Dh), lambda b, n: (b, 0, 0, 0)),
            pl.BlockSpec((1, LANES), const2),
            pl.BlockSpec((1, LANES), const2),
            pl.BlockSpec((1, Dh), const2),
        ],
        out_specs=[
            pl.BlockSpec((C, TM_WIDTH), lambda b, n: (b * nc + n, 0)),
            pl.BlockSpec((1, H, Dh, Dh), lambda b, n: (b, 0, 0, 0)),
        ],
        out_shape=[
            jax.ShapeDtypeStruct((B * L, TM_WIDTH), BF16),
            jax.ShapeDtypeStruct((B, H, Dh, Dh), F32),
        ],
        scratch_shapes=[pltpu.VMEM((SUBLANES + C, GDN_QKV), F32)],
        compiler_params=_cparams(("parallel", "arbitrary")),
        name="gdn",
    )(u, u, u, cw, cb.reshape(1, GDN_QKV), c0, S0, alog, dtb, o_norm.reshape(1, Dh))
    return o, S


def _xattn_kernel(q_ref, k_ref, v_ref, o_ref):
    scale = HEAD_DIM ** -0.5
    for h in range(XA_HEADS):
        cs = slice(h * HEAD_DIM, (h + 1) * HEAD_DIM)
        qh = q_ref[:, cs].astype(BF16)
        kh = k_ref[:, cs].astype(BF16)
        vh = v_ref[:, cs].astype(BF16)
        s = _dot_nt(qh, kh) * scale
        m = jnp.max(s, axis=-1, keepdims=True)
        p = jnp.exp(s - m)
        p = p / jnp.sum(p, axis=-1, keepdims=True)
        o_ref[:, cs] = _dot(p.astype(BF16), vh).astype(o_ref.dtype)


def xattn(q, k, v, *, B, Lq, tq, q_blk, k_blk, v_blk):
    nq = Lq // tq
    return pl.pallas_call(
        _xattn_kernel,
        grid=(B, nq),
        in_specs=[
            pl.BlockSpec((tq, XA_WIDTH), lambda b, n: (b * nq + n, q_blk)),
            pl.BlockSpec((N_MEM, XA_WIDTH), lambda b, n: (b, k_blk)),
            pl.BlockSpec((N_MEM, XA_WIDTH), lambda b, n: (b, v_blk)),
        ],
        out_specs=pl.BlockSpec((tq, XA_WIDTH), lambda b, n: (b * nq + n, 0)),
        out_shape=jax.ShapeDtypeStruct((B * Lq, XA_WIDTH), BF16),
        compiler_params=_cparams(("parallel", "arbitrary")),
        name="xattn",
    )(q, k, v)


def _cand_counts():
    return [PEER_TOPK // (a + 1) for a in range(PEER_TOPK)]


def _peer_topk_kernel(q_ref, sk_ref, pw_ref, tau_ref, sv_ref, cand_ref):
    K = PEER_TOPK
    q = q_ref[...]
    s = []
    for p in range(2):
        qp = q[:, p * LANES:(p + 1) * LANES]
        s.append(_dot_nt(sk_ref[0, p], qp, precision=lax.Precision.HIGHEST))
    for p in range(2):
        cur = s[p]
        for r in range(K):
            m = jnp.max(cur, axis=0, keepdims=True)
            sv_ref[p, r:r + 1, :] = m
            cur = jnp.where(cur == m, NEG_INF, cur)
    sv1 = sv_ref[0]
    sv2 = sv_ref[1]
    cand_ref[...] = jnp.full(cand_ref.shape, NEG_INF, F32)
    off = 0
    for a, nb in enumerate(_cand_counts()):
        cand_ref[off:off + nb, :] = sv1[a:a + 1, :] + sv2[0:nb, :]
        off += nb
    cur = cand_ref[...]
    tops = []
    for r in range(K):
        m = jnp.max(cur, axis=0, keepdims=True)
        tops.append(m)
        cur = jnp.where(cur == m, NEG_INF, cur)
    z = jnp.ones_like(tops[0])
    for r in range(1, K):
        z = z + jnp.exp(tops[r] - tops[0])
    pw_ref[0, 0] = s[0]
    pw_ref[0, 1] = s[1]
    pw_ref[0, 2] = jnp.exp(s[0] - sv1[0:1, :]) / z
    pw_ref[0, 3] = jnp.exp(s[1] - sv2[0:1, :])
    tau_ref[0] = jnp.broadcast_to(tops[K - 1], tau_ref.shape[1:])


def peer_topk(q, subkeys, *, tt=512):
    T = q.shape[0]
    tt = min(tt, T)
    assert T % tt == 0
    H, NK = PEER_HEADS, PEER_NKEYS
    ncand = -(-sum(_cand_counts()) // SUBLANES) * SUBLANES
    return pl.pallas_call(
        _peer_topk_kernel,
        grid=(T // tt, H),
        in_specs=[
            pl.BlockSpec((tt, 2 * LANES), lambda i, h: (i, h)),
            pl.BlockSpec((1, 2, NK, LANES), lambda i, h: (h, 0, 0, 0)),
        ],
        out_specs=[
            pl.BlockSpec((1, 4, NK, tt), lambda i, h: (h, 0, 0, i)),
            pl.BlockSpec((1, SUBLANES, tt), lambda i, h: (h, 0, i)),
        ],
        out_shape=[
            jax.ShapeDtypeStruct((H, 4, NK, T), F32),
            jax.ShapeDtypeStruct((H, SUBLANES, T), F32),
        ],
        scratch_shapes=[pltpu.VMEM((2, PEER_TOPK, tt), F32), pltpu.VMEM((ncand, tt), F32)],
        compiler_params=_cparams(("parallel", "arbitrary")),
        name="peer_topk",
    )(q, subkeys)


def _peer_dense_kernel(x_ref, g_ref, pw_ref, tau_ref, u_ref, v_ref, o_ref, xnT_ref, acc_ref, *, ib):
    e = pl.program_id(1)
    H, NK = PEER_HEADS, PEER_NKEYS

    @pl.when(e == 0)
    def _():
        xn = _rms(x_ref[...], g_ref[...])
        xnT_ref[...] = xn.T.astype(BF16)
        acc_ref[...] = jnp.zeros_like(acc_ref)

    ws = []
    for ii in range(ib):
        i = e * ib + ii
        w = None
        for h in range(H):
            s1 = pw_ref[h, 0, pl.ds(i, 1), :]
            e1 = pw_ref[h, 2, pl.ds(i, 1), :]
            sel = (s1 + pw_ref[h, 1]) >= tau_ref[h, 0:1, :]
            c = jnp.where(sel, pw_ref[h, 3], 0.0) * e1
            w = c if w is None else w + c
        ws.append(w)
    W = ws[0] if ib == 1 else jnp.concatenate(ws, axis=0)
    hT = _dot(u_ref[...], xnT_ref[...])
    aT = W * _gelu_tanh(hT)
    acc_ref[...] += _dot(aT.T.astype(BF16), v_ref[...])

    @pl.when(e == pl.num_programs(1) - 1)
    def _():
        o_ref[...] = x_ref[...] + acc_ref[...]


def peer_dense(x, g, pw, tau, u_tab, v_tab, *, tt=512, ib=2):
    T, D = x.shape
    tt = min(tt, T)
    assert T % tt == 0 and PEER_NKEYS % ib == 0
    H, NK = PEER_HEADS, PEER_NKEYS
    eb = ib * NK
    kern = functools.partial(_peer_dense_kernel, ib=ib)
    return pl.pallas_call(
        kern,
        grid=(T // tt, NK // ib),
        in_specs=[
            pl.BlockSpec((tt, D), lambda i, e: (i, 0)),
            pl.BlockSpec((1, D), lambda i, e: (0, 0)),
            pl.BlockSpec((H, 4, NK, tt), lambda i, e: (0, 0, 0, i)),
            pl.BlockSpec((H, SUBLANES, tt), lambda i, e: (0, 0, i)),
            pl.BlockSpec((eb, D), lambda i, e: (e, 0)),
            pl.BlockSpec((eb, D), lambda i, e: (e, 0)),
        ],
        out_specs=pl.BlockSpec((tt, D), lambda i, e: (i, 0)),
        out_shape=jax.ShapeDtypeStruct((T, D), F32),
        scratch_shapes=[pltpu.VMEM((D, tt), BF16), pltpu.VMEM((tt, D), F32)],
        compiler_params=_cparams(("parallel", "arbitrary")),
        name="peer_dense",
    )(x, g.reshape(1, D), pw, tau, u_tab, v_tab)


def _prep_weights(w_in_a, w_in_b, w_out, w_mem_kv, peer_wq, peer_u, peer_v, lru_wr, lru_wi):
    H = GDN_HEADS
    qkvz = GDN_QKV + TM_WIDTH
    wb = jnp.concatenate(
        [
            w_in_b[:, :, :qkvz],
            w_in_b[:, :, qkvz + 2 * H:],
            w_in_b[:, :, qkvz:qkvz + 2 * H],
            jnp.zeros(w_in_b.shape[:2] + (B_IN_PAD - w_in_b.shape[2],), w_in_b.dtype),
        ],
        axis=-1,
    )
    return dict(
        w_in_a=w_in_a.astype(BF16),
        w_in_b=wb.astype(BF16),
        w_out_tm=w_out[:, :TM_WIDTH].astype(BF16),
        w_out_xa=w_out[:, TM_WIDTH:].astype(BF16),
        w_mem_kv=w_mem_kv.astype(BF16),
        peer_wq=peer_wq.astype(BF16),
        peer_u=peer_u.astype(BF16),
        peer_v=peer_v.astype(BF16),
        lru_wr=lru_wr.astype(BF16),
        lru_wi=lru_wi.astype(BF16),
    )


def _trunk(x, mem_k, mem_v, kv_blks, lru_h0, lru_c0, gdn_S0, gdn_c0, w, wc, *, B, L, time_major):
    T = B * L
    if time_major:
        G, R, tl = 1, B, L
    else:
        G, R, tl = B, 1, min(L, 256)
    Lp = -(-L // SUBLANES) * SUBLANES

    def to_batch_major(a):
        c = a.shape[-1]
        a = a.reshape(L, B, c).transpose(1, 0, 2)
        return jnp.pad(a, ((0, 0), (0, Lp - L), (0, 0))).reshape(B * Lp, c)

    def to_time_major(a):
        c = a.shape[-1]
        return a.reshape(B, Lp, c)[:, :L].transpose(1, 0, 2).reshape(L * B, c)

    lru_h, lru_c, gdn_S, gdn_c = [], [], [], []
    for l in range(DEPTH):
        j = l // 2
        if l % 2 == 0:
            u = norm_matmul(x, w['norm_mix'][l], wc['w_in_a'][j])
            c0 = lru_c0[j]
            if time_major:
                c0k = c0.transpose(1, 0, 2).reshape(1, (CONV_W - 1) * B, TM_WIDTH)
                h0k = lru_h0[j].reshape(1, B, TM_WIDTH)
            else:
                c0k = c0
                h0k = lru_h0[j].reshape(B, 1, TM_WIDTH)
            tm, h = rglru(u, c0k, h0k, w['lru_conv_w'][j], w['lru_conv_b'][j], wc['lru_wr'][j], w['lru_br'][j],
                          wc['lru_wi'][j], w['lru_bi'][j], w['lru_lambda'][j], G=G, L=L, R=R, tl=tl)
            lru_h.append(h.reshape(B, TM_WIDTH))
            xr = u[:, TM_WIDTH:2 * TM_WIDTH]
            if time_major:
                lru_c.append(xr.reshape(L, B, TM_WIDTH)[L - (CONV_W - 1):].transpose(1, 0, 2))
            else:
                lru_c.append(xr.reshape(B, L, TM_WIDTH)[:, L - (CONV_W - 1):])
            q_blk = 2 * TM_WIDTH // XA_WIDTH
            u_bm = None
        else:
            u = norm_matmul(x, w['norm_mix'][l], wc['w_in_b'][j])
            C = min(GDN_CHUNK, Lp)
            u_bm = to_batch_major(u) if time_major else u
            o_bm, S = gdn(u_bm, gdn_c0[j], gdn_S0[j], w['gdn_conv_w'][j], w['gdn_conv_b'][j], w['gdn_a_log'][j],
                          w['gdn_dt_bias'][j], w['gdn_o_norm'][j], B=B, L=Lp, C=C, n_valid=min(L, C),
                          qkv_blk=0, z_blk=GDN_QKV // TM_WIDTH,
                          ba_blk=(GDN_QKV + TM_WIDTH + XA_WIDTH) // LANES)
            tm = to_time_major(o_bm) if time_major else o_bm
            gdn_S.append(S)
            qkv = u[:, :GDN_QKV]
            if time_major:
                gdn_c.append(qkv.reshape(L, B, GDN_QKV)[L - (CONV_W - 1):].transpose(1, 0, 2))
            else:
                gdn_c.append(qkv.reshape(B, L, GDN_QKV)[:, L - (CONV_W - 1):])
            q_blk = (GDN_QKV + TM_WIDTH) // XA_WIDTH
        if time_major:
            q_bm = to_batch_major(u[:, q_blk * XA_WIDTH:(q_blk + 1) * XA_WIDTH])
            xa = to_time_major(xattn(q_bm, mem_k[l], mem_v[l], B=B, Lq=Lp, tq=Lp, q_blk=0,
                                     k_blk=kv_blks[0], v_blk=kv_blks[1]))
        else:
            xa = xattn(u, mem_k[l], mem_v[l], B=B, Lq=L, tq=min(L, 512), q_blk=q_blk,
                       k_blk=kv_blks[0], v_blk=kv_blks[1])
        x = out_proj(tm, xa, wc['w_out_tm'][l], wc['w_out_xa'][l], x)
        q = norm_matmul(x, w['norm_ffn'][l], wc['peer_wq'][l])
        pw, tau = peer_topk(q, w['peer_subkeys'][l])
        x = peer_dense(x, w['norm_ffn'][l], pw, tau, wc['peer_u'][l], wc['peer_v'][l])
    y = rmsnorm_rows(x, w['norm_final'])
    return y, jnp.stack(lru_h), jnp.stack(lru_c), jnp.stack(gdn_S), jnp.stack(gdn_c)


def kernel(x_prompt, x_sample, state_rglru_h, state_rglru_conv, state_gdn_S, state_gdn_conv, cache_mem_k, cache_mem_v, mem_prompt, norm_mix, norm_ffn, norm_final, w_in_a, w_in_b, w_out, lru_conv_w, lru_conv_b, lru_wr, lru_br, lru_wi, lru_bi, lru_lambda, gdn_conv_w, gdn_conv_b, gdn_a_log, gdn_dt_bias, gdn_o_norm, norm_mem, w_mem_kv, peer_wq, peer_subkeys, peer_u, peer_v):
    w = dict(norm_mix=norm_mix, norm_ffn=norm_ffn, norm_final=norm_final, lru_conv_w=lru_conv_w,
             lru_conv_b=lru_conv_b, lru_br=lru_br, lru_bi=lru_bi, lru_lambda=lru_lambda, gdn_conv_w=gdn_conv_w,
             gdn_conv_b=gdn_conv_b, gdn_a_log=gdn_a_log, gdn_dt_bias=gdn_dt_bias, gdn_o_norm=gdn_o_norm,
             peer_subkeys=peer_subkeys)
    wc = _prep_weights(w_in_a, w_in_b, w_out, w_mem_kv, peer_wq, peer_u, peer_v, lru_wr, lru_wi)
    Bp, Lp_, D = x_prompt.shape
    Bs, Ls, _ = x_sample.shape
    n_a, n_b = state_rglru_h.shape[0], state_gdn_S.shape[0]

    mem2 = mem_prompt.reshape(Bp * N_MEM, D)
    kv = [norm_matmul(mem2, norm_mem[l], wc['w_mem_kv'][l]) for l in range(DEPTH)]
    y_p, p_h, p_ca, p_S, p_cb = _trunk(
        x_prompt.reshape(Bp * Lp_, D), kv, kv, (0, 1),
        jnp.zeros((n_a, Bp, TM_WIDTH), F32), jnp.zeros((n_a, Bp, CONV_W - 1, TM_WIDTH), F32),
        jnp.zeros((n_b, Bp, GDN_HEADS, HEAD_DIM, HEAD_DIM), F32), jnp.zeros((n_b, Bp, CONV_W - 1, GDN_QKV), F32),
        w, wc, B=Bp, L=Lp_, time_major=False)
    p_mem_k = jnp.stack([a[:, :XA_WIDTH].reshape(Bp, N_MEM, XA_HEADS, HEAD_DIM) for a in kv])
    p_mem_v = jnp.stack([a[:, XA_WIDTH:].reshape(Bp, N_MEM, XA_HEADS, HEAD_DIM) for a in kv])

    xs = x_sample.transpose(1, 0, 2).reshape(Ls * Bs, D)
    ck = [cache_mem_k[l].reshape(Bs * N_MEM, XA_WIDTH) for l in range(DEPTH)]
    cv = [cache_mem_v[l].reshape(Bs * N_MEM, XA_WIDTH) for l in range(DEPTH)]
    y_s, s_h, s_ca, s_S, s_cb = _trunk(xs, ck, cv, (0, 0), state_rglru_h, state_rglru_conv, state_gdn_S,
                                       state_gdn_conv, w, wc, B=Bs, L=Ls, time_major=True)
    y_s = y_s.reshape(Ls, Bs, D).transpose(1, 0, 2)
    return (y_p.reshape(Bp, Lp_, D), y_s, p_h, p_ca, p_S, p_cb, p_mem_k, p_mem_v, s_h, s_ca, s_S, s_cb)
```

```python
import functools
import math

import jax
import jax.numpy as jnp
from jax import lax
from jax.experimental import pallas as pl
from jax.experimental.pallas import tpu as pltpu

F32 = jnp.float32
BF16 = jnp.bfloat16

D_MODEL = 2048
DEPTH = 4
HEAD_DIM = 128
XA_HEADS = 4
XA_WIDTH = XA_HEADS * HEAD_DIM
TM_WIDTH = D_MODEL - XA_WIDTH
N_MEM = 256
CONV_W = 4
EPS = 1e-6
LRU_BLOCKS = TM_WIDTH // HEAD_DIM
LRU_C = 8.0
GDN_HEADS = TM_WIDTH // HEAD_DIM
GDN_QKV = 3 * TM_WIDTH
GDN_CHUNK = 64
GDN_SEQS_PER_STEP = 4
PEER_HEADS = 8
PEER_NKEYS = 128
PEER_TOPK = 16
B_IN_PAD = 7168

SUBLANES = 8
LANES = 128
VMEM_LIMIT = 52 * 1024 * 1024

NEG_INF = float("-inf")


def _cparams(sem):
    return pltpu.CompilerParams(dimension_semantics=sem, vmem_limit_bytes=VMEM_LIMIT)


def _dot(a, b, precision=None):
    return jnp.dot(a, b, preferred_element_type=F32, precision=precision)


def _mm(a, b):
    return jnp.dot(a.astype(BF16), b.astype(BF16), preferred_element_type=F32)


_F32_DOT = lax.Precision.HIGHEST


def _dot_nt(a, b, precision=None):
    return lax.dot_general(a, b, (((1,), (1,)), ((), ())), preferred_element_type=F32, precision=precision)


def _dot_tn(a, b, precision=None):
    return lax.dot_general(a, b, (((0,), (0,)), ((), ())), preferred_element_type=F32, precision=precision)


def _sigmoid(x):
    return 1.0 / (1.0 + jnp.exp(-x))


def _silu(x):
    return x * _sigmoid(x)


def _gelu_tanh(x):
    c = math.sqrt(2.0 / math.pi)
    return 0.5 * x * (1.0 + jnp.tanh(c * (x + 0.044715 * (x * x * x))))


def _gelu_tanh_x2(x):
    c = math.sqrt(2.0 / math.pi)
    return x * (1.0 + jnp.tanh(x * (c + (0.044715 * c) * (x * x))))


def _softplus(x):
    return jnp.maximum(x, 0.0) + jnp.log(1.0 + jnp.exp(-jnp.abs(x)))


def _rms(x, g):
    ms = jnp.mean(x * x, axis=-1, keepdims=True)
    return x * lax.rsqrt(ms + EPS) * g


def _norm_matmul_kernel(x_ref, g_ref, w_ref, o_ref, xn_ref):
    @pl.when(pl.program_id(1) == 0)
    def _():
        xn_ref[...] = _rms(x_ref[...], g_ref[...]).astype(BF16)

    o_ref[...] = _dot(xn_ref[...], w_ref[...])


def norm_matmul(x, g, w, *, tm=512, tn=512):
    m, k = x.shape
    n = w.shape[1]
    tm = min(tm, m)
    assert m % tm == 0 and n % tn == 0
    return pl.pallas_call(
        _norm_matmul_kernel,
        grid=(m // tm, n // tn),
        in_specs=[
            pl.BlockSpec((tm, k), lambda i, j: (i, 0)),
            pl.BlockSpec((1, k), lambda i, j: (0, 0)),
            pl.BlockSpec((k, tn), lambda i, j: (0, j)),
        ],
        out_specs=pl.BlockSpec((tm, tn), lambda i, j: (i, j)),
        out_shape=jax.ShapeDtypeStruct((m, n), F32),
        scratch_shapes=[pltpu.VMEM((tm, k), BF16)],
        compiler_params=_cparams(("parallel", "arbitrary")),
        name="norm_matmul",
    )(x, g.reshape(1, k), w)


def _out_proj_kernel(a_ref, b_ref, wa_ref, wb_ref, r_ref, o_ref):
    acc = _dot(a_ref[...], wa_ref[...]) + _dot(b_ref[...], wb_ref[...])
    o_ref[...] = r_ref[...] + acc


def out_proj(a, b, wa, wb, r, *, tm=512, tn=512):
    m, ka = a.shape
    kb = b.shape[1]
    n = wa.shape[1]
    tm = min(tm, m)
    assert m % tm == 0 and n % tn == 0
    return pl.pallas_call(
        _out_proj_kernel,
        grid=(m // tm, n // tn),
        in_specs=[
            pl.BlockSpec((tm, ka), lambda i, j: (i, 0)),
            pl.BlockSpec((tm, kb), lambda i, j: (i, 0)),
            pl.BlockSpec((ka, tn), lambda i, j: (0, j)),
            pl.BlockSpec((kb, tn), lambda i, j: (0, j)),
            pl.BlockSpec((tm, tn), lambda i, j: (i, j)),
        ],
        out_specs=pl.BlockSpec((tm, tn), lambda i, j: (i, j)),
        out_shape=jax.ShapeDtypeStruct((m, n), F32),
        compiler_params=_cparams(("parallel", "arbitrary")),
        name="out_proj",
    )(a, b, wa, wb, r)


def _rmsnorm_kernel(x_ref, g_ref, o_ref):
    o_ref[...] = _rms(x_ref[...], g_ref[...])


def rmsnorm_rows(x, g, *, tm=512):
    m, k = x.shape
    tm = min(tm, m)
    assert m % tm == 0
    return pl.pallas_call(
        _rmsnorm_kernel,
        grid=(m // tm,),
        in_specs=[pl.BlockSpec((tm, k), lambda i: (i, 0)), pl.BlockSpec((1, k), lambda i: (0, 0))],
        out_specs=pl.BlockSpec((tm, k), lambda i: (i, 0)),
        out_shape=jax.ShapeDtypeStruct((m, k), F32),
        compiler_params=_cparams(("parallel",)),
        name="final_rmsnorm",
    )(x, g.reshape(1, k))


def _rglru_kernel(gate_ref, xr_ref, cw_ref, cb_ref, c0_ref, h0_ref, wr_ref, br_ref, wi_ref, bi_ref, lam_ref,
                  y_ref, hout_ref, hist_ref, a_ref, b_ref, hs_ref, h_ref, *, tl, R):
    n = pl.program_id(1)
    rows = tl * R
    hist_rows = (CONV_W - 1) * R
    off0 = -(-hist_rows // SUBLANES) * SUBLANES

    @pl.when(n == 0)
    def _():
        hist_ref[off0 - hist_rows:off0, :] = c0_ref[0]
        h_ref[...] = h0_ref[0]

    x = xr_ref[...]
    hist_ref[off0:off0 + rows, :] = x
    xc = cb_ref[...]
    for j in range(CONV_W):
        s = off0 - hist_rows + j * R
        xc = xc + hist_ref[s:s + rows, :] * cw_ref[j:j + 1, :]
    tail = hist_ref[off0 + rows - hist_rows:off0 + rows, :]
    hist_ref[off0 - hist_rows:off0, :] = tail

    sp = _softplus(-lam_ref[...])
    for blk in range(LRU_BLOCKS):
        cs = slice(blk * HEAD_DIM, (blk + 1) * HEAD_DIM)
        xb = xc[:, cs]
        xb16 = xb.astype(BF16)
        r = _sigmoid(_dot(xb16, wr_ref[blk]) + br_ref[blk])
        i = _sigmoid(_dot(xb16, wi_ref[blk]) + bi_ref[blk])
        log_a = (-LRU_C) * r * sp[:, cs]
        a = jnp.exp(log_a)
        a_ref[:, cs] = a
        b_ref[:, cs] = jnp.sqrt(1.0 - a * a) * (i * xb)

    def step(l, carry):
        if R % SUBLANES == 0:
            rs = pl.ds(pl.multiple_of(l * R, SUBLANES), R)
        else:
            rs = pl.ds(l * R, R)
        h = a_ref[rs, :] * h_ref[...] + b_ref[rs, :]
        h_ref[...] = h
        hs_ref[rs, :] = h
        return carry

    lax.fori_loop(0, tl, step, 0)
    y_ref[...] = (hs_ref[...] * _gelu_tanh(gate_ref[...])).astype(y_ref.dtype)
    hout_ref[0] = h_ref[...]


def rglru(u, c0, h0, cw, cb, wr, br, wi, bi, lam, *, G, L, R, tl):
    W = TM_WIDTH
    assert L % tl == 0
    rows = tl * R
    nt = L // tl
    hist_rows = (CONV_W - 1) * R
    off0 = -(-hist_rows // SUBLANES) * SUBLANES
    kern = functools.partial(_rglru_kernel, tl=tl, R=R)
    const2 = lambda g, n: (0, 0)
    const3 = lambda g, n: (0, 0, 0)
    y, hout = pl.pallas_call(
        kern,
        grid=(G, nt),
        in_specs=[
            pl.BlockSpec((rows, W), lambda g, n: (g * nt + n, 0)),
            pl.BlockSpec((rows, W), lambda g, n: (g * nt + n, 1)),
            pl.BlockSpec((CONV_W, W), const2),
            pl.BlockSpec((1, W), const2),
            pl.BlockSpec((1, hist_rows, W), lambda g, n: (g, 0, 0)),
            pl.BlockSpec((1, R, W), lambda g, n: (g, 0, 0)),
            pl.BlockSpec((LRU_BLOCKS, HEAD_DIM, HEAD_DIM), const3),
            pl.BlockSpec((LRU_BLOCKS, 1, HEAD_DIM), const3),
            pl.BlockSpec((LRU_BLOCKS, HEAD_DIM, HEAD_DIM), const3),
            pl.BlockSpec((LRU_BLOCKS, 1, HEAD_DIM), const3),
            pl.BlockSpec((1, W), const2),
        ],
        out_specs=[
            pl.BlockSpec((rows, W), lambda g, n: (g * nt + n, 0)),
            pl.BlockSpec((1, R, W), lambda g, n: (g, 0, 0)),
        ],
        out_shape=[
            jax.ShapeDtypeStruct((G * L * R, W), BF16),
            jax.ShapeDtypeStruct((G, R, W), F32),
        ],
        scratch_shapes=[
            pltpu.VMEM((off0 + rows, W), F32),
            pltpu.VMEM((rows, W), F32),
            pltpu.VMEM((rows, W), F32),
            pltpu.VMEM((rows, W), F32),
            pltpu.VMEM((R, W), F32),
        ],
        compiler_params=_cparams(("parallel", "arbitrary")),
        name="rglru",
    )(u, u, cw, cb.reshape(1, W), c0, h0, wr, br.reshape(LRU_BLOCKS, 1, HEAD_DIM), wi,
      bi.reshape(LRU_BLOCKS, 1, HEAD_DIM), lam.reshape(1, W))
    return y, hout


def _inv_unit_lower_many(As, C, row, col):
    eye = (row == col).astype(F32)
    base = min(SUBLANES, C)
    sh = int(math.log2(base))
    dmask = (row >> sh) == (col >> sh)
    Ps = [jnp.where(dmask, A, 0.0) for A in As]
    Ts = [eye - P for P in Ps]
    span = 2
    while span < base:
        Ps = [_dot(P, P, _F32_DOT) for P in Ps]
        Ts = [_dot(T, eye + P, _F32_DOT) for T, P in zip(Ts, Ps)]
        span *= 2
    s = base
    while s < C:
        sh = int(math.log2(s))
        off = ((row >> (sh + 1)) == (col >> (sh + 1))) & ((row >> sh) != (col >> sh))
        Ms = [_dot(jnp.where(off, A, 0.0), T, _F32_DOT) for A, T in zip(As, Ts)]
        Ts = [T - _dot(T, M, _F32_DOT) for T, M in zip(Ts, Ms)]
        s *= 2
    return Ts


def _gdn_kernel(qkv_ref, z_ref, ba_ref, cw_ref, cb_ref, c0_ref, S0_ref, alog_ref, dtb_ref, onorm_ref,
                *rest, C, n_valid, nb):
    o_ref, S_ref, hist_ref = rest[-3:]
    n = pl.program_id(1)
    H, Dh = GDN_HEADS, HEAD_DIM
    hist_rows = CONV_W - 1
    off0 = SUBLANES
    rows = nb * C

    @pl.when(n == 0)
    def _():
        for s in range(nb):
            hist_ref[s, off0 - hist_rows:off0, :] = c0_ref[s]
        S_ref[...] = S0_ref[...]

    ts = []
    for s in range(nb):
        hist_ref[s, off0:off0 + C, :] = qkv_ref[s * C:(s + 1) * C, :]
        xc = cb_ref[...]
        for j in range(CONV_W):
            r0 = off0 - hist_rows + j
            xc = xc + hist_ref[s, r0:r0 + C, :] * cw_ref[j:j + 1, :]
        tail = hist_ref[s, off0 + C - hist_rows:off0 + C, :]
        hist_ref[s, off0 - hist_rows:off0, :] = tail
        ts.append(_silu(xc))

    ba = ba_ref[...]
    beta_all = _sigmoid(ba)
    g_all = -jnp.exp(alog_ref[...]) * _softplus(ba + dtb_ref[...])
    if n_valid < C:
        rmask = (lax.broadcasted_iota(jnp.int32, (rows, LANES), 0) & (C - 1)) < n_valid
        beta_all = jnp.where(rmask, beta_all, 0.0)
        g_all = jnp.where(rmask, g_all, 0.0)
    if rows < LANES:
        g_pad = jnp.concatenate([g_all, jnp.zeros((LANES - rows, LANES), F32)], axis=0)
    else:
        g_pad = g_all
    r128 = lax.broadcasted_iota(jnp.int32, (LANES, LANES), 0)
    c128 = lax.broadcasted_iota(jnp.int32, (LANES, LANES), 1)
    shc = int(math.log2(C))
    tril = ((c128 <= r128) & ((c128 >> shc) == (r128 >> shc))).astype(F32)
    gc_pad = _dot(tril, g_pad, precision=lax.Precision.HIGHEST)
    gcT = gc_pad.T

    row = lax.broadcasted_iota(jnp.int32, (C, C), 0)
    col = lax.broadcasted_iota(jnp.int32, (C, C), 1)
    incl = col <= row
    strict = col < row

    probs = [(s, h) for s in range(nb) for h in range(H)]
    qn, kn, kn16, vh, bcol, gcol, decay, egc = {}, {}, {}, {}, {}, {}, {}, {}
    for p in probs:
        s, h = p
        t = ts[s]
        qh = t[:, h * Dh:(h + 1) * Dh]
        kh = t[:, TM_WIDTH + h * Dh:TM_WIDTH + (h + 1) * Dh]
        vh[p] = t[:, 2 * TM_WIDTH + h * Dh:2 * TM_WIDTH + (h + 1) * Dh]
        qn[p] = qh * lax.rsqrt(jnp.sum(qh * qh, -1, keepdims=True) + EPS) * (Dh ** -0.5)
        kn[p] = kh * lax.rsqrt(jnp.sum(kh * kh, -1, keepdims=True) + EPS)
        kn16[p] = kn[p].astype(BF16)
        bcol[p] = beta_all[s * C:(s + 1) * C, h:h + 1]
        gcol[p] = gc_pad[s * C:(s + 1) * C, H + h:H + h + 1]
        grow = gcT[H + h:H + h + 1, s * C:(s + 1) * C]
        diff = gcol[p] - grow
        decay[p] = jnp.where(incl, jnp.exp(jnp.where(incl, diff, 0.0)), 0.0)
        egc[p] = jnp.exp(gcol[p])
    kk = {p: _dot_nt(kn16[p], kn16[p]) for p in probs}
    qkr = {p: _dot_nt(qn[p].astype(BF16), kn16[p]) for p in probs}
    As = [jnp.where(strict, bcol[p] * kk[p] * decay[p], 0.0) for p in probs]
    Ts = _inv_unit_lower_many(As, C, row, col)
    rhs = [jnp.concatenate([bcol[p] * vh[p], (bcol[p] * egc[p]) * kn[p]], axis=1) for p in probs]
    sol = {p: _dot(T, r, _F32_DOT) for p, T, r in zip(probs, Ts, rhs)}
    S = {p: S_ref[p[0], p[1]] for p in probs}
    S16 = {p: S[p].astype(BF16) for p in probs}
    kS = {p: _dot(sol[p][:, Dh:].astype(BF16), S16[p]) for p in probs}
    qS = {p: _dot((qn[p] * egc[p]).astype(BF16), S16[p]) for p in probs}
    u_new = {p: (sol[p][:, :Dh] - kS[p]).astype(BF16) for p in probs}
    qk16 = {p: jnp.where(incl, qkr[p] * decay[p], 0.0).astype(BF16) for p in probs}
    o = {p: qS[p] + _dot(qk16[p], u_new[p]) for p in probs}
    dS = {}
    for p in probs:
        g_last = gcol[p][C - 1:C, :]
        kdec = (kn[p] * jnp.exp(g_last - gcol[p])).astype(BF16)
        dS[p] = _dot_tn(kdec, u_new[p])
    for p in probs:
        s, h = p
        g_last = gcol[p][C - 1:C, :]
        S_ref[s, h] = S[p] * jnp.exp(g_last) + dS[p]
        on = o[p] * lax.rsqrt(jnp.mean(o[p] * o[p], -1, keepdims=True) + EPS) * onorm_ref[...]
        zs = z_ref[s * C:(s + 1) * C, h * Dh:(h + 1) * Dh]
        o_ref[s * C:(s + 1) * C, h * Dh:(h + 1) * Dh] = (on * _silu(zs)).astype(o_ref.dtype)


def gdn(u, c0, S0_all, S_out_prev, cw, cb, a_log, dt_bias, o_norm, *, layer, B, L, C, n_valid, nb, qkv_blk, z_blk,
        ba_blk):
    H, Dh = GDN_HEADS, HEAD_DIM
    nc = L // C
    assert B % nb == 0 and (nb == 1 or nc == 1) and nb * C <= LANES
    rows = nb * C
    soff = layer * (B // nb)
    alog = jnp.zeros((1, LANES), F32).at[0, H:2 * H].set(a_log)
    dtb = jnp.zeros((1, LANES), F32).at[0, H:2 * H].set(dt_bias)
    kern = functools.partial(_gdn_kernel, C=C, n_valid=n_valid, nb=nb)
    const2 = lambda b, n: (0, 0)
    in_specs = [
        pl.BlockSpec((rows, GDN_QKV), lambda b, n: (b * nc + n, qkv_blk)),
        pl.BlockSpec((rows, TM_WIDTH), lambda b, n: (b * nc + n, z_blk)),
        pl.BlockSpec((rows, LANES), lambda b, n: (b * nc + n, ba_blk)),
        pl.BlockSpec((CONV_W, GDN_QKV), const2),
        pl.BlockSpec((1, GDN_QKV), const2),
        pl.BlockSpec((nb, CONV_W - 1, GDN_QKV), lambda b, n: (b, 0, 0)),
        pl.BlockSpec((nb, H, Dh, Dh), lambda b, n: (soff + b, 0, 0, 0)),
        pl.BlockSpec((1, LANES), const2),
        pl.BlockSpec((1, LANES), const2),
        pl.BlockSpec((1, Dh), const2),
    ]
    args = [u, u, u, cw, cb.reshape(1, GDN_QKV), c0, S0_all, alog, dtb, o_norm.reshape(1, Dh)]
    aliases = {}
    if S_out_prev is not None:
        in_specs.append(pl.BlockSpec(memory_space=pl.ANY))
        args.append(S_out_prev)
        aliases = {len(args) - 1: 1}
    o, S = pl.pallas_call(
        kern,
        grid=(B // nb, nc),
        in_specs=in_specs,
        out_specs=[
            pl.BlockSpec((rows, TM_WIDTH), lambda b, n: (b * nc + n, 0)),
            pl.BlockSpec((nb, H, Dh, Dh), lambda b, n: (soff + b, 0, 0, 0)),
        ],
        out_shape=[
            jax.ShapeDtypeStruct((B * L, TM_WIDTH), BF16),
            jax.ShapeDtypeStruct(S0_all.shape, F32),
        ],
        scratch_shapes=[pltpu.VMEM((nb, SUBLANES + C, GDN_QKV), F32)],
        input_output_aliases=aliases,
        compiler_params=_cparams(("parallel", "arbitrary")),
        name="gdn",
    )(*args)
    return o, S


def _xattn_kernel(q_ref, k_ref, v_ref, o_ref, *, heads_major):
    scale = HEAD_DIM ** -0.5
    for h in range(XA_HEADS):
        cs = slice(h * HEAD_DIM, (h + 1) * HEAD_DIM)
        qh = q_ref[:, cs].astype(BF16)
        if heads_major:
            kh = k_ref[:, h, :].astype(BF16)
            vh = v_ref[:, h, :].astype(BF16)
        else:
            kh = k_ref[:, cs].astype(BF16)
            vh = v_ref[:, cs].astype(BF16)
        s = _dot_nt(qh, kh) * scale
        m = jnp.max(s, axis=-1, keepdims=True)
        p = jnp.exp(s - m)
        p = p / jnp.sum(p, axis=-1, keepdims=True)
        o_ref[:, cs] = _dot(p.astype(BF16), vh).astype(o_ref.dtype)


def xattn(q, k, v, *, B, Lq, tq, q_blk, k_blk=0, v_blk=0, layer=None):
    nq = Lq // tq
    if layer is None:
        kv_specs = [
            pl.BlockSpec((N_MEM, XA_WIDTH), lambda b, n: (b, k_blk)),
            pl.BlockSpec((N_MEM, XA_WIDTH), lambda b, n: (b, v_blk)),
        ]
    else:
        spec = pl.BlockSpec((None, None, N_MEM, XA_HEADS, HEAD_DIM), lambda b, n: (layer, b, 0, 0, 0))
        kv_specs = [spec, spec]
    return pl.pallas_call(
        functools.partial(_xattn_kernel, heads_major=layer is not None),
        grid=(B, nq),
        in_specs=[pl.BlockSpec((tq, XA_WIDTH), lambda b, n: (b * nq + n, q_blk))] + kv_specs,
        out_specs=pl.BlockSpec((tq, XA_WIDTH), lambda b, n: (b * nq + n, 0)),
        out_shape=jax.ShapeDtypeStruct((B * Lq, XA_WIDTH), BF16),
        compiler_params=_cparams(("parallel", "arbitrary")),
        name="xattn",
    )(q, k, v)


PEER_RANKS = PEER_TOPK + 1
GATE_ROWS = 32


def _cand_counts():
    return [PEER_RANKS // (a + 1) for a in range(PEER_RANKS)]


def _peer_topk_kernel(q_ref, sk_ref, pw_ref, aux_ref, sv_ref, cand_ref):
    K = PEER_TOPK
    q = q_ref[...]
    s = []
    for p in range(2):
        qp = q[:, p * LANES:(p + 1) * LANES]
        s.append(_dot_nt(sk_ref[0, p], qp, precision=lax.Precision.HIGHEST))
    for p in range(2):
        cur = s[p]
        for r in range(PEER_RANKS):
            m = jnp.max(cur, axis=0, keepdims=True)
            sv_ref[p, r:r + 1, :] = m
            cur = jnp.where(cur == m, NEG_INF, cur)
    sv1 = sv_ref[0, 0:PEER_RANKS, :]
    sv2 = sv_ref[1, 0:PEER_RANKS, :]
    cand_ref[...] = jnp.full(cand_ref.shape, NEG_INF, F32)
    off = 0
    for a, nb in enumerate(_cand_counts()):
        cand_ref[off:off + nb, :] = sv1[a:a + 1, :] + sv2[0:nb, :]
        off += nb
    cur = cand_ref[...]
    tops = []
    for r in range(PEER_RANKS):
        m = jnp.max(cur, axis=0, keepdims=True)
        tops.append(m)
        cur = jnp.where(cur == m, NEG_INF, cur)
    z = jnp.ones_like(tops[0])
    for r in range(1, K):
        z = z + jnp.exp(tops[r] - tops[0])
    thr = 0.5 * (tops[K - 1] + tops[K])
    pw_ref[0, 0] = thr - s[0]
    pw_ref[0, 1] = s[1]
    pw_ref[0, 2] = jnp.exp(s[1] - sv2[0:1, :])
    aux_ref[0] = jnp.broadcast_to(thr - sv1[0:1, :] - jnp.log(z), aux_ref.shape[1:])


def peer_topk(q, subkeys, *, tt=512):
    T = q.shape[0]
    tt = min(tt, T)
    assert T % tt == 0
    H, NK = PEER_HEADS, PEER_NKEYS
    pad8 = lambda n: -(-n // SUBLANES) * SUBLANES
    return pl.pallas_call(
        _peer_topk_kernel,
        grid=(T // tt, H),
        in_specs=[
            pl.BlockSpec((tt, 2 * LANES), lambda i, h: (i, h)),
            pl.BlockSpec((1, 2, NK, LANES), lambda i, h: (h, 0, 0, 0)),
        ],
        out_specs=[
            pl.BlockSpec((1, 3, NK, tt), lambda i, h: (h, 0, 0, i)),
            pl.BlockSpec((1, SUBLANES, tt), lambda i, h: (h, 0, i)),
        ],
        out_shape=[
            jax.ShapeDtypeStruct((H, 3, NK, T), F32),
            jax.ShapeDtypeStruct((H, SUBLANES, T), F32),
        ],
        scratch_shapes=[pltpu.VMEM((2, pad8(PEER_RANKS), tt), F32), pltpu.VMEM((pad8(sum(_cand_counts())), tt), F32)],
        compiler_params=_cparams(("parallel", "arbitrary")),
        name="peer_topk",
    )(q, subkeys)


def _peer_dense_kernel(x_ref, g_ref, pw_ref, aux_ref, u_ref, v_ref, o_ref, xnT_ref, acc_ref, w_ref, *, ib):
    e = pl.program_id(1)
    H, NK = PEER_HEADS, PEER_NKEYS
    tt = x_ref.shape[0]
    strip = min(LANES, tt)
    assert tt % strip == 0

    def gates(blk, ii, dst):
        i = blk * ib + ii
        thr1 = [pw_ref[h, 0, pl.ds(i, 1), :] for h in range(H)]
        e1 = [0.5 * jnp.exp(aux_ref[h, 0:1, :] - thr1[h]) for h in range(H)]
        for tc in range(tt // strip):
            ts_ = slice(tc * strip, (tc + 1) * strip)
            for j0 in range(0, NK, GATE_ROWS):
                js = slice(j0, j0 + GATE_ROWS)
                w = None
                for h in range(H):
                    sel = pw_ref[h, 1, js, ts_] >= thr1[h][:, ts_]
                    c = jnp.where(sel, pw_ref[h, 2, js, ts_], 0.0) * e1[h][:, ts_]
                    w = c if w is None else w + c
                dst[ii * NK + j0:ii * NK + j0 + GATE_ROWS, ts_] = w

    @pl.when(e == 0)
    def _():
        xn = _rms(x_ref[...], g_ref[...])
        xnT_ref[...] = xn.T.astype(BF16)
        acc_ref[...] = jnp.zeros_like(acc_ref)

    hT = _dot(u_ref[...], xnT_ref[...])
    for ii in range(ib):
        gates(e, ii, w_ref)
    aT = w_ref[...] * _gelu_tanh_x2(hT)
    acc_ref[...] += _dot(aT.T.astype(BF16), v_ref[...])

    @pl.when(e == pl.num_programs(1) - 1)
    def _():
        o_ref[...] = x_ref[...] + acc_ref[...]


def peer_dense(x, g, pw, aux, u_tab, v_tab, *, layer=0, tt=512, ib=4):
    T, D = x.shape
    boff = layer * (PEER_NKEYS // ib)
    tt = min(tt, T)
    assert T % tt == 0 and PEER_NKEYS % ib == 0
    H, NK = PEER_HEADS, PEER_NKEYS
    eb = ib * NK
    kern = functools.partial(_peer_dense_kernel, ib=ib)
    return pl.pallas_call(
        kern,
        grid=(T // tt, NK // ib),
        in_specs=[
            pl.BlockSpec((tt, D), lambda i, e: (i, 0)),
            pl.BlockSpec((1, D), lambda i, e: (0, 0)),
            pl.BlockSpec((H, 3, NK, tt), lambda i, e: (0, 0, 0, i)),
            pl.BlockSpec((H, SUBLANES, tt), lambda i, e: (0, 0, i)),
            pl.BlockSpec((eb, D), lambda i, e: (boff + e, 0)),
            pl.BlockSpec((eb, D), lambda i, e: (boff + e, 0)),
        ],
        out_specs=pl.BlockSpec((tt, D), lambda i, e: (i, 0)),
        out_shape=jax.ShapeDtypeStruct((T, D), F32),
        scratch_shapes=[
            pltpu.VMEM((D, tt), BF16),
            pltpu.VMEM((tt, D), F32),
            pltpu.VMEM((eb, tt), F32),
        ],
        compiler_params=_cparams(("parallel", "arbitrary")),
        name="peer_dense",
    )(x, g.reshape(1, D), pw, aux, u_tab, v_tab)


def _prep_weights(w_in_a, w_in_b, w_out, w_mem_kv, peer_wq, peer_u, peer_v, lru_wr, lru_wi):
    H = GDN_HEADS
    qkvz = GDN_QKV + TM_WIDTH
    wb = jnp.concatenate(
        [
            w_in_b[:, :, :qkvz],
            w_in_b[:, :, qkvz + 2 * H:],
            w_in_b[:, :, qkvz:qkvz + 2 * H],
            jnp.zeros(w_in_b.shape[:2] + (B_IN_PAD - w_in_b.shape[2],), w_in_b.dtype),
        ],
        axis=-1,
    )
    return dict(
        w_in_a=w_in_a.astype(BF16),
        w_in_b=wb.astype(BF16),
        w_out_tm=w_out[:, :TM_WIDTH].astype(BF16),
        w_out_xa=w_out[:, TM_WIDTH:].astype(BF16),
        w_mem_kv=w_mem_kv.astype(BF16),
        peer_wq=peer_wq.astype(BF16),
        peer_u=peer_u.astype(BF16).reshape(-1, peer_u.shape[-1]),
        peer_v=peer_v.astype(BF16).reshape(-1, peer_v.shape[-1]),
        lru_wr=lru_wr.astype(BF16),
        lru_wi=lru_wi.astype(BF16),
    )


def _trunk(x, mem, lru_h0, lru_c0, gdn_S0, gdn_c0, w, wc, *, B, L, time_major):
    if time_major:
        G, R, tl = 1, B, L
    else:
        G, R, tl = B, 1, min(L, 256)
    Lp = -(-L // SUBLANES) * SUBLANES
    hist = CONV_W - 1

    def to_batch_major(a):
        c = a.shape[-1]
        a = a.reshape(L, B, c).transpose(1, 0, 2)
        return jnp.pad(a, ((0, 0), (0, Lp - L), (0, 0))).reshape(B * Lp, c)

    def to_time_major(a):
        c = a.shape[-1]
        return a.reshape(B, Lp, c)[:, :L].transpose(1, 0, 2).reshape(L * B, c)

    def last_rows(u, c0, c1):
        if time_major:
            return u.reshape(L, B, -1)[L - hist:, :, c0:c1].transpose(1, 0, 2)
        return u.reshape(B, L, -1)[:, L - hist:, c0:c1]

    n_b = gdn_S0.shape[0]
    S0_all = gdn_S0.reshape((n_b * B,) + gdn_S0.shape[2:])
    S_all = None
    lru_h, lru_c, gdn_c = [], [], []
    for l in range(DEPTH):
        j = l // 2
        if l % 2 == 0:
            u = norm_matmul(x, w['norm_mix'][l], wc['w_in_a'][j])
            c0 = lru_c0[j]
            if time_major:
                c0k = c0.transpose(1, 0, 2).reshape(1, hist * B, TM_WIDTH)
                h0k = lru_h0[j].reshape(1, B, TM_WIDTH)
            else:
                c0k = c0
                h0k = lru_h0[j].reshape(B, 1, TM_WIDTH)
            tm, h = rglru(u, c0k, h0k, w['lru_conv_w'][j], w['lru_conv_b'][j], wc['lru_wr'][j], w['lru_br'][j],
                          wc['lru_wi'][j], w['lru_bi'][j], w['lru_lambda'][j], G=G, L=L, R=R, tl=tl)
            lru_h.append(h.reshape(B, TM_WIDTH))
            lru_c.append(last_rows(u, TM_WIDTH, 2 * TM_WIDTH))
            q_blk = 2 * TM_WIDTH // XA_WIDTH
        else:
            u = norm_matmul(x, w['norm_mix'][l], wc['w_in_b'][j])
            C = min(GDN_CHUNK, Lp)
            nb = GDN_SEQS_PER_STEP if (Lp == C and B % GDN_SEQS_PER_STEP == 0) else 1
            u_bm = to_batch_major(u) if time_major else u
            o_bm, S_all = gdn(u_bm, gdn_c0[j], S0_all, S_all, w['gdn_conv_w'][j], w['gdn_conv_b'][j],
                              w['gdn_a_log'][j], w['gdn_dt_bias'][j], w['gdn_o_norm'][j], layer=j, B=B, L=Lp, C=C,
                              n_valid=min(L, C), nb=nb, qkv_blk=0, z_blk=GDN_QKV // TM_WIDTH,
                              ba_blk=(GDN_QKV + TM_WIDTH + XA_WIDTH) // LANES)
            tm = to_time_major(o_bm) if time_major else o_bm
            gdn_c.append(last_rows(u, 0, GDN_QKV))
            q_blk = (GDN_QKV + TM_WIDTH) // XA_WIDTH
        if time_major:
            q_bm = to_batch_major(u[:, q_blk * XA_WIDTH:(q_blk + 1) * XA_WIDTH])
            xa = to_time_major(xattn(q_bm, mem[0], mem[1], B=B, Lq=Lp, tq=Lp, q_blk=0, layer=l))
        else:
            xa = xattn(u, mem[l], mem[l], B=B, Lq=L, tq=min(L, 512), q_blk=q_blk, k_blk=0, v_blk=1)
        x = out_proj(tm, xa, wc['w_out_tm'][l], wc['w_out_xa'][l], x)
        q = norm_matmul(x, w['norm_ffn'][l], wc['peer_wq'][l])
        pw, aux = peer_topk(q, w['peer_subkeys'][l])
        x = peer_dense(x, w['norm_ffn'][l], pw, aux, wc['peer_u'], wc['peer_v'], layer=l)
    y = rmsnorm_rows(x, w['norm_final'])
    return y, jnp.stack(lru_h), jnp.stack(lru_c), S_all.reshape(gdn_S0.shape), jnp.stack(gdn_c)


def kernel(x_prompt, x_sample, state_rglru_h, state_rglru_conv, state_gdn_S, state_gdn_conv, cache_mem_k, cache_mem_v, mem_prompt, norm_mix, norm_ffn, norm_final, w_in_a, w_in_b, w_out, lru_conv_w, lru_conv_b, lru_wr, lru_br, lru_wi, lru_bi, lru_lambda, gdn_conv_w, gdn_conv_b, gdn_a_log, gdn_dt_bias, gdn_o_norm, norm_mem, w_mem_kv, peer_wq, peer_subkeys, peer_u, peer_v):
    w = dict(norm_mix=norm_mix, norm_ffn=norm_ffn, norm_final=norm_final, lru_conv_w=lru_conv_w,
             lru_conv_b=lru_conv_b, lru_br=lru_br, lru_bi=lru_bi, lru_lambda=lru_lambda, gdn_conv_w=gdn_conv_w,
             gdn_conv_b=gdn_conv_b, gdn_a_log=gdn_a_log, gdn_dt_bias=gdn_dt_bias, gdn_o_norm=gdn_o_norm,
             peer_subkeys=peer_subkeys)
    wc = _prep_weights(w_in_a, w_in_b, w_out, w_mem_kv, peer_wq, peer_u, peer_v, lru_wr, lru_wi)
    Bp, Lp_, D = x_prompt.shape
    Bs, Ls, _ = x_sample.shape
    n_a, n_b = state_rglru_h.shape[0], state_gdn_S.shape[0]

    mem2 = mem_prompt.reshape(Bp * N_MEM, D)
    kv = [norm_matmul(mem2, norm_mem[l], wc['w_mem_kv'][l]) for l in range(DEPTH)]
    y_p, p_h, p_ca, p_S, p_cb = _trunk(
        x_prompt.reshape(Bp * Lp_, D), kv,
        jnp.zeros((n_a, Bp, TM_WIDTH), F32), jnp.zeros((n_a, Bp, CONV_W - 1, TM_WIDTH), F32),
        jnp.zeros((n_b, Bp, GDN_HEADS, HEAD_DIM, HEAD_DIM), F32), jnp.zeros((n_b, Bp, CONV_W - 1, GDN_QKV), F32),
        w, wc, B=Bp, L=Lp_, time_major=False)
    p_mem_k = jnp.stack([a[:, :XA_WIDTH].reshape(Bp, N_MEM, XA_HEADS, HEAD_DIM) for a in kv])
    p_mem_v = jnp.stack([a[:, XA_WIDTH:].reshape(Bp, N_MEM, XA_HEADS, HEAD_DIM) for a in kv])

    xs = x_sample.transpose(1, 0, 2).reshape(Ls * Bs, D)
    y_s, s_h, s_ca, s_S, s_cb = _trunk(xs, (cache_mem_k, cache_mem_v), state_rglru_h, state_rglru_conv,
                                       state_gdn_S, state_gdn_conv, w, wc, B=Bs, L=Ls, time_major=True)
    y_s = y_s.reshape(Ls, Bs, D).transpose(1, 0, 2)
    return (y_p.reshape(Bp, Lp_, D), y_s, p_h, p_ca, p_S, p_cb, p_mem_k, p_mem_v, s_h, s_ca, s_S, s_cb)
```

```python
import functools
import math

import jax
import jax.numpy as jnp
from jax import lax
from jax.experimental import pallas as pl
from jax.experimental.pallas import tpu as pltpu

F32 = jnp.float32
BF16 = jnp.bfloat16

D_MODEL = 2048
DEPTH = 4
HEAD_DIM = 128
XA_HEADS = 4
XA_WIDTH = XA_HEADS * HEAD_DIM
TM_WIDTH = D_MODEL - XA_WIDTH
N_MEM = 256
CONV_W = 4
EPS = 1e-6
LRU_BLOCKS = TM_WIDTH // HEAD_DIM
LRU_C = 8.0
GDN_HEADS = TM_WIDTH // HEAD_DIM
GDN_QKV = 3 * TM_WIDTH
GDN_CHUNK = 64
GDN_SEQS_PER_STEP = 4
PEER_HEADS = 8
PEER_NKEYS = 128
PEER_TOPK = 16
B_IN_PAD = 7168

SUBLANES = 8
LANES = 128
VMEM_LIMIT = 52 * 1024 * 1024

NEG_INF = float("-inf")


def _cparams(sem):
    return pltpu.CompilerParams(dimension_semantics=sem, vmem_limit_bytes=VMEM_LIMIT)


def _dot(a, b, precision=None):
    return jnp.dot(a, b, preferred_element_type=F32, precision=precision)


def _mm(a, b):
    return jnp.dot(a.astype(BF16), b.astype(BF16), preferred_element_type=F32)


def _split_bf16(a):
    hi = a.astype(BF16)
    return hi, (a - hi.astype(F32)).astype(BF16)


def _dot3(a, b):
    ah, al = _split_bf16(a)
    bh, bl = _split_bf16(b)
    return _dot(ah, bh) + (_dot(ah, bl) + _dot(al, bh))


def _dot_nt(a, b, precision=None):
    return lax.dot_general(a, b, (((1,), (1,)), ((), ())), preferred_element_type=F32, precision=precision)


def _dot_tn(a, b, precision=None):
    return lax.dot_general(a, b, (((0,), (0,)), ((), ())), preferred_element_type=F32, precision=precision)


def _sigmoid(x):
    return 1.0 / (1.0 + jnp.exp(-x))


def _silu(x):
    return x * _sigmoid(x)


def _gelu_tanh(x):
    c = math.sqrt(2.0 / math.pi)
    return 0.5 * x * (1.0 + jnp.tanh(c * (x + 0.044715 * (x * x * x))))


def _gelu_tanh_x2(x):
    c = math.sqrt(2.0 / math.pi)
    return x * (1.0 + jnp.tanh(x * (c + (0.044715 * c) * (x * x))))


def _softplus(x):
    return jnp.maximum(x, 0.0) + jnp.log(1.0 + jnp.exp(-jnp.abs(x)))


def _rms(x, g):
    ms = jnp.mean(x * x, axis=-1, keepdims=True)
    return x * lax.rsqrt(ms + EPS) * g


def _norm_matmul_kernel(x_ref, g_ref, w_ref, o_ref, *rest, side):
    xn_ref = rest[-1]

    @pl.when(pl.program_id(1) == 0)
    def _():
        xn_ref[...] = _rms(x_ref[...], g_ref[...]).astype(BF16)

    acc = _dot(xn_ref[...], w_ref[...])
    o_ref[...] = acc
    if side is not None:
        tile, off, width = side

        @pl.when(pl.program_id(1) == tile)
        def _():
            rest[0][...] = acc[:, off:off + width]


def _matmul_tiles(m, n, tm, tn):
    tm = min(tm, m)
    while m % tm:
        tm //= 2
    while n % tn:
        tn //= 2
    return tm, tn


def norm_matmul(x, g, w, *, side_cols=None, tm=1024, tn=1024):
    m, k = x.shape
    n = w.shape[1]
    tm, tn = _matmul_tiles(m, n, tm, tn)
    assert tm % SUBLANES == 0 and tn % LANES == 0
    out_specs = [pl.BlockSpec((tm, tn), lambda i, j: (i, j))]
    out_shape = [jax.ShapeDtypeStruct((m, n), F32)]
    side = None
    if side_cols is not None:
        start, width = side_cols
        side = (start // tn, start % tn, width)
        assert start % tn + width <= tn
        out_specs.append(pl.BlockSpec((tm, width), lambda i, j: (i, 0)))
        out_shape.append(jax.ShapeDtypeStruct((m, width), F32))
    outs = pl.pallas_call(
        functools.partial(_norm_matmul_kernel, side=side),
        grid=(m // tm, n // tn),
        in_specs=[
            pl.BlockSpec((tm, k), lambda i, j: (i, 0)),
            pl.BlockSpec((1, k), lambda i, j: (0, 0)),
            pl.BlockSpec((k, tn), lambda i, j: (0, j)),
        ],
        out_specs=out_specs,
        out_shape=out_shape,
        scratch_shapes=[pltpu.VMEM((tm, k), BF16)],
        compiler_params=_cparams(("parallel", "arbitrary")),
        name="norm_matmul",
    )(x, g.reshape(1, k), w)
    return outs[0] if side_cols is None else tuple(outs)


def _out_proj_kernel(a_ref, b_ref, wa_ref, wb_ref, r_ref, o_ref):
    acc = _dot(a_ref[...], wa_ref[...]) + _dot(b_ref[...], wb_ref[...])
    o_ref[...] = r_ref[...] + acc


def out_proj(a, b, wa, wb, r, *, tm=1024, tn=1024):
    m, ka = a.shape
    kb = b.shape[1]
    n = wa.shape[1]
    tm, tn = _matmul_tiles(m, n, tm, tn)
    assert tm % SUBLANES == 0 and tn % LANES == 0
    return pl.pallas_call(
        _out_proj_kernel,
        grid=(m // tm, n // tn),
        in_specs=[
            pl.BlockSpec((tm, ka), lambda i, j: (i, 0)),
            pl.BlockSpec((tm, kb), lambda i, j: (i, 0)),
            pl.BlockSpec((ka, tn), lambda i, j: (0, j)),
            pl.BlockSpec((kb, tn), lambda i, j: (0, j)),
            pl.BlockSpec((tm, tn), lambda i, j: (i, j)),
        ],
        out_specs=pl.BlockSpec((tm, tn), lambda i, j: (i, j)),
        out_shape=jax.ShapeDtypeStruct((m, n), F32),
        compiler_params=_cparams(("parallel", "arbitrary")),
        name="out_proj",
    )(a, b, wa, wb, r)


def _rmsnorm_kernel(x_ref, g_ref, o_ref):
    o_ref[...] = _rms(x_ref[...], g_ref[...])


def rmsnorm_rows(x, g, *, tm=512):
    m, k = x.shape
    tm = min(tm, m)
    assert m % tm == 0
    return pl.pallas_call(
        _rmsnorm_kernel,
        grid=(m // tm,),
        in_specs=[pl.BlockSpec((tm, k), lambda i: (i, 0)), pl.BlockSpec((1, k), lambda i: (0, 0))],
        out_specs=pl.BlockSpec((tm, k), lambda i: (i, 0)),
        out_shape=jax.ShapeDtypeStruct((m, k), F32),
        compiler_params=_cparams(("parallel",)),
        name="final_rmsnorm",
    )(x, g.reshape(1, k))


def _rglru_kernel(gate_ref, xr_ref, cw_ref, cb_ref, c0_ref, h0_ref, wr_ref, br_ref, wi_ref, bi_ref, lam_ref,
                  y_ref, hout_ref, hist_ref, a_ref, b_ref, hs_ref, h_ref, *, tl, R):
    n = pl.program_id(1)
    rows = tl * R
    hist_rows = (CONV_W - 1) * R
    off0 = -(-hist_rows // SUBLANES) * SUBLANES

    @pl.when(n == 0)
    def _():
        hist_ref[off0 - hist_rows:off0, :] = c0_ref[0]
        h_ref[...] = h0_ref[0]

    x = xr_ref[...]
    hist_ref[off0:off0 + rows, :] = x
    xc = cb_ref[...]
    for j in range(CONV_W):
        s = off0 - hist_rows + j * R
        xc = xc + hist_ref[s:s + rows, :] * cw_ref[j:j + 1, :]
    tail = hist_ref[off0 + rows - hist_rows:off0 + rows, :]
    hist_ref[off0 - hist_rows:off0, :] = tail

    sp = _softplus(-lam_ref[...])
    for blk in range(LRU_BLOCKS):
        cs = slice(blk * HEAD_DIM, (blk + 1) * HEAD_DIM)
        xb = xc[:, cs]
        xb16 = xb.astype(BF16)
        r = _sigmoid(_dot(xb16, wr_ref[blk]) + br_ref[blk])
        i = _sigmoid(_dot(xb16, wi_ref[blk]) + bi_ref[blk])
        log_a = (-LRU_C) * r * sp[:, cs]
        a = jnp.exp(log_a)
        a_ref[:, cs] = a
        b_ref[:, cs] = jnp.sqrt(1.0 - a * a) * (i * xb)

    def step(l, carry):
        if R % SUBLANES == 0:
            rs = pl.ds(pl.multiple_of(l * R, SUBLANES), R)
        else:
            rs = pl.ds(l * R, R)
        h = a_ref[rs, :] * h_ref[...] + b_ref[rs, :]
        h_ref[...] = h
        hs_ref[rs, :] = h
        return carry

    lax.fori_loop(0, tl, step, 0)
    y_ref[...] = (hs_ref[...] * _gelu_tanh(gate_ref[...])).astype(y_ref.dtype)
    hout_ref[0] = h_ref[...]


def rglru(u, c0, h0, cw, cb, wr, br, wi, bi, lam, *, G, L, R, tl):
    W = TM_WIDTH
    assert L % tl == 0
    rows = tl * R
    nt = L // tl
    hist_rows = (CONV_W - 1) * R
    off0 = -(-hist_rows // SUBLANES) * SUBLANES
    kern = functools.partial(_rglru_kernel, tl=tl, R=R)
    const2 = lambda g, n: (0, 0)
    const3 = lambda g, n: (0, 0, 0)
    y, hout = pl.pallas_call(
        kern,
        grid=(G, nt),
        in_specs=[
            pl.BlockSpec((rows, W), lambda g, n: (g * nt + n, 0)),
            pl.BlockSpec((rows, W), lambda g, n: (g * nt + n, 1)),
            pl.BlockSpec((CONV_W, W), const2),
            pl.BlockSpec((1, W), const2),
            pl.BlockSpec((1, hist_rows, W), lambda g, n: (g, 0, 0)),
            pl.BlockSpec((1, R, W), lambda g, n: (g, 0, 0)),
            pl.BlockSpec((LRU_BLOCKS, HEAD_DIM, HEAD_DIM), const3),
            pl.BlockSpec((LRU_BLOCKS, 1, HEAD_DIM), const3),
            pl.BlockSpec((LRU_BLOCKS, HEAD_DIM, HEAD_DIM), const3),
            pl.BlockSpec((LRU_BLOCKS, 1, HEAD_DIM), const3),
            pl.BlockSpec((1, W), const2),
        ],
        out_specs=[
            pl.BlockSpec((rows, W), lambda g, n: (g * nt + n, 0)),
            pl.BlockSpec((1, R, W), lambda g, n: (g, 0, 0)),
        ],
        out_shape=[
            jax.ShapeDtypeStruct((G * L * R, W), BF16),
            jax.ShapeDtypeStruct((G, R, W), F32),
        ],
        scratch_shapes=[
            pltpu.VMEM((off0 + rows, W), F32),
            pltpu.VMEM((rows, W), F32),
            pltpu.VMEM((rows, W), F32),
            pltpu.VMEM((rows, W), F32),
            pltpu.VMEM((R, W), F32),
        ],
        compiler_params=_cparams(("parallel", "arbitrary")),
        name="rglru",
    )(u, u, cw, cb.reshape(1, W), c0, h0, wr, br.reshape(LRU_BLOCKS, 1, HEAD_DIM), wi,
      bi.reshape(LRU_BLOCKS, 1, HEAD_DIM), lam.reshape(1, W))
    return y, hout


def _inv_unit_lower_many(As, C, row, col):
    eye = (row == col).astype(F32)
    base = min(SUBLANES, C)
    sh = int(math.log2(base))
    dmask = (row >> sh) == (col >> sh)
    Ps = [jnp.where(dmask, A, 0.0) for A in As]
    Ts = [eye - P for P in Ps]
    span = 2
    while span < base:
        Ps = [_dot3(P, P) for P in Ps]
        Ts = [_dot3(T, eye + P) for T, P in zip(Ts, Ps)]
        span *= 2
    s = base
    while s < C:
        sh = int(math.log2(s))
        off = ((row >> (sh + 1)) == (col >> (sh + 1))) & ((row >> sh) != (col >> sh))
        Ms = [_dot3(jnp.where(off, A, 0.0), T) for A, T in zip(As, Ts)]
        Ts = [T - _dot3(T, M) for T, M in zip(Ts, Ms)]
        s *= 2
    return Ts


def _gdn_kernel(qkv_ref, z_ref, ba_ref, cw_ref, cb_ref, c0_ref, S0_ref, alog_ref, dtb_ref, onorm_ref,
                *rest, C, n_valid, nb):
    o_ref, S_ref, hist_ref = rest[-3:]
    n = pl.program_id(1)
    H, Dh = GDN_HEADS, HEAD_DIM
    hist_rows = CONV_W - 1
    off0 = SUBLANES
    rows = nb * C

    @pl.when(n == 0)
    def _():
        for s in range(nb):
            hist_ref[s, off0 - hist_rows:off0, :] = c0_ref[s]
        S_ref[...] = S0_ref[...]

    ts = []
    for s in range(nb):
        hist_ref[s, off0:off0 + C, :] = qkv_ref[s * C:(s + 1) * C, :]
        xc = cb_ref[...]
        for j in range(CONV_W):
            r0 = off0 - hist_rows + j
            xc = xc + hist_ref[s, r0:r0 + C, :] * cw_ref[j:j + 1, :]
        tail = hist_ref[s, off0 + C - hist_rows:off0 + C, :]
        hist_ref[s, off0 - hist_rows:off0, :] = tail
        ts.append(_silu(xc))

    ba = ba_ref[...]
    beta_all = _sigmoid(ba)
    g_all = -jnp.exp(alog_ref[...]) * _softplus(ba + dtb_ref[...])
    if n_valid < C:
        rmask = (lax.broadcasted_iota(jnp.int32, (rows, LANES), 0) & (C - 1)) < n_valid
        beta_all = jnp.where(rmask, beta_all, 0.0)
        g_all = jnp.where(rmask, g_all, 0.0)
    if rows < LANES:
        g_pad = jnp.concatenate([g_all, jnp.zeros((LANES - rows, LANES), F32)], axis=0)
    else:
        g_pad = g_all
    r128 = lax.broadcasted_iota(jnp.int32, (LANES, LANES), 0)
    c128 = lax.broadcasted_iota(jnp.int32, (LANES, LANES), 1)
    shc = int(math.log2(C))
    tril = ((c128 <= r128) & ((c128 >> shc) == (r128 >> shc))).astype(F32)
    gc_pad = _dot(tril, g_pad, precision=lax.Precision.HIGHEST)
    gcT = gc_pad.T

    row = lax.broadcasted_iota(jnp.int32, (C, C), 0)
    col = lax.broadcasted_iota(jnp.int32, (C, C), 1)
    incl = col <= row
    strict = col < row

    probs = [(s, h) for s in range(nb) for h in range(H)]
    qn, kn, kn16, vh, bcol, gcol, decay, egc = {}, {}, {}, {}, {}, {}, {}, {}
    for p in probs:
        s, h = p
        t = ts[s]
        qh = t[:, h * Dh:(h + 1) * Dh]
        kh = t[:, TM_WIDTH + h * Dh:TM_WIDTH + (h + 1) * Dh]
        vh[p] = t[:, 2 * TM_WIDTH + h * Dh:2 * TM_WIDTH + (h + 1) * Dh]
        qn[p] = qh * lax.rsqrt(jnp.sum(qh * qh, -1, keepdims=True) + EPS) * (Dh ** -0.5)
        kn[p] = kh * lax.rsqrt(jnp.sum(kh * kh, -1, keepdims=True) + EPS)
        kn16[p] = kn[p].astype(BF16)
        bcol[p] = beta_all[s * C:(s + 1) * C, h:h + 1]
        gcol[p] = gc_pad[s * C:(s + 1) * C, H + h:H + h + 1]
        grow = gcT[H + h:H + h + 1, s * C:(s + 1) * C]
        diff = gcol[p] - grow
        decay[p] = jnp.where(incl, jnp.exp(jnp.where(incl, diff, 0.0)), 0.0)
        egc[p] = jnp.exp(gcol[p])
    kk = {p: _dot_nt(kn16[p], kn16[p]) for p in probs}
    qkr = {p: _dot_nt(qn[p].astype(BF16), kn16[p]) for p in probs}
    As = [jnp.where(strict, bcol[p] * kk[p] * decay[p], 0.0) for p in probs]
    Ts = _inv_unit_lower_many(As, C, row, col)
    rhs = [jnp.concatenate([bcol[p] * vh[p], (bcol[p] * egc[p]) * kn[p]], axis=1) for p in probs]
    sol = {p: _dot3(T, r) for p, T, r in zip(probs, Ts, rhs)}
    S = {p: S_ref[p[0], p[1]] for p in probs}
    S16 = {p: S[p].astype(BF16) for p in probs}
    kS = {p: _dot(sol[p][:, Dh:].astype(BF16), S16[p]) for p in probs}
    qS = {p: _dot((qn[p] * egc[p]).astype(BF16), S16[p]) for p in probs}
    u_new = {p: (sol[p][:, :Dh] - kS[p]).astype(BF16) for p in probs}
    qk16 = {p: jnp.where(incl, qkr[p] * decay[p], 0.0).astype(BF16) for p in probs}
    o = {p: qS[p] + _dot(qk16[p], u_new[p]) for p in probs}
    dS = {}
    for p in probs:
        g_last = gcol[p][C - 1:C, :]
        kdec = (kn[p] * jnp.exp(g_last - gcol[p])).astype(BF16)
        dS[p] = _dot_tn(kdec, u_new[p])
    for p in probs:
        s, h = p
        g_last = gcol[p][C - 1:C, :]
        S_ref[s, h] = S[p] * jnp.exp(g_last) + dS[p]
        on = o[p] * lax.rsqrt(jnp.mean(o[p] * o[p], -1, keepdims=True) + EPS) * onorm_ref[...]
        zs = z_ref[s * C:(s + 1) * C, h * Dh:(h + 1) * Dh]
        o_ref[s * C:(s + 1) * C, h * Dh:(h + 1) * Dh] = (on * _silu(zs)).astype(o_ref.dtype)


def gdn(u, c0, S0_all, S_out_prev, cw, cb, a_log, dt_bias, o_norm, *, layer, B, L, C, n_valid, nb, qkv_blk, z_blk,
        ba_blk):
    H, Dh = GDN_HEADS, HEAD_DIM
    nc = L // C
    assert B % nb == 0 and (nb == 1 or nc == 1) and nb * C <= LANES
    rows = nb * C
    soff = layer * (B // nb)
    alog = jnp.zeros((1, LANES), F32).at[0, H:2 * H].set(a_log)
    dtb = jnp.zeros((1, LANES), F32).at[0, H:2 * H].set(dt_bias)
    kern = functools.partial(_gdn_kernel, C=C, n_valid=n_valid, nb=nb)
    const2 = lambda b, n: (0, 0)
    in_specs = [
        pl.BlockSpec((rows, GDN_QKV), lambda b, n: (b * nc + n, qkv_blk)),
        pl.BlockSpec((rows, TM_WIDTH), lambda b, n: (b * nc + n, z_blk)),
        pl.BlockSpec((rows, LANES), lambda b, n: (b * nc + n, ba_blk)),
        pl.BlockSpec((CONV_W, GDN_QKV), const2),
        pl.BlockSpec((1, GDN_QKV), const2),
        pl.BlockSpec((nb, CONV_W - 1, GDN_QKV), lambda b, n: (b, 0, 0)),
        pl.BlockSpec((nb, H, Dh, Dh), lambda b, n: (soff + b, 0, 0, 0)),
        pl.BlockSpec((1, LANES), const2),
        pl.BlockSpec((1, LANES), const2),
        pl.BlockSpec((1, Dh), const2),
    ]
    args = [u, u, u, cw, cb.reshape(1, GDN_QKV), c0, S0_all, alog, dtb, o_norm.reshape(1, Dh)]
    aliases = {}
    if S_out_prev is not None:
        in_specs.append(pl.BlockSpec(memory_space=pl.ANY))
        args.append(S_out_prev)
        aliases = {len(args) - 1: 1}
    o, S = pl.pallas_call(
        kern,
        grid=(B // nb, nc),
        in_specs=in_specs,
        out_specs=[
            pl.BlockSpec((rows, TM_WIDTH), lambda b, n: (b * nc + n, 0)),
            pl.BlockSpec((nb, H, Dh, Dh), lambda b, n: (soff + b, 0, 0, 0)),
        ],
        out_shape=[
            jax.ShapeDtypeStruct((B * L, TM_WIDTH), BF16),
            jax.ShapeDtypeStruct(S0_all.shape, F32),
        ],
        scratch_shapes=[pltpu.VMEM((nb, SUBLANES + C, GDN_QKV), F32)],
        input_output_aliases=aliases,
        compiler_params=_cparams(("parallel", "arbitrary")),
        name="gdn",
    )(*args)
    return o, S


def _xattn_kernel(q_ref, k_ref, v_ref, o_ref, *, heads_major):
    scale = HEAD_DIM ** -0.5
    for h in range(XA_HEADS):
        cs = slice(h * HEAD_DIM, (h + 1) * HEAD_DIM)
        qh = q_ref[:, cs].astype(BF16)
        if heads_major:
            kh = k_ref[:, h, :].astype(BF16)
            vh = v_ref[:, h, :].astype(BF16)
        else:
            kh = k_ref[:, cs].astype(BF16)
            vh = v_ref[:, cs].astype(BF16)
        s = _dot_nt(qh, kh) * scale
        m = jnp.max(s, axis=-1, keepdims=True)
        p = jnp.exp(s - m)
        p = p / jnp.sum(p, axis=-1, keepdims=True)
        o_ref[:, cs] = _dot(p.astype(BF16), vh).astype(o_ref.dtype)


def xattn(q, k, v, *, B, Lq, tq, q_blk, k_blk=0, v_blk=0, layer=None):
    nq = Lq // tq
    if layer is None:
        kv_specs = [
            pl.BlockSpec((N_MEM, XA_WIDTH), lambda b, n: (b, k_blk)),
            pl.BlockSpec((N_MEM, XA_WIDTH), lambda b, n: (b, v_blk)),
        ]
    else:
        spec = pl.BlockSpec((None, None, N_MEM, XA_HEADS, HEAD_DIM), lambda b, n: (layer, b, 0, 0, 0))
        kv_specs = [spec, spec]
    return pl.pallas_call(
        functools.partial(_xattn_kernel, heads_major=layer is not None),
        grid=(B, nq),
        in_specs=[pl.BlockSpec((tq, XA_WIDTH), lambda b, n: (b * nq + n, q_blk))] + kv_specs,
        out_specs=pl.BlockSpec((tq, XA_WIDTH), lambda b, n: (b * nq + n, 0)),
        out_shape=jax.ShapeDtypeStruct((B * Lq, XA_WIDTH), BF16),
        compiler_params=_cparams(("parallel", "arbitrary")),
        name="xattn",
    )(q, k, v)


PEER_RANKS = PEER_TOPK + 1
GATE_ROWS = 32


def _cand_counts():
    return [PEER_RANKS // (a + 1) for a in range(PEER_RANKS)]


def _peer_topk_kernel(q_ref, sk_ref, pw_ref, aux_ref, sv_ref, cand_ref):
    K = PEER_TOPK
    q = q_ref[...]
    s = []
    for p in range(2):
        qp = q[:, p * LANES:(p + 1) * LANES]
        s.append(_dot_nt(sk_ref[0, p], qp, precision=lax.Precision.HIGHEST))
    for p in range(2):
        cur = s[p]
        for r in range(PEER_RANKS):
            m = jnp.max(cur, axis=0, keepdims=True)
            sv_ref[p, r:r + 1, :] = m
            cur = jnp.where(cur == m, NEG_INF, cur)
    sv1 = sv_ref[0, 0:PEER_RANKS, :]
    sv2 = sv_ref[1, 0:PEER_RANKS, :]
    cand_ref[...] = jnp.full(cand_ref.shape, NEG_INF, F32)
    off = 0
    for a, nb in enumerate(_cand_counts()):
        cand_ref[off:off + nb, :] = sv1[a:a + 1, :] + sv2[0:nb, :]
        off += nb
    cur = cand_ref[...]
    tops = []
    for r in range(PEER_RANKS):
        m = jnp.max(cur, axis=0, keepdims=True)
        tops.append(m)
        cur = jnp.where(cur == m, NEG_INF, cur)
    z = jnp.ones_like(tops[0])
    for r in range(1, K):
        z = z + jnp.exp(tops[r] - tops[0])
    thr = 0.5 * (tops[K - 1] + tops[K])
    pw_ref[0, 0] = thr - s[0]
    pw_ref[0, 1] = s[1]
    pw_ref[0, 2] = jnp.exp(s[1] - sv2[0:1, :])
    aux_ref[0] = jnp.broadcast_to(thr - sv1[0:1, :] - jnp.log(z), aux_ref.shape[1:])


def peer_topk(q, subkeys, *, tt=512):
    T = q.shape[0]
    tt = min(tt, T)
    assert T % tt == 0
    H, NK = PEER_HEADS, PEER_NKEYS
    pad8 = lambda n: -(-n // SUBLANES) * SUBLANES
    return pl.pallas_call(
        _peer_topk_kernel,
        grid=(T // tt, H),
        in_specs=[
            pl.BlockSpec((tt, 2 * LANES), lambda i, h: (i, h)),
            pl.BlockSpec((1, 2, NK, LANES), lambda i, h: (h, 0, 0, 0)),
        ],
        out_specs=[
            pl.BlockSpec((1, 3, NK, tt), lambda i, h: (h, 0, 0, i)),
            pl.BlockSpec((1, SUBLANES, tt), lambda i, h: (h, 0, i)),
        ],
        out_shape=[
            jax.ShapeDtypeStruct((H, 3, NK, T), F32),
            jax.ShapeDtypeStruct((H, SUBLANES, T), F32),
        ],
        scratch_shapes=[pltpu.VMEM((2, pad8(PEER_RANKS), tt), F32), pltpu.VMEM((pad8(sum(_cand_counts())), tt), F32)],
        compiler_params=_cparams(("parallel", "arbitrary")),
        name="peer_topk",
    )(q, subkeys)


def _peer_dense_kernel(x_ref, g_ref, pw_ref, aux_ref, u_ref, v_ref, o_ref, xnT_ref, acc_ref, w_ref, *, ib):
    e = pl.program_id(1)
    H, NK = PEER_HEADS, PEER_NKEYS
    tt = x_ref.shape[0]
    strip = min(LANES, tt)
    assert tt % strip == 0

    def gates(blk, ii, dst):
        i = blk * ib + ii
        thr1 = [pw_ref[h, 0, pl.ds(i, 1), :] for h in range(H)]
        e1 = [0.5 * jnp.exp(aux_ref[h, 0:1, :] - thr1[h]) for h in range(H)]
        for tc in range(tt // strip):
            ts_ = slice(tc * strip, (tc + 1) * strip)
            for j0 in range(0, NK, GATE_ROWS):
                js = slice(j0, j0 + GATE_ROWS)
                w = None
                for h in range(H):
                    sel = pw_ref[h, 1, js, ts_] >= thr1[h][:, ts_]
                    c = jnp.where(sel, pw_ref[h, 2, js, ts_], 0.0) * e1[h][:, ts_]
                    w = c if w is None else w + c
                dst[ii * NK + j0:ii * NK + j0 + GATE_ROWS, ts_] = w

    @pl.when(e == 0)
    def _():
        xn = _rms(x_ref[...], g_ref[...])
        xnT_ref[...] = xn.T.astype(BF16)
        acc_ref[...] = jnp.zeros_like(acc_ref)

    hT = _dot(u_ref[...], xnT_ref[...])
    for ii in range(ib):
        gates(e, ii, w_ref)
    aT = w_ref[...] * _gelu_tanh_x2(hT)
    acc_ref[...] += _dot(aT.T.astype(BF16), v_ref[...])

    @pl.when(e == pl.num_programs(1) - 1)
    def _():
        o_ref[...] = x_ref[...] + acc_ref[...]


def peer_dense(x, g, pw, aux, u_tab, v_tab, *, layer=0, tt=512, ib=4):
    T, D = x.shape
    boff = layer * (PEER_NKEYS // ib)
    tt = min(tt, T)
    assert T % tt == 0 and PEER_NKEYS % ib == 0
    H, NK = PEER_HEADS, PEER_NKEYS
    eb = ib * NK
    kern = functools.partial(_peer_dense_kernel, ib=ib)
    return pl.pallas_call(
        kern,
        grid=(T // tt, NK // ib),
        in_specs=[
            pl.BlockSpec((tt, D), lambda i, e: (i, 0)),
            pl.BlockSpec((1, D), lambda i, e: (0, 0)),
            pl.BlockSpec((H, 3, NK, tt), lambda i, e: (0, 0, 0, i)),
            pl.BlockSpec((H, SUBLANES, tt), lambda i, e: (0, 0, i)),
            pl.BlockSpec((eb, D), lambda i, e: (boff + e, 0)),
            pl.BlockSpec((eb, D), lambda i, e: (boff + e, 0)),
        ],
        out_specs=pl.BlockSpec((tt, D), lambda i, e: (i, 0)),
        out_shape=jax.ShapeDtypeStruct((T, D), F32),
        scratch_shapes=[
            pltpu.VMEM((D, tt), BF16),
            pltpu.VMEM((tt, D), F32),
            pltpu.VMEM((eb, tt), F32),
        ],
        compiler_params=_cparams(("parallel", "arbitrary")),
        name="peer_dense",
    )(x, g.reshape(1, D), pw, aux, u_tab, v_tab)


def _prep_weights(w_in_a, w_in_b, w_out, w_mem_kv, peer_wq, peer_u, peer_v, lru_wr, lru_wi):
    H = GDN_HEADS
    qkvz = GDN_QKV + TM_WIDTH
    wb = jnp.concatenate(
        [
            w_in_b[:, :, :qkvz],
            w_in_b[:, :, qkvz + 2 * H:],
            w_in_b[:, :, qkvz:qkvz + 2 * H],
            jnp.zeros(w_in_b.shape[:2] + (B_IN_PAD - w_in_b.shape[2],), w_in_b.dtype),
        ],
        axis=-1,
    )
    return dict(
        w_in_a=w_in_a.astype(BF16),
        w_in_b=wb.astype(BF16),
        w_out_tm=w_out[:, :TM_WIDTH].astype(BF16),
        w_out_xa=w_out[:, TM_WIDTH:].astype(BF16),
        w_mem_kv=w_mem_kv.astype(BF16),
        peer_wq=peer_wq.astype(BF16),
        peer_u=peer_u.astype(BF16).reshape(-1, peer_u.shape[-1]),
        peer_v=peer_v.astype(BF16).reshape(-1, peer_v.shape[-1]),
        lru_wr=lru_wr.astype(BF16),
        lru_wi=lru_wi.astype(BF16),
    )


def _trunk(x, mem, lru_h0, lru_c0, gdn_S0, gdn_c0, w, wc, *, B, L, time_major):
    if time_major:
        G, R, tl = 1, B, L
    else:
        G, R, tl = B, 1, min(L, 256)
    Lp = -(-L // SUBLANES) * SUBLANES
    hist = CONV_W - 1

    def to_batch_major(a):
        c = a.shape[-1]
        a = a.reshape(L, B, c).transpose(1, 0, 2)
        return jnp.pad(a, ((0, 0), (0, Lp - L), (0, 0))).reshape(B * Lp, c)

    def to_time_major(a):
        c = a.shape[-1]
        return a.reshape(B, Lp, c)[:, :L].transpose(1, 0, 2).reshape(L * B, c)

    def last_rows(u, c0, c1):
        if time_major:
            return u.reshape(L, B, -1)[L - hist:, :, c0:c1].transpose(1, 0, 2)
        return u.reshape(B, L, -1)[:, L - hist:, c0:c1]

    n_b = gdn_S0.shape[0]
    S0_all = gdn_S0.reshape((n_b * B,) + gdn_S0.shape[2:])
    S_all = None
    lru_h, lru_c, gdn_c = [], [], []
    for l in range(DEPTH):
        j = l // 2
        if l % 2 == 0:
            u, qmem = norm_matmul(x, w['norm_mix'][l], wc['w_in_a'][j], side_cols=(2 * TM_WIDTH, XA_WIDTH))
            c0 = lru_c0[j]
            if time_major:
                c0k = c0.transpose(1, 0, 2).reshape(1, hist * B, TM_WIDTH)
                h0k = lru_h0[j].reshape(1, B, TM_WIDTH)
            else:
                c0k = c0
                h0k = lru_h0[j].reshape(B, 1, TM_WIDTH)
            tm, h = rglru(u, c0k, h0k, w['lru_conv_w'][j], w['lru_conv_b'][j], wc['lru_wr'][j], w['lru_br'][j],
                          wc['lru_wi'][j], w['lru_bi'][j], w['lru_lambda'][j], G=G, L=L, R=R, tl=tl)
            lru_h.append(h.reshape(B, TM_WIDTH))
            lru_c.append(last_rows(u, TM_WIDTH, 2 * TM_WIDTH))
        else:
            u, qmem = norm_matmul(x, w['norm_mix'][l], wc['w_in_b'][j], side_cols=(GDN_QKV + TM_WIDTH, XA_WIDTH))
            C = min(GDN_CHUNK, Lp)
            nb = GDN_SEQS_PER_STEP if (Lp == C and B % GDN_SEQS_PER_STEP == 0) else 1
            u_bm = to_batch_major(u) if time_major else u
            o_bm, S_all = gdn(u_bm, gdn_c0[j], S0_all, S_all, w['gdn_conv_w'][j], w['gdn_conv_b'][j],
                              w['gdn_a_log'][j], w['gdn_dt_bias'][j], w['gdn_o_norm'][j], layer=j, B=B, L=Lp, C=C,
                              n_valid=min(L, C), nb=nb, qkv_blk=0, z_blk=GDN_QKV // TM_WIDTH,
                              ba_blk=(GDN_QKV + TM_WIDTH + XA_WIDTH) // LANES)
            tm = to_time_major(o_bm) if time_major else o_bm
            gdn_c.append(last_rows(u, 0, GDN_QKV))
        if time_major:
            q_bm = to_batch_major(qmem)
            xa = to_time_major(xattn(q_bm, mem[0], mem[1], B=B, Lq=Lp, tq=Lp, q_blk=0, layer=l))
        else:
            xa = xattn(qmem, mem[l], mem[l], B=B, Lq=L, tq=min(L, 512), q_blk=0, k_blk=0, v_blk=1)
        x = out_proj(tm, xa, wc['w_out_tm'][l], wc['w_out_xa'][l], x)
        q = norm_matmul(x, w['norm_ffn'][l], wc['peer_wq'][l])
        pw, aux = peer_topk(q, w['peer_subkeys'][l])
        x = peer_dense(x, w['norm_ffn'][l], pw, aux, wc['peer_u'], wc['peer_v'], layer=l)
    y = rmsnorm_rows(x, w['norm_final'])
    return y, jnp.stack(lru_h), jnp.stack(lru_c), S_all.reshape(gdn_S0.shape), jnp.stack(gdn_c)


def kernel(x_prompt, x_sample, state_rglru_h, state_rglru_conv, state_gdn_S, state_gdn_conv, cache_mem_k, cache_mem_v, mem_prompt, norm_mix, norm_ffn, norm_final, w_in_a, w_in_b, w_out, lru_conv_w, lru_conv_b, lru_wr, lru_br, lru_wi, lru_bi, lru_lambda, gdn_conv_w, gdn_conv_b, gdn_a_log, gdn_dt_bias, gdn_o_norm, norm_mem, w_mem_kv, peer_wq, peer_subkeys, peer_u, peer_v):
    w = dict(norm_mix=norm_mix, norm_ffn=norm_ffn, norm_final=norm_final, lru_conv_w=lru_conv_w,
             lru_conv_b=lru_conv_b, lru_br=lru_br, lru_bi=lru_bi, lru_lambda=lru_lambda, gdn_conv_w=gdn_conv_w,
             gdn_conv_b=gdn_conv_b, gdn_a_log=gdn_a_log, gdn_dt_bias=gdn_dt_bias, gdn_o_norm=gdn_o_norm,
             peer_subkeys=peer_subkeys)
    wc = _prep_weights(w_in_a, w_in_b, w_out, w_mem_kv, peer_wq, peer_u, peer_v, lru_wr, lru_wi)
    Bp, Lp_, D = x_prompt.shape
    Bs, Ls, _ = x_sample.shape
    n_a, n_b = state_rglru_h.shape[0], state_gdn_S.shape[0]

    mem2 = mem_prompt.reshape(Bp * N_MEM, D)
    kv = [norm_matmul(mem2, norm_mem[l], wc['w_mem_kv'][l]) for l in range(DEPTH)]
    y_p, p_h, p_ca, p_S, p_cb = _trunk(
        x_prompt.reshape(Bp * Lp_, D), kv,
        jnp.zeros((n_a, Bp, TM_WIDTH), F32), jnp.zeros((n_a, Bp, CONV_W - 1, TM_WIDTH), F32),
        jnp.zeros((n_b, Bp, GDN_HEADS, HEAD_DIM, HEAD_DIM), F32), jnp.zeros((n_b, Bp, CONV_W - 1, GDN_QKV), F32),
        w, wc, B=Bp, L=Lp_, time_major=False)
    p_mem_k = jnp.stack([a[:, :XA_WIDTH].reshape(Bp, N_MEM, XA_HEADS, HEAD_DIM) for a in kv])
    p_mem_v = jnp.stack([a[:, XA_WIDTH:].reshape(Bp, N_MEM, XA_HEADS, HEAD_DIM) for a in kv])

    xs = x_sample.transpose(1, 0, 2).reshape(Ls * Bs, D)
    y_s, s_h, s_ca, s_S, s_cb = _trunk(xs, (cache_mem_k, cache_mem_v), state_rglru_h, state_rglru_conv,
                                       state_gdn_S, state_gdn_conv, w, wc, B=Bs, L=Ls, time_major=True)
    y_s = y_s.reshape(Ls, Bs, D).transpose(1, 0, 2)
    return (y_p.reshape(Bp, Lp_, D), y_s, p_h, p_ca, p_S, p_cb, p_mem_k, p_mem_v, s_h, s_ca, s_S, s_cb)
```

```python
import functools
import math

import jax
import jax.numpy as jnp
from jax import lax
from jax.experimental import pallas as pl
from jax.experimental.pallas import tpu as pltpu

F32 = jnp.float32
BF16 = jnp.bfloat16

D_MODEL = 2048
DEPTH = 4
HEAD_DIM = 128
XA_HEADS = 4
XA_WIDTH = XA_HEADS * HEAD_DIM
TM_WIDTH = D_MODEL - XA_WIDTH
N_MEM = 256
CONV_W = 4
EPS = 1e-6
LRU_BLOCKS = TM_WIDTH // HEAD_DIM
LRU_C = 8.0
GDN_HEADS = TM_WIDTH // HEAD_DIM
GDN_QKV = 3 * TM_WIDTH
GDN_CHUNK = 64
GDN_SEQS_PER_STEP = 4
XATTN_SEQS_PER_STEP = 4
PEER_HEADS = 8
PEER_NKEYS = 128
PEER_TOPK = 16
B_IN_PAD = 7168

SUBLANES = 8
LANES = 128
VMEM_LIMIT = 52 * 1024 * 1024

NEG_INF = float("-inf")


def _cparams(sem):
    return pltpu.CompilerParams(dimension_semantics=sem, vmem_limit_bytes=VMEM_LIMIT)


def _dot(a, b, precision=None):
    return jnp.dot(a, b, preferred_element_type=F32, precision=precision)


def _mm(a, b):
    return jnp.dot(a.astype(BF16), b.astype(BF16), preferred_element_type=F32)


def _split_bf16(a):
    hi = a.astype(BF16)
    return hi, (a - hi.astype(F32)).astype(BF16)


def _dot3(a, b):
    ah, al = _split_bf16(a)
    bh, bl = _split_bf16(b)
    return _dot(ah, bh) + (_dot(ah, bl) + _dot(al, bh))


def _dot_nt(a, b, precision=None):
    return lax.dot_general(a, b, (((1,), (1,)), ((), ())), preferred_element_type=F32, precision=precision)


def _dot_tn(a, b, precision=None):
    return lax.dot_general(a, b, (((0,), (0,)), ((), ())), preferred_element_type=F32, precision=precision)


def _sigmoid(x):
    return 1.0 / (1.0 + jnp.exp(-x))


def _silu(x):
    return x * _sigmoid(x)


def _gelu_tanh(x):
    c = math.sqrt(2.0 / math.pi)
    return 0.5 * x * (1.0 + jnp.tanh(c * (x + 0.044715 * (x * x * x))))


def _gelu_tanh_x2(x):
    c = math.sqrt(2.0 / math.pi)
    return x * (1.0 + jnp.tanh(x * (c + (0.044715 * c) * (x * x))))


def _softplus(x):
    return jnp.maximum(x, 0.0) + jnp.log(1.0 + jnp.exp(-jnp.abs(x)))


def _rms(x, g):
    ms = jnp.mean(x * x, axis=-1, keepdims=True)
    return x * lax.rsqrt(ms + EPS) * g


def _norm_matmul_kernel(x_ref, g_ref, w_ref, o_ref, *rest, side):
    xn_ref = rest[-1]

    @pl.when(pl.program_id(1) == 0)
    def _():
        xn_ref[...] = _rms(x_ref[...], g_ref[...]).astype(BF16)

    acc = _dot(xn_ref[...], w_ref[...])
    o_ref[...] = acc
    if side is not None:
        tile, off, width = side

        @pl.when(pl.program_id(1) == tile)
        def _():
            rest[0][...] = acc[:, off:off + width]


def _matmul_tiles(m, n, tm, tn):
    tm = min(tm, m)
    while m % tm:
        tm //= 2
    while n % tn:
        tn //= 2
    return tm, tn


def norm_matmul(x, g, w, *, side_cols=None, tm=1024, tn=1024):
    m, k = x.shape
    n = w.shape[1]
    tm, tn = _matmul_tiles(m, n, tm, tn)
    assert tm % SUBLANES == 0 and tn % LANES == 0
    out_specs = [pl.BlockSpec((tm, tn), lambda i, j: (i, j))]
    out_shape = [jax.ShapeDtypeStruct((m, n), F32)]
    side = None
    if side_cols is not None:
        start, width = side_cols
        side = (start // tn, start % tn, width)
        assert start % tn + width <= tn
        out_specs.append(pl.BlockSpec((tm, width), lambda i, j: (i, 0)))
        out_shape.append(jax.ShapeDtypeStruct((m, width), F32))
    outs = pl.pallas_call(
        functools.partial(_norm_matmul_kernel, side=side),
        grid=(m // tm, n // tn),
        in_specs=[
            pl.BlockSpec((tm, k), lambda i, j: (i, 0)),
            pl.BlockSpec((1, k), lambda i, j: (0, 0)),
            pl.BlockSpec((k, tn), lambda i, j: (0, j)),
        ],
        out_specs=out_specs,
        out_shape=out_shape,
        scratch_shapes=[pltpu.VMEM((tm, k), BF16)],
        compiler_params=_cparams(("parallel", "arbitrary")),
        name="norm_matmul",
    )(x, g.reshape(1, k), w)
    return outs[0] if side_cols is None else tuple(outs)


def _out_proj_kernel(a_ref, b_ref, wa_ref, wb_ref, r_ref, o_ref):
    acc = _dot(a_ref[...], wa_ref[...]) + _dot(b_ref[...], wb_ref[...])
    o_ref[...] = r_ref[...] + acc


def out_proj(a, b, wa, wb, r, *, tm=1024, tn=1024):
    m, ka = a.shape
    kb = b.shape[1]
    n = wa.shape[1]
    tm, tn = _matmul_tiles(m, n, tm, tn)
    assert tm % SUBLANES == 0 and tn % LANES == 0
    return pl.pallas_call(
        _out_proj_kernel,
        grid=(m // tm, n // tn),
        in_specs=[
            pl.BlockSpec((tm, ka), lambda i, j: (i, 0)),
            pl.BlockSpec((tm, kb), lambda i, j: (i, 0)),
            pl.BlockSpec((ka, tn), lambda i, j: (0, j)),
            pl.BlockSpec((kb, tn), lambda i, j: (0, j)),
            pl.BlockSpec((tm, tn), lambda i, j: (i, j)),
        ],
        out_specs=pl.BlockSpec((tm, tn), lambda i, j: (i, j)),
        out_shape=jax.ShapeDtypeStruct((m, n), F32),
        compiler_params=_cparams(("parallel", "arbitrary")),
        name="out_proj",
    )(a, b, wa, wb, r)


def _rmsnorm_kernel(x_ref, g_ref, o_ref):
    o_ref[...] = _rms(x_ref[...], g_ref[...])


def rmsnorm_rows(x, g, *, tm=512):
    m, k = x.shape
    tm = min(tm, m)
    assert m % tm == 0
    return pl.pallas_call(
        _rmsnorm_kernel,
        grid=(m // tm,),
        in_specs=[pl.BlockSpec((tm, k), lambda i: (i, 0)), pl.BlockSpec((1, k), lambda i: (0, 0))],
        out_specs=pl.BlockSpec((tm, k), lambda i: (i, 0)),
        out_shape=jax.ShapeDtypeStruct((m, k), F32),
        compiler_params=_cparams(("parallel",)),
        name="final_rmsnorm",
    )(x, g.reshape(1, k))


def _rglru_kernel(gate_ref, xr_ref, cw_ref, cb_ref, c0_ref, h0_ref, wr_ref, br_ref, wi_ref, bi_ref, lam_ref,
                  y_ref, hout_ref, hist_ref, a_ref, b_ref, hs_ref, h_ref, *, tl, R):
    n = pl.program_id(1)
    rows = tl * R
    hist_rows = (CONV_W - 1) * R
    off0 = -(-hist_rows // SUBLANES) * SUBLANES

    @pl.when(n == 0)
    def _():
        hist_ref[off0 - hist_rows:off0, :] = c0_ref[0]
        h_ref[...] = h0_ref[0]

    x = xr_ref[...]
    hist_ref[off0:off0 + rows, :] = x
    xc = cb_ref[...]
    for j in range(CONV_W):
        s = off0 - hist_rows + j * R
        xc = xc + hist_ref[s:s + rows, :] * cw_ref[j:j + 1, :]
    tail = hist_ref[off0 + rows - hist_rows:off0 + rows, :]
    hist_ref[off0 - hist_rows:off0, :] = tail

    sp = _softplus(-lam_ref[...])
    for blk in range(LRU_BLOCKS):
        cs = slice(blk * HEAD_DIM, (blk + 1) * HEAD_DIM)
        xb = xc[:, cs]
        xb16 = xb.astype(BF16)
        r = _sigmoid(_dot(xb16, wr_ref[blk]) + br_ref[blk])
        i = _sigmoid(_dot(xb16, wi_ref[blk]) + bi_ref[blk])
        log_a = (-LRU_C) * r * sp[:, cs]
        a = jnp.exp(log_a)
        a_ref[:, cs] = a
        b_ref[:, cs] = jnp.sqrt(1.0 - a * a) * (i * xb)

    def step(l, carry):
        if R % SUBLANES == 0:
            rs = pl.ds(pl.multiple_of(l * R, SUBLANES), R)
        else:
            rs = pl.ds(l * R, R)
        h = a_ref[rs, :] * h_ref[...] + b_ref[rs, :]
        h_ref[...] = h
        hs_ref[rs, :] = h
        return carry

    lax.fori_loop(0, tl, step, 0)
    y_ref[...] = (hs_ref[...] * _gelu_tanh(gate_ref[...])).astype(y_ref.dtype)
    hout_ref[0] = h_ref[...]


def rglru(u, c0, h0, cw, cb, wr, br, wi, bi, lam, *, G, L, R, tl):
    W = TM_WIDTH
    assert L % tl == 0
    rows = tl * R
    nt = L // tl
    hist_rows = (CONV_W - 1) * R
    off0 = -(-hist_rows // SUBLANES) * SUBLANES
    kern = functools.partial(_rglru_kernel, tl=tl, R=R)
    const2 = lambda g, n: (0, 0)
    const3 = lambda g, n: (0, 0, 0)
    y, hout = pl.pallas_call(
        kern,
        grid=(G, nt),
        in_specs=[
            pl.BlockSpec((rows, W), lambda g, n: (g * nt + n, 0)),
            pl.BlockSpec((rows, W), lambda g, n: (g * nt + n, 1)),
            pl.BlockSpec((CONV_W, W), const2),
            pl.BlockSpec((1, W), const2),
            pl.BlockSpec((1, hist_rows, W), lambda g, n: (g, 0, 0)),
            pl.BlockSpec((1, R, W), lambda g, n: (g, 0, 0)),
            pl.BlockSpec((LRU_BLOCKS, HEAD_DIM, HEAD_DIM), const3),
            pl.BlockSpec((LRU_BLOCKS, 1, HEAD_DIM), const3),
            pl.BlockSpec((LRU_BLOCKS, HEAD_DIM, HEAD_DIM), const3),
            pl.BlockSpec((LRU_BLOCKS, 1, HEAD_DIM), const3),
            pl.BlockSpec((1, W), const2),
        ],
        out_specs=[
            pl.BlockSpec((rows, W), lambda g, n: (g * nt + n, 0)),
            pl.BlockSpec((1, R, W), lambda g, n: (g, 0, 0)),
        ],
        out_shape=[
            jax.ShapeDtypeStruct((G * L * R, W), BF16),
            jax.ShapeDtypeStruct((G, R, W), F32),
        ],
        scratch_shapes=[
            pltpu.VMEM((off0 + rows, W), F32),
            pltpu.VMEM((rows, W), F32),
            pltpu.VMEM((rows, W), F32),
            pltpu.VMEM((rows, W), F32),
            pltpu.VMEM((R, W), F32),
        ],
        compiler_params=_cparams(("parallel", "arbitrary")),
        name="rglru",
    )(u, u, cw, cb.reshape(1, W), c0, h0, wr, br.reshape(LRU_BLOCKS, 1, HEAD_DIM), wi,
      bi.reshape(LRU_BLOCKS, 1, HEAD_DIM), lam.reshape(1, W))
    return y, hout


def _inv_unit_lower_many(As, C, row, col):
    eye = (row == col).astype(F32)
    base = min(SUBLANES, C)
    sh = int(math.log2(base))
    dmask = (row >> sh) == (col >> sh)
    Ps = [jnp.where(dmask, A, 0.0) for A in As]
    Ts = [eye - P for P in Ps]
    span = 2
    while span < base:
        Ps = [_dot3(P, P) for P in Ps]
        Ts = [_dot3(T, eye + P) for T, P in zip(Ts, Ps)]
        span *= 2
    s = base
    while s < C:
        sh = int(math.log2(s))
        off = ((row >> (sh + 1)) == (col >> (sh + 1))) & ((row >> sh) != (col >> sh))
        Ms = [_dot3(jnp.where(off, A, 0.0), T) for A, T in zip(As, Ts)]
        Ts = [T - _dot3(T, M) for T, M in zip(Ts, Ms)]
        s *= 2
    return Ts


def _gdn_kernel(qkv_ref, z_ref, ba_ref, cw_ref, cb_ref, c0_ref, S0_ref, alog_ref, dtb_ref, onorm_ref,
                *rest, C, n_valid, nb):
    o_ref, S_ref, hist_ref = rest[-3:]
    n = pl.program_id(1)
    H, Dh = GDN_HEADS, HEAD_DIM
    hist_rows = CONV_W - 1
    off0 = SUBLANES
    rows = nb * C

    @pl.when(n == 0)
    def _():
        for s in range(nb):
            hist_ref[s, off0 - hist_rows:off0, :] = c0_ref[s]
        S_ref[...] = S0_ref[...]

    ts = []
    for s in range(nb):
        hist_ref[s, off0:off0 + C, :] = qkv_ref[s * C:(s + 1) * C, :]
        xc = cb_ref[...]
        for j in range(CONV_W):
            r0 = off0 - hist_rows + j
            xc = xc + hist_ref[s, r0:r0 + C, :] * cw_ref[j:j + 1, :]
        tail = hist_ref[s, off0 + C - hist_rows:off0 + C, :]
        hist_ref[s, off0 - hist_rows:off0, :] = tail
        ts.append(_silu(xc))

    ba = ba_ref[...]
    beta_all = _sigmoid(ba)
    g_all = -jnp.exp(alog_ref[...]) * _softplus(ba + dtb_ref[...])
    if n_valid < C:
        rmask = (lax.broadcasted_iota(jnp.int32, (rows, LANES), 0) & (C - 1)) < n_valid
        beta_all = jnp.where(rmask, beta_all, 0.0)
        g_all = jnp.where(rmask, g_all, 0.0)
    if rows < LANES:
        g_pad = jnp.concatenate([g_all, jnp.zeros((LANES - rows, LANES), F32)], axis=0)
    else:
        g_pad = g_all
    r128 = lax.broadcasted_iota(jnp.int32, (LANES, LANES), 0)
    c128 = lax.broadcasted_iota(jnp.int32, (LANES, LANES), 1)
    shc = int(math.log2(C))
    tril = ((c128 <= r128) & ((c128 >> shc) == (r128 >> shc))).astype(F32)
    gc_pad = _dot(tril, g_pad, precision=lax.Precision.HIGHEST)
    gcT = gc_pad.T

    row = lax.broadcasted_iota(jnp.int32, (C, C), 0)
    col = lax.broadcasted_iota(jnp.int32, (C, C), 1)
    incl = col <= row
    strict = col < row

    probs = [(s, h) for s in range(nb) for h in range(H)]
    qn, kn, kn16, vh, bcol, gcol, decay, egc = {}, {}, {}, {}, {}, {}, {}, {}
    for p in probs:
        s, h = p
        t = ts[s]
        qh = t[:, h * Dh:(h + 1) * Dh]
        kh = t[:, TM_WIDTH + h * Dh:TM_WIDTH + (h + 1) * Dh]
        vh[p] = t[:, 2 * TM_WIDTH + h * Dh:2 * TM_WIDTH + (h + 1) * Dh]
        qn[p] = qh * lax.rsqrt(jnp.sum(qh * qh, -1, keepdims=True) + EPS) * (Dh ** -0.5)
        kn[p] = kh * lax.rsqrt(jnp.sum(kh * kh, -1, keepdims=True) + EPS)
        kn16[p] = kn[p].astype(BF16)
        bcol[p] = beta_all[s * C:(s + 1) * C, h:h + 1]
        gcol[p] = gc_pad[s * C:(s + 1) * C, H + h:H + h + 1]
        grow = gcT[H + h:H + h + 1, s * C:(s + 1) * C]
        diff = gcol[p] - grow
        decay[p] = jnp.where(incl, jnp.exp(jnp.where(incl, diff, 0.0)), 0.0)
        egc[p] = jnp.exp(gcol[p])
    kk = {p: _dot_nt(kn16[p], kn16[p]) for p in probs}
    qkr = {p: _dot_nt(qn[p].astype(BF16), kn16[p]) for p in probs}
    As = [jnp.where(strict, bcol[p] * kk[p] * decay[p], 0.0) for p in probs]
    Ts = _inv_unit_lower_many(As, C, row, col)
    rhs = [jnp.concatenate([bcol[p] * vh[p], (bcol[p] * egc[p]) * kn[p]], axis=1) for p in probs]
    sol = {p: _dot3(T, r) for p, T, r in zip(probs, Ts, rhs)}
    S = {p: S_ref[p[0], p[1]] for p in probs}
    S16 = {p: S[p].astype(BF16) for p in probs}
    kS = {p: _dot(sol[p][:, Dh:].astype(BF16), S16[p]) for p in probs}
    qS = {p: _dot((qn[p] * egc[p]).astype(BF16), S16[p]) for p in probs}
    u_new = {p: (sol[p][:, :Dh] - kS[p]).astype(BF16) for p in probs}
    qk16 = {p: jnp.where(incl, qkr[p] * decay[p], 0.0).astype(BF16) for p in probs}
    o = {p: qS[p] + _dot(qk16[p], u_new[p]) for p in probs}
    dS = {}
    for p in probs:
        g_last = gcol[p][C - 1:C, :]
        kdec = (kn[p] * jnp.exp(g_last - gcol[p])).astype(BF16)
        dS[p] = _dot_tn(kdec, u_new[p])
    for p in probs:
        s, h = p
        g_last = gcol[p][C - 1:C, :]
        S_ref[s, h] = S[p] * jnp.exp(g_last) + dS[p]
        on = o[p] * lax.rsqrt(jnp.mean(o[p] * o[p], -1, keepdims=True) + EPS) * onorm_ref[...]
        zs = z_ref[s * C:(s + 1) * C, h * Dh:(h + 1) * Dh]
        o_ref[s * C:(s + 1) * C, h * Dh:(h + 1) * Dh] = (on * _silu(zs)).astype(o_ref.dtype)


def gdn(u, c0, S0_all, S_out_prev, cw, cb, a_log, dt_bias, o_norm, *, layer, B, L, C, n_valid, nb, qkv_blk, z_blk,
        ba_blk):
    H, Dh = GDN_HEADS, HEAD_DIM
    nc = L // C
    assert B % nb == 0 and (nb == 1 or nc == 1) and nb * C <= LANES
    rows = nb * C
    soff = layer * (B // nb)
    alog = jnp.zeros((1, LANES), F32).at[0, H:2 * H].set(a_log)
    dtb = jnp.zeros((1, LANES), F32).at[0, H:2 * H].set(dt_bias)
    kern = functools.partial(_gdn_kernel, C=C, n_valid=n_valid, nb=nb)
    const2 = lambda b, n: (0, 0)
    in_specs = [
        pl.BlockSpec((rows, GDN_QKV), lambda b, n: (b * nc + n, qkv_blk)),
        pl.BlockSpec((rows, TM_WIDTH), lambda b, n: (b * nc + n, z_blk)),
        pl.BlockSpec((rows, LANES), lambda b, n: (b * nc + n, ba_blk)),
        pl.BlockSpec((CONV_W, GDN_QKV), const2),
        pl.BlockSpec((1, GDN_QKV), const2),
        pl.BlockSpec((nb, CONV_W - 1, GDN_QKV), lambda b, n: (b, 0, 0)),
        pl.BlockSpec((nb, H, Dh, Dh), lambda b, n: (soff + b, 0, 0, 0)),
        pl.BlockSpec((1, LANES), const2),
        pl.BlockSpec((1, LANES), const2),
        pl.BlockSpec((1, Dh), const2),
    ]
    args = [u, u, u, cw, cb.reshape(1, GDN_QKV), c0, S0_all, alog, dtb, o_norm.reshape(1, Dh)]
    aliases = {}
    if S_out_prev is not None:
        in_specs.append(pl.BlockSpec(memory_space=pl.ANY))
        args.append(S_out_prev)
        aliases = {len(args) - 1: 1}
    o, S = pl.pallas_call(
        kern,
        grid=(B // nb, nc),
        in_specs=in_specs,
        out_specs=[
            pl.BlockSpec((rows, TM_WIDTH), lambda b, n: (b * nc + n, 0)),
            pl.BlockSpec((nb, H, Dh, Dh), lambda b, n: (soff + b, 0, 0, 0)),
        ],
        out_shape=[
            jax.ShapeDtypeStruct((B * L, TM_WIDTH), BF16),
            jax.ShapeDtypeStruct(S0_all.shape, F32),
        ],
        scratch_shapes=[pltpu.VMEM((nb, SUBLANES + C, GDN_QKV), F32)],
        input_output_aliases=aliases,
        compiler_params=_cparams(("parallel", "arbitrary")),
        name="gdn",
    )(*args)
    return o, S


def _softmax_rows(s):
    e = jnp.exp(s - jnp.max(s, axis=-1, keepdims=True))
    return e / jnp.sum(e, axis=-1, keepdims=True)


def _xattn_kernel(q_ref, k_ref, v_ref, o_ref):
    scale = HEAD_DIM ** -0.5
    for h in range(XA_HEADS):
        cs = slice(h * HEAD_DIM, (h + 1) * HEAD_DIM)
        kh = k_ref[:, cs].astype(BF16)
        vh = v_ref[:, cs].astype(BF16)
        p = _softmax_rows(_dot_nt(q_ref[:, cs].astype(BF16), kh) * scale)
        o_ref[:, cs] = _dot(p.astype(BF16), vh).astype(o_ref.dtype)


def _xattn_cache_kernel(q_ref, k_ref, v_ref, o_ref, *, nb):
    scale = HEAD_DIM ** -0.5
    tq = q_ref.shape[0] // nb
    rows = XA_HEADS * tq
    cols = N_MEM * XA_HEADS
    own = ((lax.broadcasted_iota(jnp.int32, (rows, cols), 1) % XA_HEADS)
           == (lax.broadcasted_iota(jnp.int32, (rows, cols), 0) // tq))
    sc = []
    for s in range(nb):
        q = q_ref[s * tq:(s + 1) * tq, :]
        qs = jnp.concatenate([q[:, h * HEAD_DIM:(h + 1) * HEAD_DIM] for h in range(XA_HEADS)], axis=0)
        sc.append(_dot_nt(qs.astype(BF16), k_ref[s].astype(BF16)) * scale)
    pr = [_softmax_rows(jnp.where(own, x, NEG_INF)).astype(BF16) for x in sc]
    for s in range(nb):
        o = _dot(pr[s], v_ref[s].astype(BF16))
        for h in range(XA_HEADS):
            o_ref[s * tq:(s + 1) * tq, h * HEAD_DIM:(h + 1) * HEAD_DIM] = o[h * tq:(h + 1) * tq].astype(o_ref.dtype)


def xattn(q, k, v, *, B, Lq, tq, q_blk=0, k_blk=0, v_blk=0):
    nq = Lq // tq
    return pl.pallas_call(
        _xattn_kernel,
        grid=(B, nq),
        in_specs=[
            pl.BlockSpec((tq, XA_WIDTH), lambda b, n: (b * nq + n, q_blk)),
            pl.BlockSpec((N_MEM, XA_WIDTH), lambda b, n: (b, k_blk)),
            pl.BlockSpec((N_MEM, XA_WIDTH), lambda b, n: (b, v_blk)),
        ],
        out_specs=pl.BlockSpec((tq, XA_WIDTH), lambda b, n: (b * nq + n, 0)),
        out_shape=jax.ShapeDtypeStruct((B * Lq, XA_WIDTH), BF16),
        compiler_params=_cparams(("parallel", "arbitrary")),
        name="xattn",
    )(q, k, v)


def xattn_cache(q, k, v, *, layer, B, tq, nb):
    assert B % nb == 0
    boff = layer * (B // nb)
    spec = pl.BlockSpec((nb, N_MEM * XA_HEADS, HEAD_DIM), lambda b: (boff + b, 0, 0))
    return pl.pallas_call(
        functools.partial(_xattn_cache_kernel, nb=nb),
        grid=(B // nb,),
        in_specs=[pl.BlockSpec((nb * tq, XA_WIDTH), lambda b: (b, 0)), spec, spec],
        out_specs=pl.BlockSpec((nb * tq, XA_WIDTH), lambda b: (b, 0)),
        out_shape=jax.ShapeDtypeStruct((B * tq, XA_WIDTH), BF16),
        compiler_params=_cparams(("parallel",)),
        name="xattn_cache",
    )(q, k, v)


PEER_RANKS = PEER_TOPK + 1
GATE_ROWS = 32


def _cand_counts():
    return [PEER_RANKS // (a + 1) for a in range(PEER_RANKS)]


def _peer_topk_kernel(q_ref, sk_ref, pw_ref, aux_ref, sv_ref, cand_ref):
    K = PEER_TOPK
    q = q_ref[...]
    s = []
    for p in range(2):
        qp = q[:, p * LANES:(p + 1) * LANES]
        s.append(_dot_nt(sk_ref[0, p], qp, precision=lax.Precision.HIGHEST))
    for p in range(2):
        cur = s[p]
        for r in range(PEER_RANKS):
            m = jnp.max(cur, axis=0, keepdims=True)
            sv_ref[p, r:r + 1, :] = m
            cur = jnp.where(cur == m, NEG_INF, cur)
    sv1 = sv_ref[0, 0:PEER_RANKS, :]
    sv2 = sv_ref[1, 0:PEER_RANKS, :]
    cand_ref[...] = jnp.full(cand_ref.shape, NEG_INF, F32)
    off = 0
    for a, nb in enumerate(_cand_counts()):
        cand_ref[off:off + nb, :] = sv1[a:a + 1, :] + sv2[0:nb, :]
        off += nb
    cur = cand_ref[...]
    tops = []
    for r in range(PEER_RANKS):
        m = jnp.max(cur, axis=0, keepdims=True)
        tops.append(m)
        cur = jnp.where(cur == m, NEG_INF, cur)
    z = jnp.ones_like(tops[0])
    for r in range(1, K):
        z = z + jnp.exp(tops[r] - tops[0])
    thr = 0.5 * (tops[K - 1] + tops[K])
    pw_ref[0, 0] = thr - s[0]
    pw_ref[0, 1] = s[1]
    pw_ref[0, 2] = jnp.exp(s[1] - sv2[0:1, :])
    aux_ref[0] = jnp.broadcast_to(thr - sv1[0:1, :] - jnp.log(z), aux_ref.shape[1:])


def peer_topk(q, subkeys, *, tt=512):
    T = q.shape[0]
    tt = min(tt, T)
    assert T % tt == 0
    H, NK = PEER_HEADS, PEER_NKEYS
    pad8 = lambda n: -(-n // SUBLANES) * SUBLANES
    return pl.pallas_call(
        _peer_topk_kernel,
        grid=(T // tt, H),
        in_specs=[
            pl.BlockSpec((tt, 2 * LANES), lambda i, h: (i, h)),
            pl.BlockSpec((1, 2, NK, LANES), lambda i, h: (h, 0, 0, 0)),
        ],
        out_specs=[
            pl.BlockSpec((1, 3, NK, tt), lambda i, h: (h, 0, 0, i)),
            pl.BlockSpec((1, SUBLANES, tt), lambda i, h: (h, 0, i)),
        ],
        out_shape=[
            jax.ShapeDtypeStruct((H, 3, NK, T), F32),
            jax.ShapeDtypeStruct((H, SUBLANES, T), F32),
        ],
        scratch_shapes=[pltpu.VMEM((2, pad8(PEER_RANKS), tt), F32), pltpu.VMEM((pad8(sum(_cand_counts())), tt), F32)],
        compiler_params=_cparams(("parallel", "arbitrary")),
        name="peer_topk",
    )(q, subkeys)


def _peer_dense_kernel(x_ref, g_ref, pw_ref, aux_ref, u_ref, v_ref, o_ref, xnT_ref, acc_ref, w_ref, *, ib):
    e = pl.program_id(1)
    H, NK = PEER_HEADS, PEER_NKEYS
    tt = x_ref.shape[0]
    strip = min(LANES, tt)
    assert tt % strip == 0

    def gates(blk, ii, dst):
        i = blk * ib + ii
        thr1 = [pw_ref[h, 0, pl.ds(i, 1), :] for h in range(H)]
        e1 = [0.5 * jnp.exp(aux_ref[h, 0:1, :] - thr1[h]) for h in range(H)]
        for tc in range(tt // strip):
            ts_ = slice(tc * strip, (tc + 1) * strip)
            for j0 in range(0, NK, GATE_ROWS):
                js = slice(j0, j0 + GATE_ROWS)
                w = None
                for h in range(H):
                    sel = pw_ref[h, 1, js, ts_] >= thr1[h][:, ts_]
                    c = jnp.where(sel, pw_ref[h, 2, js, ts_], 0.0) * e1[h][:, ts_]
                    w = c if w is None else w + c
                dst[ii * NK + j0:ii * NK + j0 + GATE_ROWS, ts_] = w

    @pl.when(e == 0)
    def _():
        xn = _rms(x_ref[...], g_ref[...])
        xnT_ref[...] = xn.T.astype(BF16)
        acc_ref[...] = jnp.zeros_like(acc_ref)

    hT = _dot(u_ref[...], xnT_ref[...])
    for ii in range(ib):
        gates(e, ii, w_ref)
    aT = w_ref[...] * _gelu_tanh_x2(hT)
    acc_ref[...] += _dot(aT.T.astype(BF16), v_ref[...])

    @pl.when(e == pl.num_programs(1) - 1)
    def _():
        o_ref[...] = x_ref[...] + acc_ref[...]


def peer_dense(x, g, pw, aux, u_tab, v_tab, *, layer=0, tt=512, ib=4):
    T, D = x.shape
    boff = layer * (PEER_NKEYS // ib)
    tt = min(tt, T)
    assert T % tt == 0 and PEER_NKEYS % ib == 0
    H, NK = PEER_HEADS, PEER_NKEYS
    eb = ib * NK
    kern = functools.partial(_peer_dense_kernel, ib=ib)
    return pl.pallas_call(
        kern,
        grid=(T // tt, NK // ib),
        in_specs=[
            pl.BlockSpec((tt, D), lambda i, e: (i, 0)),
            pl.BlockSpec((1, D), lambda i, e: (0, 0)),
            pl.BlockSpec((H, 3, NK, tt), lambda i, e: (0, 0, 0, i)),
            pl.BlockSpec((H, SUBLANES, tt), lambda i, e: (0, 0, i)),
            pl.BlockSpec((eb, D), lambda i, e: (boff + e, 0)),
            pl.BlockSpec((eb, D), lambda i, e: (boff + e, 0)),
        ],
        out_specs=pl.BlockSpec((tt, D), lambda i, e: (i, 0)),
        out_shape=jax.ShapeDtypeStruct((T, D), F32),
        scratch_shapes=[
            pltpu.VMEM((D, tt), BF16),
            pltpu.VMEM((tt, D), F32),
            pltpu.VMEM((eb, tt), F32),
        ],
        compiler_params=_cparams(("parallel", "arbitrary")),
        name="peer_dense",
    )(x, g.reshape(1, D), pw, aux, u_tab, v_tab)


def _prep_weights(w_in_a, w_in_b, w_out, w_mem_kv, peer_wq, peer_u, peer_v, lru_wr, lru_wi):
    H = GDN_HEADS
    qkvz = GDN_QKV + TM_WIDTH
    wb = jnp.concatenate(
        [
            w_in_b[:, :, :qkvz],
            w_in_b[:, :, qkvz + 2 * H:],
            w_in_b[:, :, qkvz:qkvz + 2 * H],
            jnp.zeros(w_in_b.shape[:2] + (B_IN_PAD - w_in_b.shape[2],), w_in_b.dtype),
        ],
        axis=-1,
    )
    return dict(
        w_in_a=w_in_a.astype(BF16),
        w_in_b=wb.astype(BF16),
        w_out_tm=w_out[:, :TM_WIDTH].astype(BF16),
        w_out_xa=w_out[:, TM_WIDTH:].astype(BF16),
        w_mem_kv=w_mem_kv.astype(BF16),
        peer_wq=peer_wq.astype(BF16),
        peer_u=peer_u.astype(BF16).reshape(-1, peer_u.shape[-1]),
        peer_v=peer_v.astype(BF16).reshape(-1, peer_v.shape[-1]),
        lru_wr=lru_wr.astype(BF16),
        lru_wi=lru_wi.astype(BF16),
    )


def _trunk(x, mem, lru_h0, lru_c0, gdn_S0, gdn_c0, w, wc, *, B, L, time_major):
    if time_major:
        G, R, tl = 1, B, L
    else:
        G, R, tl = B, 1, min(L, 256)
    Lp = -(-L // SUBLANES) * SUBLANES
    hist = CONV_W - 1

    def to_batch_major(a):
        c = a.shape[-1]
        a = a.reshape(L, B, c).transpose(1, 0, 2)
        return jnp.pad(a, ((0, 0), (0, Lp - L), (0, 0))).reshape(B * Lp, c)

    def to_time_major(a):
        c = a.shape[-1]
        return a.reshape(B, Lp, c)[:, :L].transpose(1, 0, 2).reshape(L * B, c)

    def last_rows(u, c0, c1):
        if time_major:
            return u.reshape(L, B, -1)[L - hist:, :, c0:c1].transpose(1, 0, 2)
        return u.reshape(B, L, -1)[:, L - hist:, c0:c1]

    n_b = gdn_S0.shape[0]
    S0_all = gdn_S0.reshape((n_b * B,) + gdn_S0.shape[2:])
    S_all = None
    lru_h, lru_c, gdn_c = [], [], []
    for l in range(DEPTH):
        j = l // 2
        if l % 2 == 0:
            u, qmem = norm_matmul(x, w['norm_mix'][l], wc['w_in_a'][j], side_cols=(2 * TM_WIDTH, XA_WIDTH))
            c0 = lru_c0[j]
            if time_major:
                c0k = c0.transpose(1, 0, 2).reshape(1, hist * B, TM_WIDTH)
                h0k = lru_h0[j].reshape(1, B, TM_WIDTH)
            else:
                c0k = c0
                h0k = lru_h0[j].reshape(B, 1, TM_WIDTH)
            tm, h = rglru(u, c0k, h0k, w['lru_conv_w'][j], w['lru_conv_b'][j], wc['lru_wr'][j], w['lru_br'][j],
                          wc['lru_wi'][j], w['lru_bi'][j], w['lru_lambda'][j], G=G, L=L, R=R, tl=tl)
            lru_h.append(h.reshape(B, TM_WIDTH))
            lru_c.append(last_rows(u, TM_WIDTH, 2 * TM_WIDTH))
        else:
            u, qmem = norm_matmul(x, w['norm_mix'][l], wc['w_in_b'][j], side_cols=(GDN_QKV + TM_WIDTH, XA_WIDTH))
            C = min(GDN_CHUNK, Lp)
            nb = GDN_SEQS_PER_STEP if (Lp == C and B % GDN_SEQS_PER_STEP == 0) else 1
            u_bm = to_batch_major(u) if time_major else u
            o_bm, S_all = gdn(u_bm, gdn_c0[j], S0_all, S_all, w['gdn_conv_w'][j], w['gdn_conv_b'][j],
                              w['gdn_a_log'][j], w['gdn_dt_bias'][j], w['gdn_o_norm'][j], layer=j, B=B, L=Lp, C=C,
                              n_valid=min(L, C), nb=nb, qkv_blk=0, z_blk=GDN_QKV // TM_WIDTH,
                              ba_blk=(GDN_QKV + TM_WIDTH + XA_WIDTH) // LANES)
            tm = to_time_major(o_bm) if time_major else o_bm
            gdn_c.append(last_rows(u, 0, GDN_QKV))
        if time_major:
            q_bm = to_batch_major(qmem)
            xa = to_time_major(xattn_cache(q_bm, mem[0], mem[1], layer=l, B=B, tq=Lp,
                                           nb=XATTN_SEQS_PER_STEP if B % XATTN_SEQS_PER_STEP == 0 else 1))
        else:
            xa = xattn(qmem, mem[l], mem[l], B=B, Lq=L, tq=min(L, 512), q_blk=0, k_blk=0, v_blk=1)
        x = out_proj(tm, xa, wc['w_out_tm'][l], wc['w_out_xa'][l], x)
        q = norm_matmul(x, w['norm_ffn'][l], wc['peer_wq'][l])
        pw, aux = peer_topk(q, w['peer_subkeys'][l])
        x = peer_dense(x, w['norm_ffn'][l], pw, aux, wc['peer_u'], wc['peer_v'], layer=l)
    y = rmsnorm_rows(x, w['norm_final'])
    return y, jnp.stack(lru_h), jnp.stack(lru_c), S_all.reshape(gdn_S0.shape), jnp.stack(gdn_c)


def kernel(x_prompt, x_sample, state_rglru_h, state_rglru_conv, state_gdn_S, state_gdn_conv, cache_mem_k, cache_mem_v, mem_prompt, norm_mix, norm_ffn, norm_final, w_in_a, w_in_b, w_out, lru_conv_w, lru_conv_b, lru_wr, lru_br, lru_wi, lru_bi, lru_lambda, gdn_conv_w, gdn_conv_b, gdn_a_log, gdn_dt_bias, gdn_o_norm, norm_mem, w_mem_kv, peer_wq, peer_subkeys, peer_u, peer_v):
    w = dict(norm_mix=norm_mix, norm_ffn=norm_ffn, norm_final=norm_final, lru_conv_w=lru_conv_w,
             lru_conv_b=lru_conv_b, lru_br=lru_br, lru_bi=lru_bi, lru_lambda=lru_lambda, gdn_conv_w=gdn_conv_w,
             gdn_conv_b=gdn_conv_b, gdn_a_log=gdn_a_log, gdn_dt_bias=gdn_dt_bias, gdn_o_norm=gdn_o_norm,
             peer_subkeys=peer_subkeys)
    wc = _prep_weights(w_in_a, w_in_b, w_out, w_mem_kv, peer_wq, peer_u, peer_v, lru_wr, lru_wi)
    Bp, Lp_, D = x_prompt.shape
    Bs, Ls, _ = x_sample.shape
    n_a, n_b = state_rglru_h.shape[0], state_gdn_S.shape[0]

    mem2 = mem_prompt.reshape(Bp * N_MEM, D)
    kv = [norm_matmul(mem2, norm_mem[l], wc['w_mem_kv'][l]) for l in range(DEPTH)]
    y_p, p_h, p_ca, p_S, p_cb = _trunk(
        x_prompt.reshape(Bp * Lp_, D), kv,
        jnp.zeros((n_a, Bp, TM_WIDTH), F32), jnp.zeros((n_a, Bp, CONV_W - 1, TM_WIDTH), F32),
        jnp.zeros((n_b, Bp, GDN_HEADS, HEAD_DIM, HEAD_DIM), F32), jnp.zeros((n_b, Bp, CONV_W - 1, GDN_QKV), F32),
        w, wc, B=Bp, L=Lp_, time_major=False)
    p_mem_k = jnp.stack([a[:, :XA_WIDTH].reshape(Bp, N_MEM, XA_HEADS, HEAD_DIM) for a in kv])
    p_mem_v = jnp.stack([a[:, XA_WIDTH:].reshape(Bp, N_MEM, XA_HEADS, HEAD_DIM) for a in kv])

    xs = x_sample.transpose(1, 0, 2).reshape(Ls * Bs, D)
    rows_hd = (DEPTH * Bs, N_MEM * XA_HEADS, HEAD_DIM)
    y_s, s_h, s_ca, s_S, s_cb = _trunk(xs, (cache_mem_k.reshape(rows_hd), cache_mem_v.reshape(rows_hd)),
                                       state_rglru_h, state_rglru_conv,
                                       state_gdn_S, state_gdn_conv, w, wc, B=Bs, L=Ls, time_major=True)
    y_s = y_s.reshape(Ls, Bs, D).transpose(1, 0, 2)
    return (y_p.reshape(Bp, Lp_, D), y_s, p_h, p_ca, p_S, p_cb, p_mem_k, p_mem_v, s_h, s_ca, s_S, s_cb)
```

```python
import functools
import math

import jax
import jax.numpy as jnp
from jax import lax
from jax.experimental import pallas as pl
from jax.experimental.pallas import tpu as pltpu

F32 = jnp.float32
BF16 = jnp.bfloat16

D_MODEL = 2048
DEPTH = 4
HEAD_DIM = 128
XA_HEADS = 4
XA_WIDTH = XA_HEADS * HEAD_DIM
TM_WIDTH = D_MODEL - XA_WIDTH
N_MEM = 256
CONV_W = 4
EPS = 1e-6
LRU_BLOCKS = TM_WIDTH // HEAD_DIM
LRU_C = 8.0
GDN_HEADS = TM_WIDTH // HEAD_DIM
GDN_QKV = 3 * TM_WIDTH
GDN_CHUNK = 64
GDN_SEQS_PER_STEP = 4
XATTN_SEQS_PER_STEP = 4
PEER_HEADS = 8
PEER_NKEYS = 128
PEER_TOPK = 16
B_IN_PAD = 7168

SUBLANES = 8
LANES = 128
VMEM_LIMIT = 52 * 1024 * 1024

NEG_INF = float("-inf")


def _cparams(sem):
    return pltpu.CompilerParams(dimension_semantics=sem, vmem_limit_bytes=VMEM_LIMIT)


def _dot(a, b, precision=None):
    return jnp.dot(a, b, preferred_element_type=F32, precision=precision)


def _mm(a, b):
    return jnp.dot(a.astype(BF16), b.astype(BF16), preferred_element_type=F32)


def _split_bf16(a):
    hi = a.astype(BF16)
    return hi, (a - hi.astype(F32)).astype(BF16)


def _dot3(a, b):
    ah, al = _split_bf16(a)
    bh, bl = _split_bf16(b)
    return _dot(ah, bh) + (_dot(ah, bl) + _dot(al, bh))


def _dot_nt(a, b, precision=None):
    return lax.dot_general(a, b, (((1,), (1,)), ((), ())), preferred_element_type=F32, precision=precision)


def _dot_tn(a, b, precision=None):
    return lax.dot_general(a, b, (((0,), (0,)), ((), ())), preferred_element_type=F32, precision=precision)


def _sigmoid(x):
    return 1.0 / (1.0 + jnp.exp(-x))


def _silu(x):
    return x * _sigmoid(x)


def _gelu_tanh(x):
    c = math.sqrt(2.0 / math.pi)
    return 0.5 * x * (1.0 + jnp.tanh(c * (x + 0.044715 * (x * x * x))))


def _gelu_tanh_x2(x):
    c = math.sqrt(2.0 / math.pi)
    return x * (1.0 + jnp.tanh(x * (c + (0.044715 * c) * (x * x))))


def _softplus(x):
    return jnp.maximum(x, 0.0) + jnp.log(1.0 + jnp.exp(-jnp.abs(x)))


def _rms(x, g):
    ms = jnp.mean(x * x, axis=-1, keepdims=True)
    return x * lax.rsqrt(ms + EPS) * g


def _norm_matmul_kernel(x_ref, g_ref, w_ref, o_ref, *rest, side):
    xn_ref = rest[-1]

    @pl.when(pl.program_id(1) == 0)
    def _():
        xn_ref[...] = _rms(x_ref[...], g_ref[...]).astype(BF16)

    acc = _dot(xn_ref[...], w_ref[...])
    o_ref[...] = acc
    if side is not None:
        tile, off, width = side

        @pl.when(pl.program_id(1) == tile)
        def _():
            rest[0][...] = acc[:, off:off + width]


def _matmul_tiles(m, n, tm, tn):
    tm = min(tm, m)
    while m % tm:
        tm //= 2
    while n % tn:
        tn //= 2
    return tm, tn


def norm_matmul(x, g, w, *, side_cols=None, tm=1024, tn=1024):
    m, k = x.shape
    n = w.shape[1]
    tm, tn = _matmul_tiles(m, n, tm, tn)
    assert tm % SUBLANES == 0 and tn % LANES == 0
    out_specs = [pl.BlockSpec((tm, tn), lambda i, j: (i, j))]
    out_shape = [jax.ShapeDtypeStruct((m, n), F32)]
    side = None
    if side_cols is not None:
        start, width = side_cols
        side = (start // tn, start % tn, width)
        assert start % tn + width <= tn
        out_specs.append(pl.BlockSpec((tm, width), lambda i, j: (i, 0)))
        out_shape.append(jax.ShapeDtypeStruct((m, width), F32))
    outs = pl.pallas_call(
        functools.partial(_norm_matmul_kernel, side=side),
        grid=(m // tm, n // tn),
        in_specs=[
            pl.BlockSpec((tm, k), lambda i, j: (i, 0)),
            pl.BlockSpec((1, k), lambda i, j: (0, 0)),
            pl.BlockSpec((k, tn), lambda i, j: (0, j)),
        ],
        out_specs=out_specs,
        out_shape=out_shape,
        scratch_shapes=[pltpu.VMEM((tm, k), BF16)],
        compiler_params=_cparams(("parallel", "arbitrary")),
        name="norm_matmul",
    )(x, g.reshape(1, k), w)
    return outs[0] if side_cols is None else tuple(outs)


def _out_proj_kernel(a_ref, b_ref, wa_ref, wb_ref, r_ref, o_ref):
    acc = _dot(a_ref[...], wa_ref[...]) + _dot(b_ref[...], wb_ref[...])
    o_ref[...] = r_ref[...] + acc


def out_proj(a, b, wa, wb, r, *, tm=1024, tn=1024):
    m, ka = a.shape
    kb = b.shape[1]
    n = wa.shape[1]
    tm, tn = _matmul_tiles(m, n, tm, tn)
    assert tm % SUBLANES == 0 and tn % LANES == 0
    return pl.pallas_call(
        _out_proj_kernel,
        grid=(m // tm, n // tn),
        in_specs=[
            pl.BlockSpec((tm, ka), lambda i, j: (i, 0)),
            pl.BlockSpec((tm, kb), lambda i, j: (i, 0)),
            pl.BlockSpec((ka, tn), lambda i, j: (0, j)),
            pl.BlockSpec((kb, tn), lambda i, j: (0, j)),
            pl.BlockSpec((tm, tn), lambda i, j: (i, j)),
        ],
        out_specs=pl.BlockSpec((tm, tn), lambda i, j: (i, j)),
        out_shape=jax.ShapeDtypeStruct((m, n), F32),
        compiler_params=_cparams(("parallel", "arbitrary")),
        name="out_proj",
    )(a, b, wa, wb, r)


def _rmsnorm_kernel(x_ref, g_ref, o_ref):
    o_ref[...] = _rms(x_ref[...], g_ref[...])


def rmsnorm_rows(x, g, *, tm=512):
    m, k = x.shape
    tm = min(tm, m)
    assert m % tm == 0
    return pl.pallas_call(
        _rmsnorm_kernel,
        grid=(m // tm,),
        in_specs=[pl.BlockSpec((tm, k), lambda i: (i, 0)), pl.BlockSpec((1, k), lambda i: (0, 0))],
        out_specs=pl.BlockSpec((tm, k), lambda i: (i, 0)),
        out_shape=jax.ShapeDtypeStruct((m, k), F32),
        compiler_params=_cparams(("parallel",)),
        name="final_rmsnorm",
    )(x, g.reshape(1, k))


def _rglru_kernel(gate_ref, xr_ref, cw_ref, cb_ref, c0_ref, h0_ref, wr_ref, br_ref, wi_ref, bi_ref, lam_ref,
                  y_ref, hout_ref, hist_ref, a_ref, b_ref, hs_ref, h_ref, *, tl, R):
    n = pl.program_id(1)
    rows = tl * R
    hist_rows = (CONV_W - 1) * R
    off0 = -(-hist_rows // SUBLANES) * SUBLANES

    @pl.when(n == 0)
    def _():
        hist_ref[off0 - hist_rows:off0, :] = c0_ref[0]
        h_ref[...] = h0_ref[0]

    x = xr_ref[...]
    hist_ref[off0:off0 + rows, :] = x
    xc = cb_ref[...]
    for j in range(CONV_W):
        s = off0 - hist_rows + j * R
        xc = xc + hist_ref[s:s + rows, :] * cw_ref[j:j + 1, :]
    tail = hist_ref[off0 + rows - hist_rows:off0 + rows, :]
    hist_ref[off0 - hist_rows:off0, :] = tail

    sp = _softplus(-lam_ref[...])
    for blk in range(LRU_BLOCKS):
        cs = slice(blk * HEAD_DIM, (blk + 1) * HEAD_DIM)
        xb = xc[:, cs]
        xb16 = xb.astype(BF16)
        r = _sigmoid(_dot(xb16, wr_ref[blk]) + br_ref[blk])
        i = _sigmoid(_dot(xb16, wi_ref[blk]) + bi_ref[blk])
        log_a = (-LRU_C) * r * sp[:, cs]
        a = jnp.exp(log_a)
        a_ref[:, cs] = a
        b_ref[:, cs] = jnp.sqrt(1.0 - a * a) * (i * xb)

    def step(l, carry):
        if R % SUBLANES == 0:
            rs = pl.ds(pl.multiple_of(l * R, SUBLANES), R)
        else:
            rs = pl.ds(l * R, R)
        h = a_ref[rs, :] * h_ref[...] + b_ref[rs, :]
        h_ref[...] = h
        hs_ref[rs, :] = h
        return carry

    lax.fori_loop(0, tl, step, 0)
    y_ref[...] = (hs_ref[...] * _gelu_tanh(gate_ref[...])).astype(y_ref.dtype)
    hout_ref[0] = h_ref[...]


def rglru(u, c0, h0, cw, cb, wr, br, wi, bi, lam, *, G, L, R, tl):
    W = TM_WIDTH
    assert L % tl == 0
    rows = tl * R
    nt = L // tl
    hist_rows = (CONV_W - 1) * R
    off0 = -(-hist_rows // SUBLANES) * SUBLANES
    kern = functools.partial(_rglru_kernel, tl=tl, R=R)
    const2 = lambda g, n: (0, 0)
    const3 = lambda g, n: (0, 0, 0)
    y, hout = pl.pallas_call(
        kern,
        grid=(G, nt),
        in_specs=[
            pl.BlockSpec((rows, W), lambda g, n: (g * nt + n, 0)),
            pl.BlockSpec((rows, W), lambda g, n: (g * nt + n, 1)),
            pl.BlockSpec((CONV_W, W), const2),
            pl.BlockSpec((1, W), const2),
            pl.BlockSpec((1, hist_rows, W), lambda g, n: (g, 0, 0)),
            pl.BlockSpec((1, R, W), lambda g, n: (g, 0, 0)),
            pl.BlockSpec((LRU_BLOCKS, HEAD_DIM, HEAD_DIM), const3),
            pl.BlockSpec((LRU_BLOCKS, 1, HEAD_DIM), const3),
            pl.BlockSpec((LRU_BLOCKS, HEAD_DIM, HEAD_DIM), const3),
            pl.BlockSpec((LRU_BLOCKS, 1, HEAD_DIM), const3),
            pl.BlockSpec((1, W), const2),
        ],
        out_specs=[
            pl.BlockSpec((rows, W), lambda g, n: (g * nt + n, 0)),
            pl.BlockSpec((1, R, W), lambda g, n: (g, 0, 0)),
        ],
        out_shape=[
            jax.ShapeDtypeStruct((G * L * R, W), BF16),
            jax.ShapeDtypeStruct((G, R, W), F32),
        ],
        scratch_shapes=[
            pltpu.VMEM((off0 + rows, W), F32),
            pltpu.VMEM((rows, W), F32),
            pltpu.VMEM((rows, W), F32),
            pltpu.VMEM((rows, W), F32),
            pltpu.VMEM((R, W), F32),
        ],
        compiler_params=_cparams(("parallel", "arbitrary")),
        name="rglru",
    )(u, u, cw, cb.reshape(1, W), c0, h0, wr, br.reshape(LRU_BLOCKS, 1, HEAD_DIM), wi,
      bi.reshape(LRU_BLOCKS, 1, HEAD_DIM), lam.reshape(1, W))
    return y, hout


def _inv_unit_lower_many(As, C, row, col):
    eye = (row == col).astype(F32)
    base = min(SUBLANES, C)
    sh = int(math.log2(base))
    dmask = (row >> sh) == (col >> sh)
    Ps = [jnp.where(dmask, A, 0.0) for A in As]
    Ts = [eye - P for P in Ps]
    span = 2
    while span < base:
        Ps = [_dot3(P, P) for P in Ps]
        Ts = [_dot3(T, eye + P) for T, P in zip(Ts, Ps)]
        span *= 2
    s = base
    while s < C:
        sh = int(math.log2(s))
        off = ((row >> (sh + 1)) == (col >> (sh + 1))) & ((row >> sh) != (col >> sh))
        Ms = [_dot3(jnp.where(off, A, 0.0), T) for A, T in zip(As, Ts)]
        Ts = [T - _dot3(T, M) for T, M in zip(Ts, Ms)]
        s *= 2
    return Ts


def _gdn_kernel(qkv_ref, z_ref, ba_ref, cw_ref, cb_ref, c0_ref, S0_ref, alog_ref, dtb_ref, onorm_ref,
                *rest, C, n_valid, nb):
    o_ref, S_ref, hist_ref = rest[-3:]
    n = pl.program_id(1)
    H, Dh = GDN_HEADS, HEAD_DIM
    hist_rows = CONV_W - 1
    off0 = SUBLANES
    rows = nb * C

    @pl.when(n == 0)
    def _():
        for s in range(nb):
            hist_ref[s, off0 - hist_rows:off0, :] = c0_ref[s]
        S_ref[...] = S0_ref[...]

    ts = []
    for s in range(nb):
        hist_ref[s, off0:off0 + C, :] = qkv_ref[s * C:(s + 1) * C, :]
        xc = cb_ref[...]
        for j in range(CONV_W):
            r0 = off0 - hist_rows + j
            xc = xc + hist_ref[s, r0:r0 + C, :] * cw_ref[j:j + 1, :]
        tail = hist_ref[s, off0 + C - hist_rows:off0 + C, :]
        hist_ref[s, off0 - hist_rows:off0, :] = tail
        ts.append(_silu(xc))

    ba = ba_ref[...]
    beta_all = _sigmoid(ba)
    g_all = -jnp.exp(alog_ref[...]) * _softplus(ba + dtb_ref[...])
    if n_valid < C:
        rmask = (lax.broadcasted_iota(jnp.int32, (rows, LANES), 0) & (C - 1)) < n_valid
        beta_all = jnp.where(rmask, beta_all, 0.0)
        g_all = jnp.where(rmask, g_all, 0.0)
    if rows < LANES:
        g_pad = jnp.concatenate([g_all, jnp.zeros((LANES - rows, LANES), F32)], axis=0)
    else:
        g_pad = g_all
    r128 = lax.broadcasted_iota(jnp.int32, (LANES, LANES), 0)
    c128 = lax.broadcasted_iota(jnp.int32, (LANES, LANES), 1)
    shc = int(math.log2(C))
    tril = ((c128 <= r128) & ((c128 >> shc) == (r128 >> shc))).astype(F32)
    gc_pad = _dot(tril, g_pad, precision=lax.Precision.HIGHEST)
    gcT = gc_pad.T

    row = lax.broadcasted_iota(jnp.int32, (C, C), 0)
    col = lax.broadcasted_iota(jnp.int32, (C, C), 1)
    incl = col <= row
    strict = col < row

    probs = [(s, h) for s in range(nb) for h in range(H)]
    qn, kn, kn16, vh, bcol, gcol, decay, egc = {}, {}, {}, {}, {}, {}, {}, {}
    for p in probs:
        s, h = p
        t = ts[s]
        qh = t[:, h * Dh:(h + 1) * Dh]
        kh = t[:, TM_WIDTH + h * Dh:TM_WIDTH + (h + 1) * Dh]
        vh[p] = t[:, 2 * TM_WIDTH + h * Dh:2 * TM_WIDTH + (h + 1) * Dh]
        qn[p] = qh * lax.rsqrt(jnp.sum(qh * qh, -1, keepdims=True) + EPS) * (Dh ** -0.5)
        kn[p] = kh * lax.rsqrt(jnp.sum(kh * kh, -1, keepdims=True) + EPS)
        kn16[p] = kn[p].astype(BF16)
        bcol[p] = beta_all[s * C:(s + 1) * C, h:h + 1]
        gcol[p] = gc_pad[s * C:(s + 1) * C, H + h:H + h + 1]
        grow = gcT[H + h:H + h + 1, s * C:(s + 1) * C]
        diff = gcol[p] - grow
        decay[p] = jnp.where(incl, jnp.exp(jnp.where(incl, diff, 0.0)), 0.0)
        egc[p] = jnp.exp(gcol[p])
    kk = {p: _dot_nt(kn16[p], kn16[p]) for p in probs}
    qkr = {p: _dot_nt(qn[p].astype(BF16), kn16[p]) for p in probs}
    As = [jnp.where(strict, bcol[p] * kk[p] * decay[p], 0.0) for p in probs]
    Ts = _inv_unit_lower_many(As, C, row, col)
    rhs = [jnp.concatenate([bcol[p] * vh[p], (bcol[p] * egc[p]) * kn[p]], axis=1) for p in probs]
    sol = {p: _dot3(T, r) for p, T, r in zip(probs, Ts, rhs)}
    S = {p: S_ref[p[0], p[1]] for p in probs}
    S16 = {p: S[p].astype(BF16) for p in probs}
    kS = {p: _dot(sol[p][:, Dh:].astype(BF16), S16[p]) for p in probs}
    qS = {p: _dot((qn[p] * egc[p]).astype(BF16), S16[p]) for p in probs}
    u_new = {p: (sol[p][:, :Dh] - kS[p]).astype(BF16) for p in probs}
    qk16 = {p: jnp.where(incl, qkr[p] * decay[p], 0.0).astype(BF16) for p in probs}
    o = {p: qS[p] + _dot(qk16[p], u_new[p]) for p in probs}
    dS = {}
    for p in probs:
        g_last = gcol[p][C - 1:C, :]
        kdec = (kn[p] * jnp.exp(g_last - gcol[p])).astype(BF16)
        dS[p] = _dot_tn(kdec, u_new[p])
    for p in probs:
        s, h = p
        g_last = gcol[p][C - 1:C, :]
        S_ref[s, h] = S[p] * jnp.exp(g_last) + dS[p]
        on = o[p] * lax.rsqrt(jnp.mean(o[p] * o[p], -1, keepdims=True) + EPS) * onorm_ref[...]
        zs = z_ref[s * C:(s + 1) * C, h * Dh:(h + 1) * Dh]
        o_ref[s * C:(s + 1) * C, h * Dh:(h + 1) * Dh] = (on * _silu(zs)).astype(o_ref.dtype)


def gdn(u, c0, S0_all, S_out_prev, cw, cb, a_log, dt_bias, o_norm, *, layer, B, L, C, n_valid, nb, qkv_blk, z_blk,
        ba_blk):
    H, Dh = GDN_HEADS, HEAD_DIM
    nc = L // C
    assert B % nb == 0 and (nb == 1 or nc == 1) and nb * C <= LANES
    rows = nb * C
    soff = layer * (B // nb)
    alog = jnp.zeros((1, LANES), F32).at[0, H:2 * H].set(a_log)
    dtb = jnp.zeros((1, LANES), F32).at[0, H:2 * H].set(dt_bias)
    kern = functools.partial(_gdn_kernel, C=C, n_valid=n_valid, nb=nb)
    const2 = lambda b, n: (0, 0)
    in_specs = [
        pl.BlockSpec((rows, GDN_QKV), lambda b, n: (b * nc + n, qkv_blk)),
        pl.BlockSpec((rows, TM_WIDTH), lambda b, n: (b * nc + n, z_blk)),
        pl.BlockSpec((rows, LANES), lambda b, n: (b * nc + n, ba_blk)),
        pl.BlockSpec((CONV_W, GDN_QKV), const2),
        pl.BlockSpec((1, GDN_QKV), const2),
        pl.BlockSpec((nb, CONV_W - 1, GDN_QKV), lambda b, n: (b, 0, 0)),
        pl.BlockSpec((nb, H, Dh, Dh), lambda b, n: (soff + b, 0, 0, 0)),
        pl.BlockSpec((1, LANES), const2),
        pl.BlockSpec((1, LANES), const2),
        pl.BlockSpec((1, Dh), const2),
    ]
    args = [u, u, u, cw, cb.reshape(1, GDN_QKV), c0, S0_all, alog, dtb, o_norm.reshape(1, Dh)]
    aliases = {}
    if S_out_prev is not None:
        in_specs.append(pl.BlockSpec(memory_space=pl.ANY))
        args.append(S_out_prev)
        aliases = {len(args) - 1: 1}
    o, S = pl.pallas_call(
        kern,
        grid=(B // nb, nc),
        in_specs=in_specs,
        out_specs=[
            pl.BlockSpec((rows, TM_WIDTH), lambda b, n: (b * nc + n, 0)),
            pl.BlockSpec((nb, H, Dh, Dh), lambda b, n: (soff + b, 0, 0, 0)),
        ],
        out_shape=[
            jax.ShapeDtypeStruct((B * L, TM_WIDTH), BF16),
            jax.ShapeDtypeStruct(S0_all.shape, F32),
        ],
        scratch_shapes=[pltpu.VMEM((nb, SUBLANES + C, GDN_QKV), F32)],
        input_output_aliases=aliases,
        compiler_params=_cparams(("parallel", "arbitrary")),
        name="gdn",
    )(*args)
    return o, S


def _softmax_rows(s):
    e = jnp.exp(s - jnp.max(s, axis=-1, keepdims=True))
    return e / jnp.sum(e, axis=-1, keepdims=True)


def _xattn_kernel(q_ref, k_ref, v_ref, o_ref):
    scale = HEAD_DIM ** -0.5
    for h in range(XA_HEADS):
        cs = slice(h * HEAD_DIM, (h + 1) * HEAD_DIM)
        kh = k_ref[:, cs].astype(BF16)
        vh = v_ref[:, cs].astype(BF16)
        p = _softmax_rows(_dot_nt(q_ref[:, cs].astype(BF16), kh) * scale)
        o_ref[:, cs] = _dot(p.astype(BF16), vh).astype(o_ref.dtype)


def _xattn_cache_kernel(q_ref, k_ref, v_ref, o_ref, *, nb):
    scale = HEAD_DIM ** -0.5
    tq = q_ref.shape[0] // nb
    rows = XA_HEADS * tq
    cols = N_MEM * XA_HEADS
    own = ((lax.broadcasted_iota(jnp.int32, (rows, cols), 1) % XA_HEADS)
           == (lax.broadcasted_iota(jnp.int32, (rows, cols), 0) // tq))
    sc = []
    for s in range(nb):
        q = q_ref[s * tq:(s + 1) * tq, :]
        qs = jnp.concatenate([q[:, h * HEAD_DIM:(h + 1) * HEAD_DIM] for h in range(XA_HEADS)], axis=0)
        sc.append(_dot_nt(qs.astype(BF16), k_ref[s].astype(BF16)) * scale)
    pr = [_softmax_rows(jnp.where(own, x, NEG_INF)).astype(BF16) for x in sc]
    for s in range(nb):
        o = _dot(pr[s], v_ref[s].astype(BF16))
        for h in range(XA_HEADS):
            o_ref[s * tq:(s + 1) * tq, h * HEAD_DIM:(h + 1) * HEAD_DIM] = o[h * tq:(h + 1) * tq].astype(o_ref.dtype)


def xattn(q, k, v, *, B, Lq, tq, q_blk=0, k_blk=0, v_blk=0):
    nq = Lq // tq
    return pl.pallas_call(
        _xattn_kernel,
        grid=(B, nq),
        in_specs=[
            pl.BlockSpec((tq, XA_WIDTH), lambda b, n: (b * nq + n, q_blk)),
            pl.BlockSpec((N_MEM, XA_WIDTH), lambda b, n: (b, k_blk)),
            pl.BlockSpec((N_MEM, XA_WIDTH), lambda b, n: (b, v_blk)),
        ],
        out_specs=pl.BlockSpec((tq, XA_WIDTH), lambda b, n: (b * nq + n, 0)),
        out_shape=jax.ShapeDtypeStruct((B * Lq, XA_WIDTH), BF16),
        compiler_params=_cparams(("parallel", "arbitrary")),
        name="xattn",
    )(q, k, v)


def xattn_cache(q, k, v, *, layer, B, tq, nb):
    assert B % nb == 0
    boff = layer * (B // nb)
    spec = pl.BlockSpec((nb, N_MEM * XA_HEADS, HEAD_DIM), lambda b: (boff + b, 0, 0))
    return pl.pallas_call(
        functools.partial(_xattn_cache_kernel, nb=nb),
        grid=(B // nb,),
        in_specs=[pl.BlockSpec((nb * tq, XA_WIDTH), lambda b: (b, 0)), spec, spec],
        out_specs=pl.BlockSpec((nb * tq, XA_WIDTH), lambda b: (b, 0)),
        out_shape=jax.ShapeDtypeStruct((B * tq, XA_WIDTH), BF16),
        compiler_params=_cparams(("parallel",)),
        name="xattn_cache",
    )(q, k, v)


PEER_RANKS = PEER_TOPK + 1
GATE_ROWS = 32


def _cand_counts():
    return [PEER_RANKS // (a + 1) for a in range(PEER_RANKS)]


def _peer_topk_kernel(q_ref, sk_ref, pw_ref, aux_ref, sv_ref, cand_ref):
    K = PEER_TOPK
    q = q_ref[...]
    s = []
    for p in range(2):
        qp = q[:, p * LANES:(p + 1) * LANES]
        s.append(_dot_nt(sk_ref[0, p], qp, precision=lax.Precision.HIGHEST))
    for p in range(2):
        cur = s[p]
        for r in range(PEER_RANKS):
            m = jnp.max(cur, axis=0, keepdims=True)
            sv_ref[p, r:r + 1, :] = m
            cur = jnp.where(cur == m, NEG_INF, cur)
    sv1 = sv_ref[0, 0:PEER_RANKS, :]
    sv2 = sv_ref[1, 0:PEER_RANKS, :]
    cand_ref[...] = jnp.full(cand_ref.shape, NEG_INF, F32)
    off = 0
    for a, nb in enumerate(_cand_counts()):
        cand_ref[off:off + nb, :] = sv1[a:a + 1, :] + sv2[0:nb, :]
        off += nb
    cur = cand_ref[...]
    tops = []
    for r in range(PEER_RANKS):
        m = jnp.max(cur, axis=0, keepdims=True)
        tops.append(m)
        cur = jnp.where(cur == m, NEG_INF, cur)
    z = jnp.ones_like(tops[0])
    for r in range(1, K):
        z = z + jnp.exp(tops[r] - tops[0])
    thr = 0.5 * (tops[K - 1] + tops[K])
    pw_ref[0, 0] = thr - s[0]
    pw_ref[0, 1] = s[1]
    pw_ref[0, 2] = jnp.exp(s[1] - sv2[0:1, :])
    aux_ref[0] = jnp.broadcast_to(thr - sv1[0:1, :] - jnp.log(z), aux_ref.shape[1:])


def peer_topk(q, subkeys, *, tt=512):
    T = q.shape[0]
    tt = min(tt, T)
    assert T % tt == 0
    H, NK = PEER_HEADS, PEER_NKEYS
    pad8 = lambda n: -(-n // SUBLANES) * SUBLANES
    return pl.pallas_call(
        _peer_topk_kernel,
        grid=(T // tt, H),
        in_specs=[
            pl.BlockSpec((tt, 2 * LANES), lambda i, h: (i, h)),
            pl.BlockSpec((1, 2, NK, LANES), lambda i, h: (h, 0, 0, 0)),
        ],
        out_specs=[
            pl.BlockSpec((1, 3, NK, tt), lambda i, h: (h, 0, 0, i)),
            pl.BlockSpec((1, SUBLANES, tt), lambda i, h: (h, 0, i)),
        ],
        out_shape=[
            jax.ShapeDtypeStruct((H, 3, NK, T), F32),
            jax.ShapeDtypeStruct((H, SUBLANES, T), F32),
        ],
        scratch_shapes=[pltpu.VMEM((2, pad8(PEER_RANKS), tt), F32), pltpu.VMEM((pad8(sum(_cand_counts())), tt), F32)],
        compiler_params=_cparams(("parallel", "arbitrary")),
        name="peer_topk",
    )(q, subkeys)


def _peer_dense_kernel(x_ref, g_ref, pw_ref, aux_ref, u_ref, v_ref, o_ref, *rest, ib, emit_tables):
    xnT_ref, acc_ref, w_ref = rest[-3:]
    e = pl.program_id(1)
    H, NK = PEER_HEADS, PEER_NKEYS
    tt = x_ref.shape[0]
    strip = min(LANES, tt)
    assert tt % strip == 0

    def gates(blk, ii, dst):
        i = blk * ib + ii
        thr1 = [pw_ref[h, 0, pl.ds(i, 1), :] for h in range(H)]
        e1 = [0.5 * jnp.exp(aux_ref[h, 0:1, :] - thr1[h]) for h in range(H)]
        for tc in range(tt // strip):
            ts_ = slice(tc * strip, (tc + 1) * strip)
            for j0 in range(0, NK, GATE_ROWS):
                js = slice(j0, j0 + GATE_ROWS)
                w = None
                for h in range(H):
                    sel = pw_ref[h, 1, js, ts_] >= thr1[h][:, ts_]
                    c = jnp.where(sel, pw_ref[h, 2, js, ts_], 0.0) * e1[h][:, ts_]
                    w = c if w is None else w + c
                dst[ii * NK + j0:ii * NK + j0 + GATE_ROWS, ts_] = w

    @pl.when(e == 0)
    def _():
        xn = _rms(x_ref[...], g_ref[...])
        xnT_ref[...] = xn.T.astype(BF16)
        acc_ref[...] = jnp.zeros_like(acc_ref)

    u16 = u_ref[...].astype(BF16)
    v16 = v_ref[...].astype(BF16)
    if emit_tables:
        rest[0][...] = u16
        rest[1][...] = v16
    hT = _dot(u16, xnT_ref[...])
    for ii in range(ib):
        gates(e, ii, w_ref)
    aT = w_ref[...] * _gelu_tanh_x2(hT)
    acc_ref[...] += _dot(aT.T.astype(BF16), v16)

    @pl.when(e == pl.num_programs(1) - 1)
    def _():
        o_ref[...] = x_ref[...] + acc_ref[...]


def peer_dense(x, g, pw, aux, u_tab, v_tab, *, layer=0, emit_tables=False, tt=512, ib=None):
    T, D = x.shape
    if ib is None:
        ib = 2 if emit_tables else 4
    boff = layer * (PEER_NKEYS // ib)
    tt = min(tt, T)
    assert T % tt == 0 and PEER_NKEYS % ib == 0 and not (emit_tables and T != tt)
    H, NK = PEER_HEADS, PEER_NKEYS
    eb = ib * NK
    out_specs = [pl.BlockSpec((tt, D), lambda i, e: (i, 0))]
    out_shape = [jax.ShapeDtypeStruct((T, D), F32)]
    if emit_tables:
        out_specs += [pl.BlockSpec((eb, D), lambda i, e: (e, 0))] * 2
        out_shape += [jax.ShapeDtypeStruct((NK * NK, D), BF16)] * 2
    outs = pl.pallas_call(
        functools.partial(_peer_dense_kernel, ib=ib, emit_tables=emit_tables),
        grid=(T // tt, NK // ib),
        in_specs=[
            pl.BlockSpec((tt, D), lambda i, e: (i, 0)),
            pl.BlockSpec((1, D), lambda i, e: (0, 0)),
            pl.BlockSpec((H, 3, NK, tt), lambda i, e: (0, 0, 0, i)),
            pl.BlockSpec((H, SUBLANES, tt), lambda i, e: (0, 0, i)),
            pl.BlockSpec((eb, D), lambda i, e: (boff + e, 0)),
            pl.BlockSpec((eb, D), lambda i, e: (boff + e, 0)),
        ],
        out_specs=out_specs,
        out_shape=out_shape,
        scratch_shapes=[
            pltpu.VMEM((D, tt), BF16),
            pltpu.VMEM((tt, D), F32),
            pltpu.VMEM((eb, tt), F32),
        ],
        compiler_params=_cparams(("parallel", "arbitrary")),
        name="peer_dense",
    )(x, g.reshape(1, D), pw, aux, u_tab, v_tab)
    return tuple(outs) if emit_tables else outs[0]


def _prep_weights(w_in_a, w_in_b, w_out, w_mem_kv, peer_wq, peer_u, peer_v, lru_wr, lru_wi):
    H = GDN_HEADS
    qkvz = GDN_QKV + TM_WIDTH
    wb = jnp.concatenate(
        [
            w_in_b[:, :, :qkvz],
            w_in_b[:, :, qkvz + 2 * H:],
            w_in_b[:, :, qkvz:qkvz + 2 * H],
            jnp.zeros(w_in_b.shape[:2] + (B_IN_PAD - w_in_b.shape[2],), w_in_b.dtype),
        ],
        axis=-1,
    )
    return dict(
        w_in_a=w_in_a.astype(BF16),
        w_in_b=wb.astype(BF16),
        w_out_tm=w_out[:, :TM_WIDTH].astype(BF16),
        w_out_xa=w_out[:, TM_WIDTH:].astype(BF16),
        w_mem_kv=w_mem_kv.astype(BF16),
        peer_wq=peer_wq.astype(BF16),
        peer_u=peer_u.reshape(-1, peer_u.shape[-1]),
        peer_v=peer_v.reshape(-1, peer_v.shape[-1]),
        lru_wr=lru_wr.astype(BF16),
        lru_wi=lru_wi.astype(BF16),
    )


def _trunk(x, mem, lru_h0, lru_c0, gdn_S0, gdn_c0, w, wc, *, B, L, time_major, peer_tabs16=None):
    if time_major:
        G, R, tl = 1, B, L
    else:
        G, R, tl = B, 1, min(L, 256)
    Lp = -(-L // SUBLANES) * SUBLANES
    hist = CONV_W - 1

    def to_batch_major(a):
        c = a.shape[-1]
        a = a.reshape(L, B, c).transpose(1, 0, 2)
        return jnp.pad(a, ((0, 0), (0, Lp - L), (0, 0))).reshape(B * Lp, c)

    def to_time_major(a):
        c = a.shape[-1]
        return a.reshape(B, Lp, c)[:, :L].transpose(1, 0, 2).reshape(L * B, c)

    def last_rows(u, c0, c1):
        if time_major:
            return u.reshape(L, B, -1)[L - hist:, :, c0:c1].transpose(1, 0, 2)
        return u.reshape(B, L, -1)[:, L - hist:, c0:c1]

    n_b = gdn_S0.shape[0]
    S0_all = gdn_S0.reshape((n_b * B,) + gdn_S0.shape[2:])
    S_all = None
    lru_h, lru_c, gdn_c, tabs16 = [], [], [], []
    for l in range(DEPTH):
        j = l // 2
        if l % 2 == 0:
            u, qmem = norm_matmul(x, w['norm_mix'][l], wc['w_in_a'][j], side_cols=(2 * TM_WIDTH, XA_WIDTH))
            c0 = lru_c0[j]
            if time_major:
                c0k = c0.transpose(1, 0, 2).reshape(1, hist * B, TM_WIDTH)
                h0k = lru_h0[j].reshape(1, B, TM_WIDTH)
            else:
                c0k = c0
                h0k = lru_h0[j].reshape(B, 1, TM_WIDTH)
            tm, h = rglru(u, c0k, h0k, w['lru_conv_w'][j], w['lru_conv_b'][j], wc['lru_wr'][j], w['lru_br'][j],
                          wc['lru_wi'][j], w['lru_bi'][j], w['lru_lambda'][j], G=G, L=L, R=R, tl=tl)
            lru_h.append(h.reshape(B, TM_WIDTH))
            lru_c.append(last_rows(u, TM_WIDTH, 2 * TM_WIDTH))
        else:
            u, qmem = norm_matmul(x, w['norm_mix'][l], wc['w_in_b'][j], side_cols=(GDN_QKV + TM_WIDTH, XA_WIDTH))
            C = min(GDN_CHUNK, Lp)
            nb = GDN_SEQS_PER_STEP if (Lp == C and B % GDN_SEQS_PER_STEP == 0) else 1
            u_bm = to_batch_major(u) if time_major else u
            o_bm, S_all = gdn(u_bm, gdn_c0[j], S0_all, S_all, w['gdn_conv_w'][j], w['gdn_conv_b'][j],
                              w['gdn_a_log'][j], w['gdn_dt_bias'][j], w['gdn_o_norm'][j], layer=j, B=B, L=Lp, C=C,
                              n_valid=min(L, C), nb=nb, qkv_blk=0, z_blk=GDN_QKV // TM_WIDTH,
                              ba_blk=(GDN_QKV + TM_WIDTH + XA_WIDTH) // LANES)
            tm = to_time_major(o_bm) if time_major else o_bm
            gdn_c.append(last_rows(u, 0, GDN_QKV))
        if time_major:
            q_bm = to_batch_major(qmem)
            xa = to_time_major(xattn_cache(q_bm, mem[0], mem[1], layer=l, B=B, tq=Lp,
                                           nb=XATTN_SEQS_PER_STEP if B % XATTN_SEQS_PER_STEP == 0 else 1))
        else:
            xa = xattn(qmem, mem[l], mem[l], B=B, Lq=L, tq=min(L, 512), q_blk=0, k_blk=0, v_blk=1)
        x = out_proj(tm, xa, wc['w_out_tm'][l], wc['w_out_xa'][l], x)
        q = norm_matmul(x, w['norm_ffn'][l], wc['peer_wq'][l])
        pw, aux = peer_topk(q, w['peer_subkeys'][l])
        if peer_tabs16 is None:
            x, u16, v16 = peer_dense(x, w['norm_ffn'][l], pw, aux, wc['peer_u'], wc['peer_v'], layer=l,
                                     emit_tables=True)
            tabs16.append((u16, v16))
        else:
            x = peer_dense(x, w['norm_ffn'][l], pw, aux, peer_tabs16[l][0], peer_tabs16[l][1])
    y = rmsnorm_rows(x, w['norm_final'])
    return (y, jnp.stack(lru_h), jnp.stack(lru_c), S_all.reshape(gdn_S0.shape), jnp.stack(gdn_c)), tabs16


def kernel(x_prompt, x_sample, state_rglru_h, state_rglru_conv, state_gdn_S, state_gdn_conv, cache_mem_k, cache_mem_v, mem_prompt, norm_mix, norm_ffn, norm_final, w_in_a, w_in_b, w_out, lru_conv_w, lru_conv_b, lru_wr, lru_br, lru_wi, lru_bi, lru_lambda, gdn_conv_w, gdn_conv_b, gdn_a_log, gdn_dt_bias, gdn_o_norm, norm_mem, w_mem_kv, peer_wq, peer_subkeys, peer_u, peer_v):
    w = dict(norm_mix=norm_mix, norm_ffn=norm_ffn, norm_final=norm_final, lru_conv_w=lru_conv_w,
             lru_conv_b=lru_conv_b, lru_br=lru_br, lru_bi=lru_bi, lru_lambda=lru_lambda, gdn_conv_w=gdn_conv_w,
             gdn_conv_b=gdn_conv_b, gdn_a_log=gdn_a_log, gdn_dt_bias=gdn_dt_bias, gdn_o_norm=gdn_o_norm,
             peer_subkeys=peer_subkeys)
    wc = _prep_weights(w_in_a, w_in_b, w_out, w_mem_kv, peer_wq, peer_u, peer_v, lru_wr, lru_wi)
    Bp, Lp_, D = x_prompt.shape
    Bs, Ls, _ = x_sample.shape
    n_a, n_b = state_rglru_h.shape[0], state_gdn_S.shape[0]

    xs = x_sample.transpose(1, 0, 2).reshape(Ls * Bs, D)
    rows_hd = (DEPTH * Bs, N_MEM * XA_HEADS, HEAD_DIM)
    (y_s, s_h, s_ca, s_S, s_cb), tabs16 = _trunk(
        xs, (cache_mem_k.reshape(rows_hd), cache_mem_v.reshape(rows_hd)), state_rglru_h, state_rglru_conv,
        state_gdn_S, state_gdn_conv, w, wc, B=Bs, L=Ls, time_major=True)
    y_s = y_s.reshape(Ls, Bs, D).transpose(1, 0, 2)

    mem2 = mem_prompt.reshape(Bp * N_MEM, D)
    kv = [norm_matmul(mem2, norm_mem[l], wc['w_mem_kv'][l]) for l in range(DEPTH)]
    (y_p, p_h, p_ca, p_S, p_cb), _ = _trunk(
        x_prompt.reshape(Bp * Lp_, D), kv,
        jnp.zeros((n_a, Bp, TM_WIDTH), F32), jnp.zeros((n_a, Bp, CONV_W - 1, TM_WIDTH), F32),
        jnp.zeros((n_b, Bp, GDN_HEADS, HEAD_DIM, HEAD_DIM), F32), jnp.zeros((n_b, Bp, CONV_W - 1, GDN_QKV), F32),
        w, wc, B=Bp, L=Lp_, time_major=False, peer_tabs16=tabs16)
    p_mem_k = jnp.stack([a[:, :XA_WIDTH].reshape(Bp, N_MEM, XA_HEADS, HEAD_DIM) for a in kv])
    p_mem_v = jnp.stack([a[:, XA_WIDTH:].reshape(Bp, N_MEM, XA_HEADS, HEAD_DIM) for a in kv])
    return (y_p.reshape(Bp, Lp_, D), y_s, p_h, p_ca, p_S, p_cb, p_mem_k, p_mem_v, s_h, s_ca, s_S, s_cb)
```

```python
import functools
import math

import jax
import jax.numpy as jnp
from jax import lax
from jax.experimental import pallas as pl
from jax.experimental.pallas import tpu as pltpu

F32 = jnp.float32
BF16 = jnp.bfloat16

D_MODEL = 2048
DEPTH = 4
HEAD_DIM = 128
XA_HEADS = 4
XA_WIDTH = XA_HEADS * HEAD_DIM
TM_WIDTH = D_MODEL - XA_WIDTH
N_MEM = 256
CONV_W = 4
EPS = 1e-6
LRU_BLOCKS = TM_WIDTH // HEAD_DIM
LRU_C = 8.0
GDN_HEADS = TM_WIDTH // HEAD_DIM
GDN_QKV = 3 * TM_WIDTH
GDN_CHUNK = 64
GDN_SEQS_PER_STEP = 4
XATTN_SEQS_PER_STEP = 4
PEER_HEADS = 8
PEER_NKEYS = 128
PEER_TOPK = 16
B_IN_PAD = 7168

SUBLANES = 8
LANES = 128
VMEM_LIMIT = 52 * 1024 * 1024

NEG_INF = float("-inf")


def _cparams(sem):
    return pltpu.CompilerParams(dimension_semantics=sem, vmem_limit_bytes=VMEM_LIMIT)


def _dot(a, b, precision=None):
    return jnp.dot(a, b, preferred_element_type=F32, precision=precision)


def _mm(a, b):
    return jnp.dot(a.astype(BF16), b.astype(BF16), preferred_element_type=F32)


def _split_bf16(a):
    hi = a.astype(BF16)
    return hi, (a - hi.astype(F32)).astype(BF16)


def _dot3(a, b):
    ah, al = _split_bf16(a)
    bh, bl = _split_bf16(b)
    return _dot(ah, bh) + (_dot(ah, bl) + _dot(al, bh))


def _dot_nt(a, b, precision=None):
    return lax.dot_general(a, b, (((1,), (1,)), ((), ())), preferred_element_type=F32, precision=precision)


def _dot_tn(a, b, precision=None):
    return lax.dot_general(a, b, (((0,), (0,)), ((), ())), preferred_element_type=F32, precision=precision)


def _sigmoid(x):
    return 1.0 / (1.0 + jnp.exp(-x))


def _silu(x):
    return x * _sigmoid(x)


def _gelu_tanh(x):
    c = math.sqrt(2.0 / math.pi)
    return 0.5 * x * (1.0 + jnp.tanh(c * (x + 0.044715 * (x * x * x))))


def _gelu_tanh_x2(x):
    c = math.sqrt(2.0 / math.pi)
    return x * (1.0 + jnp.tanh(x * (c + (0.044715 * c) * (x * x))))


def _softplus(x):
    return jnp.maximum(x, 0.0) + jnp.log(1.0 + jnp.exp(-jnp.abs(x)))


def _rms(x, g):
    ms = jnp.mean(x * x, axis=-1, keepdims=True)
    return x * lax.rsqrt(ms + EPS) * g


def _norm_matmul_kernel(x_ref, g_ref, w_ref, o_ref, *rest, side):
    xn_ref = rest[-1]

    @pl.when(pl.program_id(1) == 0)
    def _():
        xn_ref[...] = _rms(x_ref[...], g_ref[...]).astype(BF16)

    acc = _dot(xn_ref[...], w_ref[...])
    o_ref[...] = acc
    if side is not None:
        tile, off, width = side

        @pl.when(pl.program_id(1) == tile)
        def _():
            rest[0][...] = acc[:, off:off + width]


def _matmul_tiles(m, n, tm, tn):
    tm = min(tm, m)
    while m % tm:
        tm //= 2
    while n % tn:
        tn //= 2
    return tm, tn


def norm_matmul(x, g, w, *, side_cols=None, tm=1024, tn=1024):
    m, k = x.shape
    n = w.shape[1]
    tm, tn = _matmul_tiles(m, n, tm, tn)
    assert tm % SUBLANES == 0 and tn % LANES == 0
    out_specs = [pl.BlockSpec((tm, tn), lambda i, j: (i, j))]
    out_shape = [jax.ShapeDtypeStruct((m, n), F32)]
    side = None
    if side_cols is not None:
        start, width = side_cols
        side = (start // tn, start % tn, width)
        assert start % tn + width <= tn
        out_specs.append(pl.BlockSpec((tm, width), lambda i, j: (i, 0)))
        out_shape.append(jax.ShapeDtypeStruct((m, width), F32))
    outs = pl.pallas_call(
        functools.partial(_norm_matmul_kernel, side=side),
        grid=(m // tm, n // tn),
        in_specs=[
            pl.BlockSpec((tm, k), lambda i, j: (i, 0)),
            pl.BlockSpec((1, k), lambda i, j: (0, 0)),
            pl.BlockSpec((k, tn), lambda i, j: (0, j)),
        ],
        out_specs=out_specs,
        out_shape=out_shape,
        scratch_shapes=[pltpu.VMEM((tm, k), BF16)],
        compiler_params=_cparams(("parallel", "arbitrary")),
        name="norm_matmul",
    )(x, g.reshape(1, k), w)
    return outs[0] if side_cols is None else tuple(outs)


def _out_proj_kernel(a_ref, b_ref, wa_ref, wb_ref, r_ref, o_ref):
    acc = _dot(a_ref[...], wa_ref[...]) + _dot(b_ref[...], wb_ref[...])
    o_ref[...] = r_ref[...] + acc


def out_proj(a, b, wa, wb, r, *, tm=1024, tn=1024):
    m, ka = a.shape
    kb = b.shape[1]
    n = wa.shape[1]
    tm, tn = _matmul_tiles(m, n, tm, tn)
    assert tm % SUBLANES == 0 and tn % LANES == 0
    return pl.pallas_call(
        _out_proj_kernel,
        grid=(m // tm, n // tn),
        in_specs=[
            pl.BlockSpec((tm, ka), lambda i, j: (i, 0)),
            pl.BlockSpec((tm, kb), lambda i, j: (i, 0)),
            pl.BlockSpec((ka, tn), lambda i, j: (0, j)),
            pl.BlockSpec((kb, tn), lambda i, j: (0, j)),
            pl.BlockSpec((tm, tn), lambda i, j: (i, j)),
        ],
        out_specs=pl.BlockSpec((tm, tn), lambda i, j: (i, j)),
        out_shape=jax.ShapeDtypeStruct((m, n), F32),
        compiler_params=_cparams(("parallel", "arbitrary")),
        name="out_proj",
    )(a, b, wa, wb, r)


def _rmsnorm_kernel(x_ref, g_ref, o_ref):
    o_ref[...] = _rms(x_ref[...], g_ref[...])


def rmsnorm_rows(x, g, *, tm=512):
    m, k = x.shape
    tm = min(tm, m)
    assert m % tm == 0
    return pl.pallas_call(
        _rmsnorm_kernel,
        grid=(m // tm,),
        in_specs=[pl.BlockSpec((tm, k), lambda i: (i, 0)), pl.BlockSpec((1, k), lambda i: (0, 0))],
        out_specs=pl.BlockSpec((tm, k), lambda i: (i, 0)),
        out_shape=jax.ShapeDtypeStruct((m, k), F32),
        compiler_params=_cparams(("parallel",)),
        name="final_rmsnorm",
    )(x, g.reshape(1, k))


def _rglru_kernel(gate_ref, xr_ref, cw_ref, cb_ref, c0_ref, h0_ref, wr_ref, br_ref, wi_ref, bi_ref, lam_ref,
                  y_ref, hout_ref, hist_ref, a_ref, b_ref, hs_ref, h_ref, *, tl, R):
    n = pl.program_id(1)
    rows = tl * R
    hist_rows = (CONV_W - 1) * R
    off0 = -(-hist_rows // SUBLANES) * SUBLANES

    @pl.when(n == 0)
    def _():
        hist_ref[off0 - hist_rows:off0, :] = c0_ref[0]
        h_ref[...] = h0_ref[0]

    x = xr_ref[...]
    hist_ref[off0:off0 + rows, :] = x
    xc = cb_ref[...]
    for j in range(CONV_W):
        s = off0 - hist_rows + j * R
        xc = xc + hist_ref[s:s + rows, :] * cw_ref[j:j + 1, :]
    tail = hist_ref[off0 + rows - hist_rows:off0 + rows, :]
    hist_ref[off0 - hist_rows:off0, :] = tail

    sp = _softplus(-lam_ref[...])
    for blk in range(LRU_BLOCKS):
        cs = slice(blk * HEAD_DIM, (blk + 1) * HEAD_DIM)
        xb = xc[:, cs]
        xb16 = xb.astype(BF16)
        r = _sigmoid(_dot(xb16, wr_ref[blk]) + br_ref[blk])
        i = _sigmoid(_dot(xb16, wi_ref[blk]) + bi_ref[blk])
        log_a = (-LRU_C) * r * sp[:, cs]
        a = jnp.exp(log_a)
        a_ref[:, cs] = a
        b_ref[:, cs] = jnp.sqrt(1.0 - a * a) * (i * xb)

    if R % SUBLANES == 0:
        def step(l, carry):
            rs = pl.ds(pl.multiple_of(l * R, SUBLANES), R)
            h = a_ref[rs, :] * h_ref[...] + b_ref[rs, :]
            h_ref[...] = h
            hs_ref[rs, :] = h
            return carry

        lax.fori_loop(0, tl, step, 0)
    else:
        def step(l, h):
            rs = pl.ds(l * R, R)
            h = a_ref[rs, :] * h + b_ref[rs, :]
            hs_ref[rs, :] = h
            return h

        h_ref[...] = lax.fori_loop(0, tl, step, h_ref[...], unroll=SUBLANES)
    y_ref[...] = (hs_ref[...] * _gelu_tanh(gate_ref[...])).astype(y_ref.dtype)
    hout_ref[0] = h_ref[...]


def rglru(u, c0, h0, cw, cb, wr, br, wi, bi, lam, *, G, L, R, tl):
    W = TM_WIDTH
    assert L % tl == 0
    rows = tl * R
    nt = L // tl
    hist_rows = (CONV_W - 1) * R
    off0 = -(-hist_rows // SUBLANES) * SUBLANES
    kern = functools.partial(_rglru_kernel, tl=tl, R=R)
    const2 = lambda g, n: (0, 0)
    const3 = lambda g, n: (0, 0, 0)
    y, hout = pl.pallas_call(
        kern,
        grid=(G, nt),
        in_specs=[
            pl.BlockSpec((rows, W), lambda g, n: (g * nt + n, 0)),
            pl.BlockSpec((rows, W), lambda g, n: (g * nt + n, 1)),
            pl.BlockSpec((CONV_W, W), const2),
            pl.BlockSpec((1, W), const2),
            pl.BlockSpec((1, hist_rows, W), lambda g, n: (g, 0, 0)),
            pl.BlockSpec((1, R, W), lambda g, n: (g, 0, 0)),
            pl.BlockSpec((LRU_BLOCKS, HEAD_DIM, HEAD_DIM), const3),
            pl.BlockSpec((LRU_BLOCKS, 1, HEAD_DIM), const3),
            pl.BlockSpec((LRU_BLOCKS, HEAD_DIM, HEAD_DIM), const3),
            pl.BlockSpec((LRU_BLOCKS, 1, HEAD_DIM), const3),
            pl.BlockSpec((1, W), const2),
        ],
        out_specs=[
            pl.BlockSpec((rows, W), lambda g, n: (g * nt + n, 0)),
            pl.BlockSpec((1, R, W), lambda g, n: (g, 0, 0)),
        ],
        out_shape=[
            jax.ShapeDtypeStruct((G * L * R, W), BF16),
            jax.ShapeDtypeStruct((G, R, W), F32),
        ],
        scratch_shapes=[
            pltpu.VMEM((off0 + rows, W), F32),
            pltpu.VMEM((rows, W), F32),
            pltpu.VMEM((rows, W), F32),
            pltpu.VMEM((rows, W), F32),
            pltpu.VMEM((R, W), F32),
        ],
        compiler_params=_cparams(("parallel", "arbitrary")),
        name="rglru",
    )(u, u, cw, cb.reshape(1, W), c0, h0, wr, br.reshape(LRU_BLOCKS, 1, HEAD_DIM), wi,
      bi.reshape(LRU_BLOCKS, 1, HEAD_DIM), lam.reshape(1, W))
    return y, hout


def _inv_unit_lower_many(As, C, row, col):
    eye = (row == col).astype(F32)
    base = min(SUBLANES, C)
    sh = int(math.log2(base))
    dmask = (row >> sh) == (col >> sh)
    Ps = [jnp.where(dmask, A, 0.0) for A in As]
    Ts = [eye - P for P in Ps]
    span = 2
    while span < base:
        Ps = [_dot3(P, P) for P in Ps]
        Ts = [_dot3(T, eye + P) for T, P in zip(Ts, Ps)]
        span *= 2
    s = base
    while s < C:
        sh = int(math.log2(s))
        off = ((row >> (sh + 1)) == (col >> (sh + 1))) & ((row >> sh) != (col >> sh))
        Ms = [_dot3(jnp.where(off, A, 0.0), T) for A, T in zip(As, Ts)]
        Ts = [T - _dot3(T, M) for T, M in zip(Ts, Ms)]
        s *= 2
    return Ts


def _gdn_kernel(qkv_ref, z_ref, ba_ref, cw_ref, cb_ref, c0_ref, S0_ref, alog_ref, dtb_ref, onorm_ref,
                *rest, C, n_valid, nb):
    o_ref, S_ref, hist_ref = rest[-3:]
    n = pl.program_id(1)
    H, Dh = GDN_HEADS, HEAD_DIM
    hist_rows = CONV_W - 1
    off0 = SUBLANES
    rows = nb * C

    @pl.when(n == 0)
    def _():
        for s in range(nb):
            hist_ref[s, off0 - hist_rows:off0, :] = c0_ref[s]
        S_ref[...] = S0_ref[...]

    ts = []
    for s in range(nb):
        hist_ref[s, off0:off0 + C, :] = qkv_ref[s * C:(s + 1) * C, :]
        xc = cb_ref[...]
        for j in range(CONV_W):
            r0 = off0 - hist_rows + j
            xc = xc + hist_ref[s, r0:r0 + C, :] * cw_ref[j:j + 1, :]
        tail = hist_ref[s, off0 + C - hist_rows:off0 + C, :]
        hist_ref[s, off0 - hist_rows:off0, :] = tail
        ts.append(_silu(xc))

    ba = ba_ref[...]
    beta_all = _sigmoid(ba)
    g_all = -jnp.exp(alog_ref[...]) * _softplus(ba + dtb_ref[...])
    if n_valid < C:
        rmask = (lax.broadcasted_iota(jnp.int32, (rows, LANES), 0) & (C - 1)) < n_valid
        beta_all = jnp.where(rmask, beta_all, 0.0)
        g_all = jnp.where(rmask, g_all, 0.0)
    if rows < LANES:
        g_pad = jnp.concatenate([g_all, jnp.zeros((LANES - rows, LANES), F32)], axis=0)
    else:
        g_pad = g_all
    r128 = lax.broadcasted_iota(jnp.int32, (LANES, LANES), 0)
    c128 = lax.broadcasted_iota(jnp.int32, (LANES, LANES), 1)
    shc = int(math.log2(C))
    tril = ((c128 <= r128) & ((c128 >> shc) == (r128 >> shc))).astype(F32)
    gc_pad = _dot(tril, g_pad, precision=lax.Precision.HIGHEST)
    gcT = gc_pad.T

    row = lax.broadcasted_iota(jnp.int32, (C, C), 0)
    col = lax.broadcasted_iota(jnp.int32, (C, C), 1)
    incl = col <= row
    strict = col < row

    probs = [(s, h) for s in range(nb) for h in range(H)]
    qn, kn, kn16, vh, bcol, gcol, decay, egc = {}, {}, {}, {}, {}, {}, {}, {}
    for p in probs:
        s, h = p
        t = ts[s]
        qh = t[:, h * Dh:(h + 1) * Dh]
        kh = t[:, TM_WIDTH + h * Dh:TM_WIDTH + (h + 1) * Dh]
        vh[p] = t[:, 2 * TM_WIDTH + h * Dh:2 * TM_WIDTH + (h + 1) * Dh]
        qn[p] = qh * lax.rsqrt(jnp.sum(qh * qh, -1, keepdims=True) + EPS) * (Dh ** -0.5)
        kn[p] = kh * lax.rsqrt(jnp.sum(kh * kh, -1, keepdims=True) + EPS)
        kn16[p] = kn[p].astype(BF16)
        bcol[p] = beta_all[s * C:(s + 1) * C, h:h + 1]
        gcol[p] = gc_pad[s * C:(s + 1) * C, H + h:H + h + 1]
        grow = gcT[H + h:H + h + 1, s * C:(s + 1) * C]
        diff = gcol[p] - grow
        decay[p] = jnp.where(incl, jnp.exp(jnp.where(incl, diff, 0.0)), 0.0)
        egc[p] = jnp.exp(gcol[p])
    kk = {p: _dot_nt(kn16[p], kn16[p]) for p in probs}
    qkr = {p: _dot_nt(qn[p].astype(BF16), kn16[p]) for p in probs}
    As = [jnp.where(strict, bcol[p] * kk[p] * decay[p], 0.0) for p in probs]
    Ts = _inv_unit_lower_many(As, C, row, col)
    rhs = [jnp.concatenate([bcol[p] * vh[p], (bcol[p] * egc[p]) * kn[p]], axis=1) for p in probs]
    sol = {p: _dot3(T, r) for p, T, r in zip(probs, Ts, rhs)}
    S = {p: S_ref[p[0], p[1]] for p in probs}
    S16 = {p: S[p].astype(BF16) for p in probs}
    kS = {p: _dot(sol[p][:, Dh:].astype(BF16), S16[p]) for p in probs}
    qS = {p: _dot((qn[p] * egc[p]).astype(BF16), S16[p]) for p in probs}
    u_new = {p: (sol[p][:, :Dh] - kS[p]).astype(BF16) for p in probs}
    qk16 = {p: jnp.where(incl, qkr[p] * decay[p], 0.0).astype(BF16) for p in probs}
    o = {p: qS[p] + _dot(qk16[p], u_new[p]) for p in probs}
    dS = {}
    for p in probs:
        g_last = gcol[p][C - 1:C, :]
        kdec = (kn[p] * jnp.exp(g_last - gcol[p])).astype(BF16)
        dS[p] = _dot_tn(kdec, u_new[p])
    for p in probs:
        s, h = p
        g_last = gcol[p][C - 1:C, :]
        S_ref[s, h] = S[p] * jnp.exp(g_last) + dS[p]
        on = o[p] * lax.rsqrt(jnp.mean(o[p] * o[p], -1, keepdims=True) + EPS) * onorm_ref[...]
        zs = z_ref[s * C:(s + 1) * C, h * Dh:(h + 1) * Dh]
        o_ref[s * C:(s + 1) * C, h * Dh:(h + 1) * Dh] = (on * _silu(zs)).astype(o_ref.dtype)


def gdn(u, c0, S0_all, S_out_prev, cw, cb, a_log, dt_bias, o_norm, *, layer, B, L, C, n_valid, nb, qkv_blk, z_blk,
        ba_blk):
    H, Dh = GDN_HEADS, HEAD_DIM
    nc = L // C
    assert B % nb == 0 and (nb == 1 or nc == 1) and nb * C <= LANES
    rows = nb * C
    soff = layer * (B // nb)
    alog = jnp.zeros((1, LANES), F32).at[0, H:2 * H].set(a_log)
    dtb = jnp.zeros((1, LANES), F32).at[0, H:2 * H].set(dt_bias)
    kern = functools.partial(_gdn_kernel, C=C, n_valid=n_valid, nb=nb)
    const2 = lambda b, n: (0, 0)
    in_specs = [
        pl.BlockSpec((rows, GDN_QKV), lambda b, n: (b * nc + n, qkv_blk)),
        pl.BlockSpec((rows, TM_WIDTH), lambda b, n: (b * nc + n, z_blk)),
        pl.BlockSpec((rows, LANES), lambda b, n: (b * nc + n, ba_blk)),
        pl.BlockSpec((CONV_W, GDN_QKV), const2),
        pl.BlockSpec((1, GDN_QKV), const2),
        pl.BlockSpec((nb, CONV_W - 1, GDN_QKV), lambda b, n: (b, 0, 0)),
        pl.BlockSpec((nb, H, Dh, Dh), lambda b, n: (soff + b, 0, 0, 0)),
        pl.BlockSpec((1, LANES), const2),
        pl.BlockSpec((1, LANES), const2),
        pl.BlockSpec((1, Dh), const2),
    ]
    args = [u, u, u, cw, cb.reshape(1, GDN_QKV), c0, S0_all, alog, dtb, o_norm.reshape(1, Dh)]
    aliases = {}
    if S_out_prev is not None:
        in_specs.append(pl.BlockSpec(memory_space=pl.ANY))
        args.append(S_out_prev)
        aliases = {len(args) - 1: 1}
    o, S = pl.pallas_call(
        kern,
        grid=(B // nb, nc),
        in_specs=in_specs,
        out_specs=[
            pl.BlockSpec((rows, TM_WIDTH), lambda b, n: (b * nc + n, 0)),
            pl.BlockSpec((nb, H, Dh, Dh), lambda b, n: (soff + b, 0, 0, 0)),
        ],
        out_shape=[
            jax.ShapeDtypeStruct((B * L, TM_WIDTH), BF16),
            jax.ShapeDtypeStruct(S0_all.shape, F32),
        ],
        scratch_shapes=[pltpu.VMEM((nb, SUBLANES + C, GDN_QKV), F32)],
        input_output_aliases=aliases,
        compiler_params=_cparams(("parallel", "arbitrary")),
        name="gdn",
    )(*args)
    return o, S


def _softmax_rows(s):
    e = jnp.exp(s - jnp.max(s, axis=-1, keepdims=True))
    return e / jnp.sum(e, axis=-1, keepdims=True)


def _xattn_kernel(q_ref, k_ref, v_ref, o_ref):
    scale = HEAD_DIM ** -0.5
    for h in range(XA_HEADS):
        cs = slice(h * HEAD_DIM, (h + 1) * HEAD_DIM)
        kh = k_ref[:, cs].astype(BF16)
        vh = v_ref[:, cs].astype(BF16)
        p = _softmax_rows(_dot_nt(q_ref[:, cs].astype(BF16), kh) * scale)
        o_ref[:, cs] = _dot(p.astype(BF16), vh).astype(o_ref.dtype)


def _xattn_cache_kernel(q_ref, k_ref, v_ref, o_ref, *, nb):
    scale = HEAD_DIM ** -0.5
    tq = q_ref.shape[0] // nb
    rows = XA_HEADS * tq
    cols = N_MEM * XA_HEADS
    own = ((lax.broadcasted_iota(jnp.int32, (rows, cols), 1) % XA_HEADS)
           == (lax.broadcasted_iota(jnp.int32, (rows, cols), 0) // tq))
    sc = []
    for s in range(nb):
        q = q_ref[s * tq:(s + 1) * tq, :]
        qs = jnp.concatenate([q[:, h * HEAD_DIM:(h + 1) * HEAD_DIM] for h in range(XA_HEADS)], axis=0)
        sc.append(_dot_nt(qs.astype(BF16), k_ref[s].astype(BF16)) * scale)
    pr = [_softmax_rows(jnp.where(own, x, NEG_INF)).astype(BF16) for x in sc]
    for s in range(nb):
        o = _dot(pr[s], v_ref[s].astype(BF16))
        for h in range(XA_HEADS):
            o_ref[s * tq:(s + 1) * tq, h * HEAD_DIM:(h + 1) * HEAD_DIM] = o[h * tq:(h + 1) * tq].astype(o_ref.dtype)


def xattn(q, k, v, *, B, Lq, tq, q_blk=0, k_blk=0, v_blk=0):
    nq = Lq // tq
    return pl.pallas_call(
        _xattn_kernel,
        grid=(B, nq),
        in_specs=[
            pl.BlockSpec((tq, XA_WIDTH), lambda b, n: (b * nq + n, q_blk)),
            pl.BlockSpec((N_MEM, XA_WIDTH), lambda b, n: (b, k_blk)),
            pl.BlockSpec((N_MEM, XA_WIDTH), lambda b, n: (b, v_blk)),
        ],
        out_specs=pl.BlockSpec((tq, XA_WIDTH), lambda b, n: (b * nq + n, 0)),
        out_shape=jax.ShapeDtypeStruct((B * Lq, XA_WIDTH), BF16),
        compiler_params=_cparams(("parallel", "arbitrary")),
        name="xattn",
    )(q, k, v)


def xattn_cache(q, k, v, *, layer, B, tq, nb):
    assert B % nb == 0
    boff = layer * (B // nb)
    spec = pl.BlockSpec((nb, N_MEM * XA_HEADS, HEAD_DIM), lambda b: (boff + b, 0, 0))
    return pl.pallas_call(
        functools.partial(_xattn_cache_kernel, nb=nb),
        grid=(B // nb,),
        in_specs=[pl.BlockSpec((nb * tq, XA_WIDTH), lambda b: (b, 0)), spec, spec],
        out_specs=pl.BlockSpec((nb * tq, XA_WIDTH), lambda b: (b, 0)),
        out_shape=jax.ShapeDtypeStruct((B * tq, XA_WIDTH), BF16),
        compiler_params=_cparams(("parallel",)),
        name="xattn_cache",
    )(q, k, v)


PEER_RANKS = PEER_TOPK + 1
GATE_ROWS = 32


def _cand_counts():
    return [PEER_RANKS // (a + 1) for a in range(PEER_RANKS)]


def _sorting_network(n):
    pairs = []
    t = 1
    while t < n:
        p = t
        while p >= 1:
            for j in range(p % t, n - p, 2 * p):
                for i in range(min(p, n - j - p)):
                    if (i + j) // (2 * t) == (i + j + p) // (2 * t):
                        pairs.append((i + j, i + j + p))
            p //= 2
        t *= 2
    return pairs


def _peer_topk_kernel(q_ref, sk_ref, pw_ref, aux_ref, sv_ref, cand_ref):
    K = PEER_TOPK
    q = q_ref[...]
    s = []
    for p in range(2):
        qp = q[:, p * LANES:(p + 1) * LANES]
        s.append(_dot_nt(sk_ref[0, p], qp, precision=lax.Precision.HIGHEST))
    for p in range(2):
        v = [s[p][g * SUBLANES:(g + 1) * SUBLANES, :] for g in range(PEER_NKEYS // SUBLANES)]
        for (a, b) in _sorting_network(len(v)):
            v[a], v[b] = jnp.maximum(v[a], v[b]), jnp.minimum(v[a], v[b])
        v.append(jnp.full_like(v[0], NEG_INF))
        for r in range(PEER_RANKS):
            m = jnp.max(v[0], axis=0, keepdims=True)
            sv_ref[p, r:r + 1, :] = m
            took = v[0] == m
            for k in range(PEER_RANKS - 1 - r):
                v[k] = jnp.where(took, v[k + 1], v[k])
    sv1 = sv_ref[0, 0:PEER_RANKS, :]
    sv2 = sv_ref[1, 0:PEER_RANKS, :]
    cand_ref[...] = jnp.full(cand_ref.shape, NEG_INF, F32)
    off = 0
    for a, nb in enumerate(_cand_counts()):
        cand_ref[off:off + nb, :] = sv1[a:a + 1, :] + sv2[0:nb, :]
        off += nb
    cur = cand_ref[...]
    tops = []
    for r in range(PEER_RANKS):
        m = jnp.max(cur, axis=0, keepdims=True)
        tops.append(m)
        cur = jnp.where(cur == m, NEG_INF, cur)
    z = jnp.ones_like(tops[0])
    for r in range(1, K):
        z = z + jnp.exp(tops[r] - tops[0])
    thr = 0.5 * (tops[K - 1] + tops[K])
    pw_ref[0, 0] = thr - s[0]
    pw_ref[0, 1] = s[1]
    pw_ref[0, 2] = jnp.exp(s[1] - sv2[0:1, :])
    aux_ref[0] = jnp.broadcast_to(thr - sv1[0:1, :] - jnp.log(z), aux_ref.shape[1:])


def peer_topk(q, subkeys, *, tt=512):
    T = q.shape[0]
    tt = min(tt, T)
    assert T % tt == 0
    H, NK = PEER_HEADS, PEER_NKEYS
    pad8 = lambda n: -(-n // SUBLANES) * SUBLANES
    return pl.pallas_call(
        _peer_topk_kernel,
        grid=(T // tt, H),
        in_specs=[
            pl.BlockSpec((tt, 2 * LANES), lambda i, h: (i, h)),
            pl.BlockSpec((1, 2, NK, LANES), lambda i, h: (h, 0, 0, 0)),
        ],
        out_specs=[
            pl.BlockSpec((1, 3, NK, tt), lambda i, h: (h, 0, 0, i)),
            pl.BlockSpec((1, SUBLANES, tt), lambda i, h: (h, 0, i)),
        ],
        out_shape=[
            jax.ShapeDtypeStruct((H, 3, NK, T), F32),
            jax.ShapeDtypeStruct((H, SUBLANES, T), F32),
        ],
        scratch_shapes=[pltpu.VMEM((2, pad8(PEER_RANKS), tt), F32), pltpu.VMEM((pad8(sum(_cand_counts())), tt), F32)],
        compiler_params=_cparams(("parallel", "arbitrary")),
        name="peer_topk",
    )(q, subkeys)


def _peer_dense_kernel(x_ref, g_ref, pw_ref, aux_ref, u_ref, v_ref, o_ref, *rest, ib, emit_tables):
    xnT_ref, acc_ref, w_ref = rest[-3:]
    e = pl.program_id(1)
    H, NK = PEER_HEADS, PEER_NKEYS
    tt = x_ref.shape[0]
    strip = min(LANES, tt)
    assert tt % strip == 0

    def gates(blk, ii, dst):
        i = blk * ib + ii
        thr1 = [pw_ref[h, 0, pl.ds(i, 1), :] for h in range(H)]
        e1 = [0.5 * jnp.exp(aux_ref[h, 0:1, :] - thr1[h]) for h in range(H)]
        for tc in range(tt // strip):
            ts_ = slice(tc * strip, (tc + 1) * strip)
            for j0 in range(0, NK, GATE_ROWS):
                js = slice(j0, j0 + GATE_ROWS)
                w = None
                for h in range(H):
                    sel = pw_ref[h, 1, js, ts_] >= thr1[h][:, ts_]
                    c = jnp.where(sel, pw_ref[h, 2, js, ts_], 0.0) * e1[h][:, ts_]
                    w = c if w is None else w + c
                dst[ii * NK + j0:ii * NK + j0 + GATE_ROWS, ts_] = w

    @pl.when(e == 0)
    def _():
        xn = _rms(x_ref[...], g_ref[...])
        xnT_ref[...] = xn.T.astype(BF16)
        acc_ref[...] = jnp.zeros_like(acc_ref)

    u16 = u_ref[...].astype(BF16)
    v16 = v_ref[...].astype(BF16)
    if emit_tables:
        rest[0][...] = u16
        rest[1][...] = v16
    hT = _dot(u16, xnT_ref[...])
    for ii in range(ib):
        gates(e, ii, w_ref)
    aT = w_ref[...] * _gelu_tanh_x2(hT)
    acc_ref[...] += _dot(aT.T.astype(BF16), v16)

    @pl.when(e == pl.num_programs(1) - 1)
    def _():
        o_ref[...] = x_ref[...] + acc_ref[...]


def peer_dense(x, g, pw, aux, u_tab, v_tab, *, layer=0, emit_tables=False, tt=512, ib=None):
    T, D = x.shape
    if ib is None:
        ib = 2 if emit_tables else 4
    boff = layer * (PEER_NKEYS // ib)
    tt = min(tt, T)
    assert T % tt == 0 and PEER_NKEYS % ib == 0 and not (emit_tables and T != tt)
    H, NK = PEER_HEADS, PEER_NKEYS
    eb = ib * NK
    out_specs = [pl.BlockSpec((tt, D), lambda i, e: (i, 0))]
    out_shape = [jax.ShapeDtypeStruct((T, D), F32)]
    if emit_tables:
        out_specs += [pl.BlockSpec((eb, D), lambda i, e: (e, 0))] * 2
        out_shape += [jax.ShapeDtypeStruct((NK * NK, D), BF16)] * 2
    outs = pl.pallas_call(
        functools.partial(_peer_dense_kernel, ib=ib, emit_tables=emit_tables),
        grid=(T // tt, NK // ib),
        in_specs=[
            pl.BlockSpec((tt, D), lambda i, e: (i, 0)),
            pl.BlockSpec((1, D), lambda i, e: (0, 0)),
            pl.BlockSpec((H, 3, NK, tt), lambda i, e: (0, 0, 0, i)),
            pl.BlockSpec((H, SUBLANES, tt), lambda i, e: (0, 0, i)),
            pl.BlockSpec((eb, D), lambda i, e: (boff + e, 0)),
            pl.BlockSpec((eb, D), lambda i, e: (boff + e, 0)),
        ],
        out_specs=out_specs,
        out_shape=out_shape,
        scratch_shapes=[
            pltpu.VMEM((D, tt), BF16),
            pltpu.VMEM((tt, D), F32),
            pltpu.VMEM((eb, tt), F32),
        ],
        compiler_params=_cparams(("parallel", "arbitrary")),
        name="peer_dense",
    )(x, g.reshape(1, D), pw, aux, u_tab, v_tab)
    return tuple(outs) if emit_tables else outs[0]


def _prep_weights(w_in_a, w_in_b, w_out, w_mem_kv, peer_wq, peer_u, peer_v, lru_wr, lru_wi):
    H = GDN_HEADS
    qkvz = GDN_QKV + TM_WIDTH
    wb = jnp.concatenate(
        [
            w_in_b[:, :, :qkvz],
            w_in_b[:, :, qkvz + 2 * H:],
            w_in_b[:, :, qkvz:qkvz + 2 * H],
            jnp.zeros(w_in_b.shape[:2] + (B_IN_PAD - w_in_b.shape[2],), w_in_b.dtype),
        ],
        axis=-1,
    )
    return dict(
        w_in_a=w_in_a.astype(BF16),
        w_in_b=wb.astype(BF16),
        w_out_tm=w_out[:, :TM_WIDTH].astype(BF16),
        w_out_xa=w_out[:, TM_WIDTH:].astype(BF16),
        w_mem_kv=w_mem_kv.astype(BF16),
        peer_wq=peer_wq.astype(BF16),
        peer_u=peer_u.reshape(-1, peer_u.shape[-1]),
        peer_v=peer_v.reshape(-1, peer_v.shape[-1]),
        lru_wr=lru_wr.astype(BF16),
        lru_wi=lru_wi.astype(BF16),
    )


def _trunk(x, mem, lru_h0, lru_c0, gdn_S0, gdn_c0, w, wc, *, B, L, time_major, peer_tabs16=None):
    if time_major:
        G, R, tl = 1, B, L
    else:
        G, R, tl = B, 1, min(L, 256)
    Lp = -(-L // SUBLANES) * SUBLANES
    hist = CONV_W - 1

    def to_batch_major(a):
        c = a.shape[-1]
        a = a.reshape(L, B, c).transpose(1, 0, 2)
        return jnp.pad(a, ((0, 0), (0, Lp - L), (0, 0))).reshape(B * Lp, c)

    def to_time_major(a):
        c = a.shape[-1]
        return a.reshape(B, Lp, c)[:, :L].transpose(1, 0, 2).reshape(L * B, c)

    def last_rows(u, c0, c1):
        if time_major:
            return u.reshape(L, B, -1)[L - hist:, :, c0:c1].transpose(1, 0, 2)
        return u.reshape(B, L, -1)[:, L - hist:, c0:c1]

    n_b = gdn_S0.shape[0]
    S0_all = gdn_S0.reshape((n_b * B,) + gdn_S0.shape[2:])
    S_all = None
    lru_h, lru_c, gdn_c, tabs16 = [], [], [], []
    for l in range(DEPTH):
        j = l // 2
        if l % 2 == 0:
            u, qmem = norm_matmul(x, w['norm_mix'][l], wc['w_in_a'][j], side_cols=(2 * TM_WIDTH, XA_WIDTH))
            c0 = lru_c0[j]
            if time_major:
                c0k = c0.transpose(1, 0, 2).reshape(1, hist * B, TM_WIDTH)
                h0k = lru_h0[j].reshape(1, B, TM_WIDTH)
            else:
                c0k = c0
                h0k = lru_h0[j].reshape(B, 1, TM_WIDTH)
            tm, h = rglru(u, c0k, h0k, w['lru_conv_w'][j], w['lru_conv_b'][j], wc['lru_wr'][j], w['lru_br'][j],
                          wc['lru_wi'][j], w['lru_bi'][j], w['lru_lambda'][j], G=G, L=L, R=R, tl=tl)
            lru_h.append(h.reshape(B, TM_WIDTH))
            lru_c.append(last_rows(u, TM_WIDTH, 2 * TM_WIDTH))
        else:
            u, qmem = norm_matmul(x, w['norm_mix'][l], wc['w_in_b'][j], side_cols=(GDN_QKV + TM_WIDTH, XA_WIDTH))
            C = min(GDN_CHUNK, Lp)
            nb = GDN_SEQS_PER_STEP if (Lp == C and B % GDN_SEQS_PER_STEP == 0) else 1
            u_bm = to_batch_major(u) if time_major else u
            o_bm, S_all = gdn(u_bm, gdn_c0[j], S0_all, S_all, w['gdn_conv_w'][j], w['gdn_conv_b'][j],
                              w['gdn_a_log'][j], w['gdn_dt_bias'][j], w['gdn_o_norm'][j], layer=j, B=B, L=Lp, C=C,
                              n_valid=min(L, C), nb=nb, qkv_blk=0, z_blk=GDN_QKV // TM_WIDTH,
                              ba_blk=(GDN_QKV + TM_WIDTH + XA_WIDTH) // LANES)
            tm = to_time_major(o_bm) if time_major else o_bm
            gdn_c.append(last_rows(u, 0, GDN_QKV))
        if time_major:
            q_bm = to_batch_major(qmem)
            xa = to_time_major(xattn_cache(q_bm, mem[0], mem[1], layer=l, B=B, tq=Lp,
                                           nb=XATTN_SEQS_PER_STEP if B % XATTN_SEQS_PER_STEP == 0 else 1))
        else:
            xa = xattn(qmem, mem[l], mem[l], B=B, Lq=L, tq=min(L, 512), q_blk=0, k_blk=0, v_blk=1)
        x = out_proj(tm, xa, wc['w_out_tm'][l], wc['w_out_xa'][l], x)
        q = norm_matmul(x, w['norm_ffn'][l], wc['peer_wq'][l])
        pw, aux = peer_topk(q, w['peer_subkeys'][l])
        if peer_tabs16 is None:
            x, u16, v16 = peer_dense(x, w['norm_ffn'][l], pw, aux, wc['peer_u'], wc['peer_v'], layer=l,
                                     emit_tables=True)
            tabs16.append((u16, v16))
        else:
            x = peer_dense(x, w['norm_ffn'][l], pw, aux, peer_tabs16[l][0], peer_tabs16[l][1])
    y = rmsnorm_rows(x, w['norm_final'])
    return (y, jnp.stack(lru_h), jnp.stack(lru_c), S_all.reshape(gdn_S0.shape), jnp.stack(gdn_c)), tabs16


def kernel(x_prompt, x_sample, state_rglru_h, state_rglru_conv, state_gdn_S, state_gdn_conv, cache_mem_k, cache_mem_v, mem_prompt, norm_mix, norm_ffn, norm_final, w_in_a, w_in_b, w_out, lru_conv_w, lru_conv_b, lru_wr, lru_br, lru_wi, lru_bi, lru_lambda, gdn_conv_w, gdn_conv_b, gdn_a_log, gdn_dt_bias, gdn_o_norm, norm_mem, w_mem_kv, peer_wq, peer_subkeys, peer_u, peer_v):
    w = dict(norm_mix=norm_mix, norm_ffn=norm_ffn, norm_final=norm_final, lru_conv_w=lru_conv_w,
             lru_conv_b=lru_conv_b, lru_br=lru_br, lru_bi=lru_bi, lru_lambda=lru_lambda, gdn_conv_w=gdn_conv_w,
             gdn_conv_b=gdn_conv_b, gdn_a_log=gdn_a_log, gdn_dt_bias=gdn_dt_bias, gdn_o_norm=gdn_o_norm,
             peer_subkeys=peer_subkeys)
    wc = _prep_weights(w_in_a, w_in_b, w_out, w_mem_kv, peer_wq, peer_u, peer_v, lru_wr, lru_wi)
    Bp, Lp_, D = x_prompt.shape
    Bs, Ls, _ = x_sample.shape
    n_a, n_b = state_rglru_h.shape[0], state_gdn_S.shape[0]

    xs = x_sample.transpose(1, 0, 2).reshape(Ls * Bs, D)
    rows_hd = (DEPTH * Bs, N_MEM * XA_HEADS, HEAD_DIM)
    (y_s, s_h, s_ca, s_S, s_cb), tabs16 = _trunk(
        xs, (cache_mem_k.reshape(rows_hd), cache_mem_v.reshape(rows_hd)), state_rglru_h, state_rglru_conv,
        state_gdn_S, state_gdn_conv, w, wc, B=Bs, L=Ls, time_major=True)
    y_s = y_s.reshape(Ls, Bs, D).transpose(1, 0, 2)

    mem2 = mem_prompt.reshape(Bp * N_MEM, D)
    kv = [norm_matmul(mem2, norm_mem[l], wc['w_mem_kv'][l]) for l in range(DEPTH)]
    (y_p, p_h, p_ca, p_S, p_cb), _ = _trunk(
        x_prompt.reshape(Bp * Lp_, D), kv,
        jnp.zeros((n_a, Bp, TM_WIDTH), F32), jnp.zeros((n_a, Bp, CONV_W - 1, TM_WIDTH), F32),
        jnp.zeros((n_b, Bp, GDN_HEADS, HEAD_DIM, HEAD_DIM), F32), jnp.zeros((n_b, Bp, CONV_W - 1, GDN_QKV), F32),
        w, wc, B=Bp, L=Lp_, time_major=False, peer_tabs16=tabs16)
    p_mem_k = jnp.stack([a[:, :XA_WIDTH].reshape(Bp, N_MEM, XA_HEADS, HEAD_DIM) for a in kv])
    p_mem_v = jnp.stack([a[:, XA_WIDTH:].reshape(Bp, N_MEM, XA_HEADS, HEAD_DIM) for a in kv])
    return (y_p.reshape(Bp, Lp_, D), y_s, p_h, p_ca, p_S, p_cb, p_mem_k, p_mem_v, s_h, s_ca, s_S, s_cb)
```

```python
import functools
import math

import jax
import jax.numpy as jnp
from jax import lax
from jax.experimental import pallas as pl
from jax.experimental.pallas import tpu as pltpu

F32 = jnp.float32
BF16 = jnp.bfloat16

D_MODEL = 2048
DEPTH = 4
HEAD_DIM = 128
XA_HEADS = 4
XA_WIDTH = XA_HEADS * HEAD_DIM
TM_WIDTH = D_MODEL - XA_WIDTH
N_MEM = 256
CONV_W = 4
EPS = 1e-6
LRU_BLOCKS = TM_WIDTH // HEAD_DIM
LRU_C = 8.0
GDN_HEADS = TM_WIDTH // HEAD_DIM
GDN_QKV = 3 * TM_WIDTH
GDN_CHUNK = 64
GDN_SEQS_PER_STEP = 4
XATTN_SEQS_PER_STEP = 4
PEER_HEADS = 8
PEER_NKEYS = 128
PEER_TOPK = 16
B_IN_PAD = 7168

SUBLANES = 8
LANES = 128
VMEM_LIMIT = 52 * 1024 * 1024

NEG_INF = float("-inf")


def _cparams(sem):
    return pltpu.CompilerParams(dimension_semantics=sem, vmem_limit_bytes=VMEM_LIMIT)


def _dot(a, b, precision=None):
    return jnp.dot(a, b, preferred_element_type=F32, precision=precision)


def _mm(a, b):
    return jnp.dot(a.astype(BF16), b.astype(BF16), preferred_element_type=F32)


def _split_bf16(a):
    hi = a.astype(BF16)
    return hi, (a - hi.astype(F32)).astype(BF16)


def _dot3(a, b):
    ah, al = _split_bf16(a)
    bh, bl = _split_bf16(b)
    return _dot(ah, bh) + (_dot(ah, bl) + _dot(al, bh))


def _dot_nt(a, b, precision=None):
    return lax.dot_general(a, b, (((1,), (1,)), ((), ())), preferred_element_type=F32, precision=precision)


def _dot_tn(a, b, precision=None):
    return lax.dot_general(a, b, (((0,), (0,)), ((), ())), preferred_element_type=F32, precision=precision)


def _sigmoid(x):
    return 1.0 / (1.0 + jnp.exp(-x))


def _silu(x):
    return x * _sigmoid(x)


def _gelu_tanh(x):
    c = math.sqrt(2.0 / math.pi)
    return 0.5 * x * (1.0 + jnp.tanh(c * (x + 0.044715 * (x * x * x))))


def _gelu_tanh_x2(x):
    c = math.sqrt(2.0 / math.pi)
    return x * (1.0 + jnp.tanh(x * (c + (0.044715 * c) * (x * x))))


def _softplus(x):
    return jnp.maximum(x, 0.0) + jnp.log(1.0 + jnp.exp(-jnp.abs(x)))


def _rms(x, g):
    ms = jnp.mean(x * x, axis=-1, keepdims=True)
    return x * lax.rsqrt(ms + EPS) * g


def _norm_matmul_kernel(x_ref, g_ref, w_ref, o_ref, *rest, side):
    xn_ref = rest[-1]

    @pl.when(pl.program_id(1) == 0)
    def _():
        xn_ref[...] = _rms(x_ref[...], g_ref[...]).astype(BF16)

    acc = _dot(xn_ref[...], w_ref[...])
    o_ref[...] = acc
    if side is not None:
        tile, off, width = side

        @pl.when(pl.program_id(1) == tile)
        def _():
            rest[0][...] = acc[:, off:off + width]


def _matmul_tiles(m, n, tm, tn):
    tm = min(tm, m)
    while m % tm:
        tm //= 2
    while n % tn:
        tn //= 2
    return tm, tn


def norm_matmul(x, g, w, *, side_cols=None, tm=1024, tn=1024):
    m, k = x.shape
    n = w.shape[1]
    tm, tn = _matmul_tiles(m, n, tm, tn)
    assert tm % SUBLANES == 0 and tn % LANES == 0
    out_specs = [pl.BlockSpec((tm, tn), lambda i, j: (i, j))]
    out_shape = [jax.ShapeDtypeStruct((m, n), F32)]
    side = None
    if side_cols is not None:
        start, width = side_cols
        side = (start // tn, start % tn, width)
        assert start % tn + width <= tn
        out_specs.append(pl.BlockSpec((tm, width), lambda i, j: (i, 0)))
        out_shape.append(jax.ShapeDtypeStruct((m, width), F32))
    outs = pl.pallas_call(
        functools.partial(_norm_matmul_kernel, side=side),
        grid=(m // tm, n // tn),
        in_specs=[
            pl.BlockSpec((tm, k), lambda i, j: (i, 0)),
            pl.BlockSpec((1, k), lambda i, j: (0, 0)),
            pl.BlockSpec((k, tn), lambda i, j: (0, j)),
        ],
        out_specs=out_specs,
        out_shape=out_shape,
        scratch_shapes=[pltpu.VMEM((tm, k), BF16)],
        compiler_params=_cparams(("parallel", "arbitrary")),
        name="norm_matmul",
    )(x, g.reshape(1, k), w)
    return outs[0] if side_cols is None else tuple(outs)


def _out_proj_kernel(a_ref, b_ref, wa_ref, wb_ref, r_ref, o_ref):
    acc = _dot(a_ref[...], wa_ref[...]) + _dot(b_ref[...], wb_ref[...])
    o_ref[...] = r_ref[...] + acc


def out_proj(a, b, wa, wb, r, *, tm=1024, tn=1024):
    m, ka = a.shape
    kb = b.shape[1]
    n = wa.shape[1]
    tm, tn = _matmul_tiles(m, n, tm, tn)
    assert tm % SUBLANES == 0 and tn % LANES == 0
    return pl.pallas_call(
        _out_proj_kernel,
        grid=(m // tm, n // tn),
        in_specs=[
            pl.BlockSpec((tm, ka), lambda i, j: (i, 0)),
            pl.BlockSpec((tm, kb), lambda i, j: (i, 0)),
            pl.BlockSpec((ka, tn), lambda i, j: (0, j)),
            pl.BlockSpec((kb, tn), lambda i, j: (0, j)),
            pl.BlockSpec((tm, tn), lambda i, j: (i, j)),
        ],
        out_specs=pl.BlockSpec((tm, tn), lambda i, j: (i, j)),
        out_shape=jax.ShapeDtypeStruct((m, n), F32),
        compiler_params=_cparams(("parallel", "arbitrary")),
        name="out_proj",
    )(a, b, wa, wb, r)


def _rmsnorm_kernel(x_ref, g_ref, o_ref):
    o_ref[...] = _rms(x_ref[...], g_ref[...])


def rmsnorm_rows(x, g, *, tm=512):
    m, k = x.shape
    tm = min(tm, m)
    assert m % tm == 0
    return pl.pallas_call(
        _rmsnorm_kernel,
        grid=(m // tm,),
        in_specs=[pl.BlockSpec((tm, k), lambda i: (i, 0)), pl.BlockSpec((1, k), lambda i: (0, 0))],
        out_specs=pl.BlockSpec((tm, k), lambda i: (i, 0)),
        out_shape=jax.ShapeDtypeStruct((m, k), F32),
        compiler_params=_cparams(("parallel",)),
        name="final_rmsnorm",
    )(x, g.reshape(1, k))


def _rglru_kernel(gate_ref, xr_ref, cw_ref, cb_ref, c0_ref, h0_ref, wr_ref, br_ref, wi_ref, bi_ref, lam_ref,
                  y_ref, hout_ref, hist_ref, a_ref, b_ref, hs_ref, h_ref, *, tl, R):
    n = pl.program_id(1)
    rows = tl * R
    hist_rows = (CONV_W - 1) * R
    off0 = -(-hist_rows // SUBLANES) * SUBLANES

    @pl.when(n == 0)
    def _():
        hist_ref[off0 - hist_rows:off0, :] = c0_ref[0]
        h_ref[...] = h0_ref[0]

    x = xr_ref[...]
    hist_ref[off0:off0 + rows, :] = x
    xc = cb_ref[...]
    for j in range(CONV_W):
        s = off0 - hist_rows + j * R
        xc = xc + hist_ref[s:s + rows, :] * cw_ref[j:j + 1, :]
    tail = hist_ref[off0 + rows - hist_rows:off0 + rows, :]
    hist_ref[off0 - hist_rows:off0, :] = tail

    sp = _softplus(-lam_ref[...])
    for blk in range(LRU_BLOCKS):
        cs = slice(blk * HEAD_DIM, (blk + 1) * HEAD_DIM)
        xb = xc[:, cs]
        xb16 = xb.astype(BF16)
        r = _sigmoid(_dot(xb16, wr_ref[blk]) + br_ref[blk])
        i = _sigmoid(_dot(xb16, wi_ref[blk]) + bi_ref[blk])
        log_a = (-LRU_C) * r * sp[:, cs]
        a = jnp.exp(log_a)
        a_ref[:, cs] = a
        b_ref[:, cs] = jnp.sqrt(1.0 - a * a) * (i * xb)

    if R % SUBLANES == 0:
        def step(l, carry):
            rs = pl.ds(pl.multiple_of(l * R, SUBLANES), R)
            h = a_ref[rs, :] * h_ref[...] + b_ref[rs, :]
            h_ref[...] = h
            hs_ref[rs, :] = h
            return carry

        lax.fori_loop(0, tl, step, 0)
    else:
        def step(l, h):
            rs = pl.ds(l * R, R)
            h = a_ref[rs, :] * h + b_ref[rs, :]
            hs_ref[rs, :] = h
            return h

        h_ref[...] = lax.fori_loop(0, tl, step, h_ref[...], unroll=SUBLANES)
    y_ref[...] = (hs_ref[...] * _gelu_tanh(gate_ref[...])).astype(y_ref.dtype)
    hout_ref[0] = h_ref[...]


def rglru(u, c0, h0, cw, cb, wr, br, wi, bi, lam, *, G, L, R, tl):
    W = TM_WIDTH
    assert L % tl == 0
    rows = tl * R
    nt = L // tl
    hist_rows = (CONV_W - 1) * R
    off0 = -(-hist_rows // SUBLANES) * SUBLANES
    kern = functools.partial(_rglru_kernel, tl=tl, R=R)
    const2 = lambda g, n: (0, 0)
    const3 = lambda g, n: (0, 0, 0)
    y, hout = pl.pallas_call(
        kern,
        grid=(G, nt),
        in_specs=[
            pl.BlockSpec((rows, W), lambda g, n: (g * nt + n, 0)),
            pl.BlockSpec((rows, W), lambda g, n: (g * nt + n, 1)),
            pl.BlockSpec((CONV_W, W), const2),
            pl.BlockSpec((1, W), const2),
            pl.BlockSpec((1, hist_rows, W), lambda g, n: (g, 0, 0)),
            pl.BlockSpec((1, R, W), lambda g, n: (g, 0, 0)),
            pl.BlockSpec((LRU_BLOCKS, HEAD_DIM, HEAD_DIM), const3),
            pl.BlockSpec((LRU_BLOCKS, 1, HEAD_DIM), const3),
            pl.BlockSpec((LRU_BLOCKS, HEAD_DIM, HEAD_DIM), const3),
            pl.BlockSpec((LRU_BLOCKS, 1, HEAD_DIM), const3),
            pl.BlockSpec((1, W), const2),
        ],
        out_specs=[
            pl.BlockSpec((rows, W), lambda g, n: (g * nt + n, 0)),
            pl.BlockSpec((1, R, W), lambda g, n: (g, 0, 0)),
        ],
        out_shape=[
            jax.ShapeDtypeStruct((G * L * R, W), BF16),
            jax.ShapeDtypeStruct((G, R, W), F32),
        ],
        scratch_shapes=[
            pltpu.VMEM((off0 + rows, W), F32),
            pltpu.VMEM((rows, W), F32),
            pltpu.VMEM((rows, W), F32),
            pltpu.VMEM((rows, W), F32),
            pltpu.VMEM((R, W), F32),
        ],
        compiler_params=_cparams(("parallel", "arbitrary")),
        name="rglru",
    )(u, u, cw, cb.reshape(1, W), c0, h0, wr, br.reshape(LRU_BLOCKS, 1, HEAD_DIM), wi,
      bi.reshape(LRU_BLOCKS, 1, HEAD_DIM), lam.reshape(1, W))
    return y, hout


def _inv_unit_lower_many(As, C, row, col):
    eye = (row == col).astype(F32)
    base = min(SUBLANES, C)
    sh = int(math.log2(base))
    dmask = (row >> sh) == (col >> sh)
    Ps = [jnp.where(dmask, A, 0.0) for A in As]
    Ts = [eye - P for P in Ps]
    span = 2
    while span < base:
        Ps = [_dot3(P, P) for P in Ps]
        Ts = [_dot3(T, eye + P) for T, P in zip(Ts, Ps)]
        span *= 2
    s = base
    while s < C:
        sh = int(math.log2(s))
        off = ((row >> (sh + 1)) == (col >> (sh + 1))) & ((row >> sh) != (col >> sh))
        Ms = [_dot3(jnp.where(off, A, 0.0), T) for A, T in zip(As, Ts)]
        Ts = [T - _dot3(T, M) for T, M in zip(Ts, Ms)]
        s *= 2
    return Ts


def _gdn_kernel(qkv_ref, z_ref, ba_ref, cw_ref, cb_ref, c0_ref, S0_ref, alog_ref, dtb_ref, onorm_ref,
                *rest, C, n_valid, nb):
    o_ref, S_ref, hist_ref = rest[-3:]
    n = pl.program_id(1)
    H, Dh = GDN_HEADS, HEAD_DIM
    hist_rows = CONV_W - 1
    off0 = SUBLANES
    rows = nb * C

    @pl.when(n == 0)
    def _():
        for s in range(nb):
            hist_ref[s, off0 - hist_rows:off0, :] = c0_ref[s]
        S_ref[...] = S0_ref[...]

    ts = []
    for s in range(nb):
        hist_ref[s, off0:off0 + C, :] = qkv_ref[s * C:(s + 1) * C, :]
        xc = cb_ref[...]
        for j in range(CONV_W):
            r0 = off0 - hist_rows + j
            xc = xc + hist_ref[s, r0:r0 + C, :] * cw_ref[j:j + 1, :]
        tail = hist_ref[s, off0 + C - hist_rows:off0 + C, :]
        hist_ref[s, off0 - hist_rows:off0, :] = tail
        ts.append(_silu(xc))

    ba = ba_ref[...]
    beta_all = _sigmoid(ba)
    g_all = -jnp.exp(alog_ref[...]) * _softplus(ba + dtb_ref[...])
    if n_valid < C:
        rmask = (lax.broadcasted_iota(jnp.int32, (rows, LANES), 0) & (C - 1)) < n_valid
        beta_all = jnp.where(rmask, beta_all, 0.0)
        g_all = jnp.where(rmask, g_all, 0.0)
    if rows < LANES:
        g_pad = jnp.concatenate([g_all, jnp.zeros((LANES - rows, LANES), F32)], axis=0)
    else:
        g_pad = g_all
    r128 = lax.broadcasted_iota(jnp.int32, (LANES, LANES), 0)
    c128 = lax.broadcasted_iota(jnp.int32, (LANES, LANES), 1)
    shc = int(math.log2(C))
    tril = ((c128 <= r128) & ((c128 >> shc) == (r128 >> shc))).astype(F32)
    gc_pad = _dot(tril, g_pad, precision=lax.Precision.HIGHEST)
    gcT = gc_pad.T

    row = lax.broadcasted_iota(jnp.int32, (C, C), 0)
    col = lax.broadcasted_iota(jnp.int32, (C, C), 1)
    incl = col <= row
    strict = col < row

    probs = [(s, h) for s in range(nb) for h in range(H)]
    qn, kn, kn16, vh, bcol, gcol, decay, egc = {}, {}, {}, {}, {}, {}, {}, {}
    for p in probs:
        s, h = p
        t = ts[s]
        qh = t[:, h * Dh:(h + 1) * Dh]
        kh = t[:, TM_WIDTH + h * Dh:TM_WIDTH + (h + 1) * Dh]
        vh[p] = t[:, 2 * TM_WIDTH + h * Dh:2 * TM_WIDTH + (h + 1) * Dh]
        qn[p] = qh * lax.rsqrt(jnp.sum(qh * qh, -1, keepdims=True) + EPS) * (Dh ** -0.5)
        kn[p] = kh * lax.rsqrt(jnp.sum(kh * kh, -1, keepdims=True) + EPS)
        kn16[p] = kn[p].astype(BF16)
        bcol[p] = beta_all[s * C:(s + 1) * C, h:h + 1]
        gcol[p] = gc_pad[s * C:(s + 1) * C, H + h:H + h + 1]
        grow = gcT[H + h:H + h + 1, s * C:(s + 1) * C]
        diff = gcol[p] - grow
        decay[p] = jnp.where(incl, jnp.exp(jnp.where(incl, diff, 0.0)), 0.0)
        egc[p] = jnp.exp(gcol[p])
    kk = {p: _dot_nt(kn16[p], kn16[p]) for p in probs}
    qkr = {p: _dot_nt(qn[p].astype(BF16), kn16[p]) for p in probs}
    As = [jnp.where(strict, bcol[p] * kk[p] * decay[p], 0.0) for p in probs]
    Ts = _inv_unit_lower_many(As, C, row, col)
    rhs = [jnp.concatenate([bcol[p] * vh[p], (bcol[p] * egc[p]) * kn[p]], axis=1) for p in probs]
    sol = {p: _dot3(T, r) for p, T, r in zip(probs, Ts, rhs)}
    S = {p: S_ref[p[0], p[1]] for p in probs}
    S16 = {p: S[p].astype(BF16) for p in probs}
    kS = {p: _dot(sol[p][:, Dh:].astype(BF16), S16[p]) for p in probs}
    qS = {p: _dot((qn[p] * egc[p]).astype(BF16), S16[p]) for p in probs}
    u_new = {p: (sol[p][:, :Dh] - kS[p]).astype(BF16) for p in probs}
    qk16 = {p: jnp.where(incl, qkr[p] * decay[p], 0.0).astype(BF16) for p in probs}
    o = {p: qS[p] + _dot(qk16[p], u_new[p]) for p in probs}
    dS = {}
    for p in probs:
        g_last = gcol[p][C - 1:C, :]
        kdec = (kn[p] * jnp.exp(g_last - gcol[p])).astype(BF16)
        dS[p] = _dot_tn(kdec, u_new[p])
    for p in probs:
        s, h = p
        g_last = gcol[p][C - 1:C, :]
        S_ref[s, h] = S[p] * jnp.exp(g_last) + dS[p]
        on = o[p] * lax.rsqrt(jnp.mean(o[p] * o[p], -1, keepdims=True) + EPS) * onorm_ref[...]
        zs = z_ref[s * C:(s + 1) * C, h * Dh:(h + 1) * Dh]
        o_ref[s * C:(s + 1) * C, h * Dh:(h + 1) * Dh] = (on * _silu(zs)).astype(o_ref.dtype)


def gdn(u, c0, S0_all, S_out_prev, cw, cb, a_log, dt_bias, o_norm, *, layer, B, L, C, n_valid, nb, qkv_blk, z_blk,
        ba_blk):
    H, Dh = GDN_HEADS, HEAD_DIM
    nc = L // C
    assert B % nb == 0 and (nb == 1 or nc == 1) and nb * C <= LANES
    rows = nb * C
    soff = layer * (B // nb)
    alog = jnp.zeros((1, LANES), F32).at[0, H:2 * H].set(a_log)
    dtb = jnp.zeros((1, LANES), F32).at[0, H:2 * H].set(dt_bias)
    kern = functools.partial(_gdn_kernel, C=C, n_valid=n_valid, nb=nb)
    const2 = lambda b, n: (0, 0)
    in_specs = [
        pl.BlockSpec((rows, GDN_QKV), lambda b, n: (b * nc + n, qkv_blk)),
        pl.BlockSpec((rows, TM_WIDTH), lambda b, n: (b * nc + n, z_blk)),
        pl.BlockSpec((rows, LANES), lambda b, n: (b * nc + n, ba_blk)),
        pl.BlockSpec((CONV_W, GDN_QKV), const2),
        pl.BlockSpec((1, GDN_QKV), const2),
        pl.BlockSpec((nb, CONV_W - 1, GDN_QKV), lambda b, n: (b, 0, 0)),
        pl.BlockSpec((nb, H, Dh, Dh), lambda b, n: (soff + b, 0, 0, 0)),
        pl.BlockSpec((1, LANES), const2),
        pl.BlockSpec((1, LANES), const2),
        pl.BlockSpec((1, Dh), const2),
    ]
    args = [u, u, u, cw, cb.reshape(1, GDN_QKV), c0, S0_all, alog, dtb, o_norm.reshape(1, Dh)]
    aliases = {}
    if S_out_prev is not None:
        in_specs.append(pl.BlockSpec(memory_space=pl.ANY))
        args.append(S_out_prev)
        aliases = {len(args) - 1: 1}
    o, S = pl.pallas_call(
        kern,
        grid=(B // nb, nc),
        in_specs=in_specs,
        out_specs=[
            pl.BlockSpec((rows, TM_WIDTH), lambda b, n: (b * nc + n, 0)),
            pl.BlockSpec((nb, H, Dh, Dh), lambda b, n: (soff + b, 0, 0, 0)),
        ],
        out_shape=[
            jax.ShapeDtypeStruct((B * L, TM_WIDTH), BF16),
            jax.ShapeDtypeStruct(S0_all.shape, F32),
        ],
        scratch_shapes=[pltpu.VMEM((nb, SUBLANES + C, GDN_QKV), F32)],
        input_output_aliases=aliases,
        compiler_params=_cparams(("parallel", "arbitrary")),
        name="gdn",
    )(*args)
    return o, S


def _softmax_rows(s):
    e = jnp.exp(s - jnp.max(s, axis=-1, keepdims=True))
    return e / jnp.sum(e, axis=-1, keepdims=True)


def _xattn_kernel(q_ref, k_ref, v_ref, o_ref):
    scale = HEAD_DIM ** -0.5
    for h in range(XA_HEADS):
        cs = slice(h * HEAD_DIM, (h + 1) * HEAD_DIM)
        kh = k_ref[:, cs].astype(BF16)
        vh = v_ref[:, cs].astype(BF16)
        p = _softmax_rows(_dot_nt(q_ref[:, cs].astype(BF16), kh) * scale)
        o_ref[:, cs] = _dot(p.astype(BF16), vh).astype(o_ref.dtype)


def _xattn_cache_kernel(q_ref, k_ref, v_ref, o_ref, *, nb):
    scale = HEAD_DIM ** -0.5
    tq = q_ref.shape[0] // nb
    rows = XA_HEADS * tq
    cols = N_MEM * XA_HEADS
    own = ((lax.broadcasted_iota(jnp.int32, (rows, cols), 1) % XA_HEADS)
           == (lax.broadcasted_iota(jnp.int32, (rows, cols), 0) // tq))
    sc = []
    for s in range(nb):
        q = q_ref[s * tq:(s + 1) * tq, :]
        qs = jnp.concatenate([q[:, h * HEAD_DIM:(h + 1) * HEAD_DIM] for h in range(XA_HEADS)], axis=0)
        sc.append(_dot_nt(qs.astype(BF16), k_ref[s].astype(BF16)) * scale)
    pr = [_softmax_rows(jnp.where(own, x, NEG_INF)).astype(BF16) for x in sc]
    for s in range(nb):
        o = _dot(pr[s], v_ref[s].astype(BF16))
        for h in range(XA_HEADS):
            o_ref[s * tq:(s + 1) * tq, h * HEAD_DIM:(h + 1) * HEAD_DIM] = o[h * tq:(h + 1) * tq].astype(o_ref.dtype)


def xattn(q, k, v, *, B, Lq, tq, q_blk=0, k_blk=0, v_blk=0):
    nq = Lq // tq
    return pl.pallas_call(
        _xattn_kernel,
        grid=(B, nq),
        in_specs=[
            pl.BlockSpec((tq, XA_WIDTH), lambda b, n: (b * nq + n, q_blk)),
            pl.BlockSpec((N_MEM, XA_WIDTH), lambda b, n: (b, k_blk)),
            pl.BlockSpec((N_MEM, XA_WIDTH), lambda b, n: (b, v_blk)),
        ],
        out_specs=pl.BlockSpec((tq, XA_WIDTH), lambda b, n: (b * nq + n, 0)),
        out_shape=jax.ShapeDtypeStruct((B * Lq, XA_WIDTH), BF16),
        compiler_params=_cparams(("parallel", "arbitrary")),
        name="xattn",
    )(q, k, v)


def xattn_cache(q, k, v, *, layer, B, tq, nb):
    assert B % nb == 0
    boff = layer * (B // nb)
    spec = pl.BlockSpec((nb, N_MEM * XA_HEADS, HEAD_DIM), lambda b: (boff + b, 0, 0))
    return pl.pallas_call(
        functools.partial(_xattn_cache_kernel, nb=nb),
        grid=(B // nb,),
        in_specs=[pl.BlockSpec((nb * tq, XA_WIDTH), lambda b: (b, 0)), spec, spec],
        out_specs=pl.BlockSpec((nb * tq, XA_WIDTH), lambda b: (b, 0)),
        out_shape=jax.ShapeDtypeStruct((B * tq, XA_WIDTH), BF16),
        compiler_params=_cparams(("parallel",)),
        name="xattn_cache",
    )(q, k, v)


PEER_RANKS = PEER_TOPK + 1
GATE_ROWS = 32


def _cand_counts():
    return [PEER_RANKS // (a + 1) for a in range(PEER_RANKS)]


def _sorting_network(n):
    pairs = []
    t = 1
    while t < n:
        p = t
        while p >= 1:
            for j in range(p % t, n - p, 2 * p):
                for i in range(min(p, n - j - p)):
                    if (i + j) // (2 * t) == (i + j + p) // (2 * t):
                        pairs.append((i + j, i + j + p))
            p //= 2
        t *= 2
    return pairs


def _peer_topk_kernel(q_ref, sk_ref, pw_ref, aux_ref, sv_ref, cand_ref):
    K = PEER_TOPK
    q = q_ref[...]
    s = []
    for p in range(2):
        qp = q[:, p * LANES:(p + 1) * LANES]
        s.append(_dot_nt(sk_ref[0, p], qp, precision=lax.Precision.HIGHEST))
    for p in range(2):
        v = [s[p][g * SUBLANES:(g + 1) * SUBLANES, :] for g in range(PEER_NKEYS // SUBLANES)]
        for (a, b) in _sorting_network(len(v)):
            v[a], v[b] = jnp.maximum(v[a], v[b]), jnp.minimum(v[a], v[b])
        v.append(jnp.full_like(v[0], NEG_INF))
        for r in range(PEER_RANKS):
            m = jnp.max(v[0], axis=0, keepdims=True)
            sv_ref[p, r:r + 1, :] = m
            took = v[0] == m
            for k in range(PEER_RANKS - 1 - r):
                v[k] = jnp.where(took, v[k + 1], v[k])
    sv1 = sv_ref[0, 0:PEER_RANKS, :]
    sv2 = sv_ref[1, 0:PEER_RANKS, :]
    cand_ref[...] = jnp.full(cand_ref.shape, NEG_INF, F32)
    off = 0
    for a, nb in enumerate(_cand_counts()):
        cand_ref[off:off + nb, :] = sv1[a:a + 1, :] + sv2[0:nb, :]
        off += nb
    cur = cand_ref[...]
    tops = []
    for r in range(PEER_RANKS):
        m = jnp.max(cur, axis=0, keepdims=True)
        tops.append(m)
        cur = jnp.where(cur == m, NEG_INF, cur)
    z = jnp.ones_like(tops[0])
    for r in range(1, K):
        z = z + jnp.exp(tops[r] - tops[0])
    thr = 0.5 * (tops[K - 1] + tops[K])
    pw_ref[0, 0] = thr - s[0]
    pw_ref[0, 1] = s[1]
    pw_ref[0, 2] = jnp.exp(s[1] - sv2[0:1, :])
    aux_ref[0] = jnp.broadcast_to(thr - sv1[0:1, :] - jnp.log(z), aux_ref.shape[1:])


def peer_topk(q, subkeys, *, tt=512):
    T = q.shape[0]
    tt = min(tt, T)
    assert T % tt == 0
    H, NK = PEER_HEADS, PEER_NKEYS
    pad8 = lambda n: -(-n // SUBLANES) * SUBLANES
    return pl.pallas_call(
        _peer_topk_kernel,
        grid=(T // tt, H),
        in_specs=[
            pl.BlockSpec((tt, 2 * LANES), lambda i, h: (i, h)),
            pl.BlockSpec((1, 2, NK, LANES), lambda i, h: (h, 0, 0, 0)),
        ],
        out_specs=[
            pl.BlockSpec((1, 3, NK, tt), lambda i, h: (h, 0, 0, i)),
            pl.BlockSpec((1, SUBLANES, tt), lambda i, h: (h, 0, i)),
        ],
        out_shape=[
            jax.ShapeDtypeStruct((H, 3, NK, T), F32),
            jax.ShapeDtypeStruct((H, SUBLANES, T), F32),
        ],
        scratch_shapes=[pltpu.VMEM((2, pad8(PEER_RANKS), tt), F32), pltpu.VMEM((pad8(sum(_cand_counts())), tt), F32)],
        compiler_params=_cparams(("parallel", "arbitrary")),
        name="peer_topk",
    )(q, subkeys)


def _peer_dense_kernel(*refs, ib, emit_tables):
    first = 0 if emit_tables else 1
    offs_ref = None if emit_tables else refs[0]
    x_ref, g_ref, pw_ref, aux_ref, u_ref, v_ref, o_ref = refs[first:first + 7]
    rest = refs[first + 7:]
    xnT_ref, acc_ref, w_ref = rest[-3:]
    e = pl.program_id(1)
    H, NK = PEER_HEADS, PEER_NKEYS
    tt = x_ref.shape[0]
    strip = min(LANES, tt)
    assert tt % strip == 0

    def gates(blk, ii, dst):
        i = blk * ib + ii
        thr1 = [pw_ref[h, 0, pl.ds(i, 1), :] for h in range(H)]
        e1 = [0.5 * jnp.exp(aux_ref[h, 0:1, :] - thr1[h]) for h in range(H)]
        for tc in range(tt // strip):
            ts_ = slice(tc * strip, (tc + 1) * strip)
            for j0 in range(0, NK, GATE_ROWS):
                js = slice(j0, j0 + GATE_ROWS)
                w = None
                for h in range(H):
                    sel = pw_ref[h, 1, js, ts_] >= thr1[h][:, ts_]
                    c = jnp.where(sel, pw_ref[h, 2, js, ts_], 0.0) * e1[h][:, ts_]
                    w = c if w is None else w + c
                dst[ii * NK + j0:ii * NK + j0 + GATE_ROWS, ts_] = w
        last = jnp.sum(w[0:1, 0:1])
        return last != last

    @pl.when(e == 0)
    def _():
        xn = _rms(x_ref[...], g_ref[...])
        xnT_ref[...] = xn.T.astype(BF16)
        acc_ref[...] = jnp.zeros_like(acc_ref)

    v16 = v_ref[...].astype(BF16)
    if emit_tables:
        u16 = u_ref[...].astype(BF16)
        rest[0][...] = u16
        rest[1][...] = v16
        hT = _dot(u16, xnT_ref[...])
        for ii in range(ib):
            gates(e, ii, w_ref)
    else:
        hs = []
        for ii in range(ib):
            bad = gates(e, ii, w_ref)
            off = pl.multiple_of(jnp.where(bad, offs_ref[ii], ii * NK), NK)
            hs.append(_dot(u_ref[pl.ds(off, NK), :], xnT_ref[...]))
        hT = jnp.concatenate(hs, axis=0)
    aT = w_ref[...] * _gelu_tanh_x2(hT)
    acc_ref[...] += _dot(aT.T.astype(BF16), v16)

    @pl.when(e == pl.num_programs(1) - 1)
    def _():
        o_ref[...] = x_ref[...] + acc_ref[...]


def peer_dense(x, g, pw, aux, u_tab, v_tab, *, layer=0, emit_tables=False, tt=512, ib=None):
    T, D = x.shape
    if ib is None:
        ib = 2 if emit_tables else 4
    boff = layer * (PEER_NKEYS // ib)
    tt = min(tt, T)
    assert T % tt == 0 and PEER_NKEYS % ib == 0 and not (emit_tables and T != tt)
    H, NK = PEER_HEADS, PEER_NKEYS
    eb = ib * NK
    out_specs = [pl.BlockSpec((tt, D), lambda i, e, *_: (i, 0))]
    out_shape = [jax.ShapeDtypeStruct((T, D), F32)]
    args = (x, g.reshape(1, D), pw, aux, u_tab, v_tab)
    if emit_tables:
        out_specs += [pl.BlockSpec((eb, D), lambda i, e, *_: (e, 0))] * 2
        out_shape += [jax.ShapeDtypeStruct((NK * NK, D), BF16)] * 2
    else:
        args = (jnp.arange(ib, dtype=jnp.int32) * NK,) + args
    outs = pl.pallas_call(
        functools.partial(_peer_dense_kernel, ib=ib, emit_tables=emit_tables),
        grid_spec=pltpu.PrefetchScalarGridSpec(
            num_scalar_prefetch=0 if emit_tables else 1,
            grid=(T // tt, NK // ib),
            in_specs=[
                pl.BlockSpec((tt, D), lambda i, e, *_: (i, 0)),
                pl.BlockSpec((1, D), lambda i, e, *_: (0, 0)),
                pl.BlockSpec((H, 3, NK, tt), lambda i, e, *_: (0, 0, 0, i)),
                pl.BlockSpec((H, SUBLANES, tt), lambda i, e, *_: (0, 0, i)),
                pl.BlockSpec((eb, D), lambda i, e, *_: (boff + e, 0)),
                pl.BlockSpec((eb, D), lambda i, e, *_: (boff + e, 0)),
            ],
            out_specs=out_specs,
            scratch_shapes=[
                pltpu.VMEM((D, tt), BF16),
                pltpu.VMEM((tt, D), F32),
                pltpu.VMEM((eb, tt), F32),
            ],
        ),
        out_shape=out_shape,
        compiler_params=_cparams(("parallel", "arbitrary")),
        name="peer_dense",
    )(*args)
    return tuple(outs) if emit_tables else outs[0]


def _prep_weights(w_in_a, w_in_b, w_out, w_mem_kv, peer_wq, peer_u, peer_v, lru_wr, lru_wi):
    H = GDN_HEADS
    qkvz = GDN_QKV + TM_WIDTH
    wb = jnp.concatenate(
        [
            w_in_b[:, :, :qkvz],
            w_in_b[:, :, qkvz + 2 * H:],
            w_in_b[:, :, qkvz:qkvz + 2 * H],
            jnp.zeros(w_in_b.shape[:2] + (B_IN_PAD - w_in_b.shape[2],), w_in_b.dtype),
        ],
        axis=-1,
    )
    return dict(
        w_in_a=w_in_a.astype(BF16),
        w_in_b=wb.astype(BF16),
        w_out_tm=w_out[:, :TM_WIDTH].astype(BF16),
        w_out_xa=w_out[:, TM_WIDTH:].astype(BF16),
        w_mem_kv=w_mem_kv.astype(BF16),
        peer_wq=peer_wq.astype(BF16),
        peer_u=peer_u.reshape(-1, peer_u.shape[-1]),
        peer_v=peer_v.reshape(-1, peer_v.shape[-1]),
        lru_wr=lru_wr.astype(BF16),
        lru_wi=lru_wi.astype(BF16),
    )


def _trunk(x, mem, lru_h0, lru_c0, gdn_S0, gdn_c0, w, wc, *, B, L, time_major, peer_tabs16=None):
    if time_major:
        G, R, tl = 1, B, L
    else:
        G, R, tl = B, 1, min(L, 256)
    Lp = -(-L // SUBLANES) * SUBLANES
    hist = CONV_W - 1

    def to_batch_major(a):
        c = a.shape[-1]
        a = a.reshape(L, B, c).transpose(1, 0, 2)
        return jnp.pad(a, ((0, 0), (0, Lp - L), (0, 0))).reshape(B * Lp, c)

    def to_time_major(a):
        c = a.shape[-1]
        return a.reshape(B, Lp, c)[:, :L].transpose(1, 0, 2).reshape(L * B, c)

    def last_rows(u, c0, c1):
        if time_major:
            return u.reshape(L, B, -1)[L - hist:, :, c0:c1].transpose(1, 0, 2)
        return u.reshape(B, L, -1)[:, L - hist:, c0:c1]

    n_b = gdn_S0.shape[0]
    S0_all = gdn_S0.reshape((n_b * B,) + gdn_S0.shape[2:])
    S_all = None
    lru_h, lru_c, gdn_c, tabs16 = [], [], [], []
    for l in range(DEPTH):
        j = l // 2
        if l % 2 == 0:
            u, qmem = norm_matmul(x, w['norm_mix'][l], wc['w_in_a'][j], side_cols=(2 * TM_WIDTH, XA_WIDTH))
            c0 = lru_c0[j]
            if time_major:
                c0k = c0.transpose(1, 0, 2).reshape(1, hist * B, TM_WIDTH)
                h0k = lru_h0[j].reshape(1, B, TM_WIDTH)
            else:
                c0k = c0
                h0k = lru_h0[j].reshape(B, 1, TM_WIDTH)
            tm, h = rglru(u, c0k, h0k, w['lru_conv_w'][j], w['lru_conv_b'][j], wc['lru_wr'][j], w['lru_br'][j],
                          wc['lru_wi'][j], w['lru_bi'][j], w['lru_lambda'][j], G=G, L=L, R=R, tl=tl)
            lru_h.append(h.reshape(B, TM_WIDTH))
            lru_c.append(last_rows(u, TM_WIDTH, 2 * TM_WIDTH))
        else:
            u, qmem = norm_matmul(x, w['norm_mix'][l], wc['w_in_b'][j], side_cols=(GDN_QKV + TM_WIDTH, XA_WIDTH))
            C = min(GDN_CHUNK, Lp)
            nb = GDN_SEQS_PER_STEP if (Lp == C and B % GDN_SEQS_PER_STEP == 0) else 1
            u_bm = to_batch_major(u) if time_major else u
            o_bm, S_all = gdn(u_bm, gdn_c0[j], S0_all, S_all, w['gdn_conv_w'][j], w['gdn_conv_b'][j],
                              w['gdn_a_log'][j], w['gdn_dt_bias'][j], w['gdn_o_norm'][j], layer=j, B=B, L=Lp, C=C,
                              n_valid=min(L, C), nb=nb, qkv_blk=0, z_blk=GDN_QKV // TM_WIDTH,
                              ba_blk=(GDN_QKV + TM_WIDTH + XA_WIDTH) // LANES)
            tm = to_time_major(o_bm) if time_major else o_bm
            gdn_c.append(last_rows(u, 0, GDN_QKV))
        if time_major:
            q_bm = to_batch_major(qmem)
            xa = to_time_major(xattn_cache(q_bm, mem[0], mem[1], layer=l, B=B, tq=Lp,
                                           nb=XATTN_SEQS_PER_STEP if B % XATTN_SEQS_PER_STEP == 0 else 1))
        else:
            xa = xattn(qmem, mem[l], mem[l], B=B, Lq=L, tq=min(L, 512), q_blk=0, k_blk=0, v_blk=1)
        x = out_proj(tm, xa, wc['w_out_tm'][l], wc['w_out_xa'][l], x)
        q = norm_matmul(x, w['norm_ffn'][l], wc['peer_wq'][l])
        pw, aux = peer_topk(q, w['peer_subkeys'][l])
        if peer_tabs16 is None:
            x, u16, v16 = peer_dense(x, w['norm_ffn'][l], pw, aux, wc['peer_u'], wc['peer_v'], layer=l,
                                     emit_tables=True)
            tabs16.append((u16, v16))
        else:
            x = peer_dense(x, w['norm_ffn'][l], pw, aux, peer_tabs16[l][0], peer_tabs16[l][1])
    y = rmsnorm_rows(x, w['norm_final'])
    return (y, jnp.stack(lru_h), jnp.stack(lru_c), S_all.reshape(gdn_S0.shape), jnp.stack(gdn_c)), tabs16


def kernel(x_prompt, x_sample, state_rglru_h, state_rglru_conv, state_gdn_S, state_gdn_conv, cache_mem_k, cache_mem_v, mem_prompt, norm_mix, norm_ffn, norm_final, w_in_a, w_in_b, w_out, lru_conv_w, lru_conv_b, lru_wr, lru_br, lru_wi, lru_bi, lru_lambda, gdn_conv_w, gdn_conv_b, gdn_a_log, gdn_dt_bias, gdn_o_norm, norm_mem, w_mem_kv, peer_wq, peer_subkeys, peer_u, peer_v):
    w = dict(norm_mix=norm_mix, norm_ffn=norm_ffn, norm_final=norm_final, lru_conv_w=lru_conv_w,
             lru_conv_b=lru_conv_b, lru_br=lru_br, lru_bi=lru_bi, lru_lambda=lru_lambda, gdn_conv_w=gdn_conv_w,
             gdn_conv_b=gdn_conv_b, gdn_a_log=gdn_a_log, gdn_dt_bias=gdn_dt_bias, gdn_o_norm=gdn_o_norm,
             peer_subkeys=peer_subkeys)
    wc = _prep_weights(w_in_a, w_in_b, w_out, w_mem_kv, peer_wq, peer_u, peer_v, lru_wr, lru_wi)
    Bp, Lp_, D = x_prompt.shape
    Bs, Ls, _ = x_sample.shape
    n_a, n_b = state_rglru_h.shape[0], state_gdn_S.shape[0]

    xs = x_sample.transpose(1, 0, 2).reshape(Ls * Bs, D)
    rows_hd = (DEPTH * Bs, N_MEM * XA_HEADS, HEAD_DIM)
    (y_s, s_h, s_ca, s_S, s_cb), tabs16 = _trunk(
        xs, (cache_mem_k.reshape(rows_hd), cache_mem_v.reshape(rows_hd)), state_rglru_h, state_rglru_conv,
        state_gdn_S, state_gdn_conv, w, wc, B=Bs, L=Ls, time_major=True)
    y_s = y_s.reshape(Ls, Bs, D).transpose(1, 0, 2)

    mem2 = mem_prompt.reshape(Bp * N_MEM, D)
    kv = [norm_matmul(mem2, norm_mem[l], wc['w_mem_kv'][l]) for l in range(DEPTH)]
    (y_p, p_h, p_ca, p_S, p_cb), _ = _trunk(
        x_prompt.reshape(Bp * Lp_, D), kv,
        jnp.zeros((n_a, Bp, TM_WIDTH), F32), jnp.zeros((n_a, Bp, CONV_W - 1, TM_WIDTH), F32),
        jnp.zeros((n_b, Bp, GDN_HEADS, HEAD_DIM, HEAD_DIM), F32), jnp.zeros((n_b, Bp, CONV_W - 1, GDN_QKV), F32),
        w, wc, B=Bp, L=Lp_, time_major=False, peer_tabs16=tabs16)
    p_mem_k = jnp.stack([a[:, :XA_WIDTH].reshape(Bp, N_MEM, XA_HEADS, HEAD_DIM) for a in kv])
    p_mem_v = jnp.stack([a[:, XA_WIDTH:].reshape(Bp, N_MEM, XA_HEADS, HEAD_DIM) for a in kv])
    return (y_p.reshape(Bp, Lp_, D), y_s, p_h, p_ca, p_S, p_cb, p_mem_k, p_mem_v, s_h, s_ca, s_S, s_cb)
```

```python
import functools
import math

import jax
import jax.numpy as jnp
from jax import lax
from jax.experimental import pallas as pl
from jax.experimental.pallas import tpu as pltpu

F32 = jnp.float32
BF16 = jnp.bfloat16

D_MODEL = 2048
DEPTH = 4
HEAD_DIM = 128
XA_HEADS = 4
XA_WIDTH = XA_HEADS * HEAD_DIM
TM_WIDTH = D_MODEL - XA_WIDTH
N_MEM = 256
CONV_W = 4
EPS = 1e-6
LRU_BLOCKS = TM_WIDTH // HEAD_DIM
LRU_C = 8.0
GDN_HEADS = TM_WIDTH // HEAD_DIM
GDN_QKV = 3 * TM_WIDTH
GDN_CHUNK = 64
GDN_SEQS_PER_STEP = 4
XATTN_SEQS_PER_STEP = 4
PEER_HEADS = 8
PEER_NKEYS = 128
PEER_TOPK = 16
B_IN_PAD = 7168

SUBLANES = 8
LANES = 128
VMEM_LIMIT = 52 * 1024 * 1024

NEG_INF = float("-inf")


def _cparams(sem):
    return pltpu.CompilerParams(dimension_semantics=sem, vmem_limit_bytes=VMEM_LIMIT)


def _dot(a, b, precision=None):
    return jnp.dot(a, b, preferred_element_type=F32, precision=precision)


def _mm(a, b):
    return jnp.dot(a.astype(BF16), b.astype(BF16), preferred_element_type=F32)


def _split_bf16(a):
    hi = a.astype(BF16)
    return hi, (a - hi.astype(F32)).astype(BF16)


def _dot3(a, b):
    ah, al = _split_bf16(a)
    bh, bl = _split_bf16(b)
    return _dot(ah, bh) + (_dot(ah, bl) + _dot(al, bh))


def _dot_nt(a, b, precision=None):
    return lax.dot_general(a, b, (((1,), (1,)), ((), ())), preferred_element_type=F32, precision=precision)


def _dot_tn(a, b, precision=None):
    return lax.dot_general(a, b, (((0,), (0,)), ((), ())), preferred_element_type=F32, precision=precision)


def _sigmoid(x):
    return 1.0 / (1.0 + jnp.exp(-x))


def _silu(x):
    return x * _sigmoid(x)


def _gelu_tanh(x):
    c = math.sqrt(2.0 / math.pi)
    return 0.5 * x * (1.0 + jnp.tanh(c * (x + 0.044715 * (x * x * x))))


def _gelu_tanh_x2(x):
    c = math.sqrt(2.0 / math.pi)
    return x * (1.0 + jnp.tanh(x * (c + (0.044715 * c) * (x * x))))


def _softplus(x):
    return jnp.maximum(x, 0.0) + jnp.log(1.0 + jnp.exp(-jnp.abs(x)))


def _rms(x, g):
    ms = jnp.mean(x * x, axis=-1, keepdims=True)
    return x * lax.rsqrt(ms + EPS) * g


def _norm_matmul_kernel(x_ref, g_ref, w_ref, o_ref, *rest, side):
    xn_ref = rest[-1]

    @pl.when(pl.program_id(1) == 0)
    def _():
        xn_ref[...] = _rms(x_ref[...], g_ref[...]).astype(BF16)

    acc = _dot(xn_ref[...], w_ref[...])
    o_ref[...] = acc
    if side is not None:
        tile, off, width = side

        @pl.when(pl.program_id(1) == tile)
        def _():
            rest[0][...] = acc[:, off:off + width]


def _matmul_tiles(m, n, tm, tn):
    tm = min(tm, m)
    while m % tm:
        tm //= 2
    while n % tn:
        tn //= 2
    return tm, tn


def norm_matmul(x, g, w, *, side_cols=None, tm=1024, tn=1024):
    m, k = x.shape
    n = w.shape[1]
    tm, tn = _matmul_tiles(m, n, tm, tn)
    assert tm % SUBLANES == 0 and tn % LANES == 0
    out_specs = [pl.BlockSpec((tm, tn), lambda i, j: (i, j))]
    out_shape = [jax.ShapeDtypeStruct((m, n), F32)]
    side = None
    if side_cols is not None:
        start, width = side_cols
        side = (start // tn, start % tn, width)
        assert start % tn + width <= tn
        out_specs.append(pl.BlockSpec((tm, width), lambda i, j: (i, 0)))
        out_shape.append(jax.ShapeDtypeStruct((m, width), F32))
    outs = pl.pallas_call(
        functools.partial(_norm_matmul_kernel, side=side),
        grid=(m // tm, n // tn),
        in_specs=[
            pl.BlockSpec((tm, k), lambda i, j: (i, 0)),
            pl.BlockSpec((1, k), lambda i, j: (0, 0)),
            pl.BlockSpec((k, tn), lambda i, j: (0, j)),
        ],
        out_specs=out_specs,
        out_shape=out_shape,
        scratch_shapes=[pltpu.VMEM((tm, k), BF16)],
        compiler_params=_cparams(("parallel", "arbitrary")),
        name="norm_matmul",
    )(x, g.reshape(1, k), w)
    return outs[0] if side_cols is None else tuple(outs)


def _out_proj_kernel(a_ref, b_ref, wa_ref, wb_ref, r_ref, o_ref):
    acc = _dot(a_ref[...], wa_ref[...]) + _dot(b_ref[...], wb_ref[...])
    o_ref[...] = r_ref[...] + acc


def out_proj(a, b, wa, wb, r, *, tm=1024, tn=1024):
    m, ka = a.shape
    kb = b.shape[1]
    n = wa.shape[1]
    tm, tn = _matmul_tiles(m, n, tm, tn)
    assert tm % SUBLANES == 0 and tn % LANES == 0
    return pl.pallas_call(
        _out_proj_kernel,
        grid=(m // tm, n // tn),
        in_specs=[
            pl.BlockSpec((tm, ka), lambda i, j: (i, 0)),
            pl.BlockSpec((tm, kb), lambda i, j: (i, 0)),
            pl.BlockSpec((ka, tn), lambda i, j: (0, j)),
            pl.BlockSpec((kb, tn), lambda i, j: (0, j)),
            pl.BlockSpec((tm, tn), lambda i, j: (i, j)),
        ],
        out_specs=pl.BlockSpec((tm, tn), lambda i, j: (i, j)),
        out_shape=jax.ShapeDtypeStruct((m, n), F32),
        compiler_params=_cparams(("parallel", "arbitrary")),
        name="out_proj",
    )(a, b, wa, wb, r)


def _rmsnorm_kernel(x_ref, g_ref, o_ref):
    o_ref[...] = _rms(x_ref[...], g_ref[...])


def rmsnorm_rows(x, g, *, tm=512):
    m, k = x.shape
    tm = min(tm, m)
    assert m % tm == 0
    return pl.pallas_call(
        _rmsnorm_kernel,
        grid=(m // tm,),
        in_specs=[pl.BlockSpec((tm, k), lambda i: (i, 0)), pl.BlockSpec((1, k), lambda i: (0, 0))],
        out_specs=pl.BlockSpec((tm, k), lambda i: (i, 0)),
        out_shape=jax.ShapeDtypeStruct((m, k), F32),
        compiler_params=_cparams(("parallel",)),
        name="final_rmsnorm",
    )(x, g.reshape(1, k))


def _rglru_kernel(gate_ref, xr_ref, cw_ref, cb_ref, c0_ref, h0_ref, wr_ref, br_ref, wi_ref, bi_ref, lam_ref,
                  y_ref, hout_ref, hist_ref, a_ref, b_ref, hs_ref, h_ref, *, tl, R):
    n = pl.program_id(1)
    rows = tl * R
    hist_rows = (CONV_W - 1) * R
    off0 = -(-hist_rows // SUBLANES) * SUBLANES

    @pl.when(n == 0)
    def _():
        hist_ref[off0 - hist_rows:off0, :] = c0_ref[0]
        h_ref[...] = h0_ref[0]

    x = xr_ref[...]
    hist_ref[off0:off0 + rows, :] = x
    xc = cb_ref[...]
    for j in range(CONV_W):
        s = off0 - hist_rows + j * R
        xc = xc + hist_ref[s:s + rows, :] * cw_ref[j:j + 1, :]
    tail = hist_ref[off0 + rows - hist_rows:off0 + rows, :]
    hist_ref[off0 - hist_rows:off0, :] = tail

    sp = _softplus(-lam_ref[...])
    for blk in range(LRU_BLOCKS):
        cs = slice(blk * HEAD_DIM, (blk + 1) * HEAD_DIM)
        xb = xc[:, cs]
        xb16 = xb.astype(BF16)
        r = _sigmoid(_dot(xb16, wr_ref[blk]) + br_ref[blk])
        i = _sigmoid(_dot(xb16, wi_ref[blk]) + bi_ref[blk])
        log_a = (-LRU_C) * r * sp[:, cs]
        a = jnp.exp(log_a)
        a_ref[:, cs] = a
        b_ref[:, cs] = jnp.sqrt(1.0 - a * a) * (i * xb)

    if R % SUBLANES == 0:
        def step(l, carry):
            rs = pl.ds(pl.multiple_of(l * R, SUBLANES), R)
            h = a_ref[rs, :] * h_ref[...] + b_ref[rs, :]
            h_ref[...] = h
            hs_ref[rs, :] = h
            return carry

        lax.fori_loop(0, tl, step, 0)
    else:
        def step(l, h):
            rs = pl.ds(l * R, R)
            h = a_ref[rs, :] * h + b_ref[rs, :]
            hs_ref[rs, :] = h
            return h

        h_ref[...] = lax.fori_loop(0, tl, step, h_ref[...], unroll=SUBLANES)
    y_ref[...] = (hs_ref[...] * _gelu_tanh(gate_ref[...])).astype(y_ref.dtype)
    hout_ref[0] = h_ref[...]


def rglru(u, c0, h0, cw, cb, wr, br, wi, bi, lam, *, G, L, R, tl):
    W = TM_WIDTH
    assert L % tl == 0
    rows = tl * R
    nt = L // tl
    hist_rows = (CONV_W - 1) * R
    off0 = -(-hist_rows // SUBLANES) * SUBLANES
    kern = functools.partial(_rglru_kernel, tl=tl, R=R)
    const2 = lambda g, n: (0, 0)
    const3 = lambda g, n: (0, 0, 0)
    y, hout = pl.pallas_call(
        kern,
        grid=(G, nt),
        in_specs=[
            pl.BlockSpec((rows, W), lambda g, n: (g * nt + n, 0)),
            pl.BlockSpec((rows, W), lambda g, n: (g * nt + n, 1)),
            pl.BlockSpec((CONV_W, W), const2),
            pl.BlockSpec((1, W), const2),
            pl.BlockSpec((1, hist_rows, W), lambda g, n: (g, 0, 0)),
            pl.BlockSpec((1, R, W), lambda g, n: (g, 0, 0)),
            pl.BlockSpec((LRU_BLOCKS, HEAD_DIM, HEAD_DIM), const3),
            pl.BlockSpec((LRU_BLOCKS, 1, HEAD_DIM), const3),
            pl.BlockSpec((LRU_BLOCKS, HEAD_DIM, HEAD_DIM), const3),
            pl.BlockSpec((LRU_BLOCKS, 1, HEAD_DIM), const3),
            pl.BlockSpec((1, W), const2),
        ],
        out_specs=[
            pl.BlockSpec((rows, W), lambda g, n: (g * nt + n, 0)),
            pl.BlockSpec((1, R, W), lambda g, n: (g, 0, 0)),
        ],
        out_shape=[
            jax.ShapeDtypeStruct((G * L * R, W), BF16),
            jax.ShapeDtypeStruct((G, R, W), F32),
        ],
        scratch_shapes=[
            pltpu.VMEM((off0 + rows, W), F32),
            pltpu.VMEM((rows, W), F32),
            pltpu.VMEM((rows, W), F32),
            pltpu.VMEM((rows, W), F32),
            pltpu.VMEM((R, W), F32),
        ],
        compiler_params=_cparams(("parallel", "arbitrary")),
        name="rglru",
    )(u, u, cw, cb.reshape(1, W), c0, h0, wr, br.reshape(LRU_BLOCKS, 1, HEAD_DIM), wi,
      bi.reshape(LRU_BLOCKS, 1, HEAD_DIM), lam.reshape(1, W))
    return y, hout


def _inv_unit_lower_many(As, C, row, col):
    eye = (row == col).astype(F32)
    base = min(SUBLANES, C)
    sh = int(math.log2(base))
    dmask = (row >> sh) == (col >> sh)
    Ps = [jnp.where(dmask, A, 0.0) for A in As]
    Ts = [eye - P for P in Ps]
    span = 2
    while span < base:
        Ps = [_dot3(P, P) for P in Ps]
        Ts = [_dot3(T, eye + P) for T, P in zip(Ts, Ps)]
        span *= 2
    s = base
    while s < C:
        sh = int(math.log2(s))
        off = ((row >> (sh + 1)) == (col >> (sh + 1))) & ((row >> sh) != (col >> sh))
        Ms = [_dot3(jnp.where(off, A, 0.0), T) for A, T in zip(As, Ts)]
        Ts = [T - _dot3(T, M) for T, M in zip(Ts, Ms)]
        s *= 2
    return Ts


def _gdn_kernel(qkv_ref, z_ref, ba_ref, cw_ref, cb_ref, c0_ref, S0_ref, alog_ref, dtb_ref, onorm_ref,
                *rest, C, n_valid, nb):
    o_ref, S_ref, hist_ref = rest[-3:]
    n = pl.program_id(1)
    H, Dh = GDN_HEADS, HEAD_DIM
    hist_rows = CONV_W - 1
    off0 = SUBLANES
    rows = nb * C

    @pl.when(n == 0)
    def _():
        for s in range(nb):
            hist_ref[s, off0 - hist_rows:off0, :] = c0_ref[s]
        S_ref[...] = S0_ref[...]

    ts = []
    for s in range(nb):
        hist_ref[s, off0:off0 + C, :] = qkv_ref[s * C:(s + 1) * C, :]
        xc = cb_ref[...]
        for j in range(CONV_W):
            r0 = off0 - hist_rows + j
            xc = xc + hist_ref[s, r0:r0 + C, :] * cw_ref[j:j + 1, :]
        tail = hist_ref[s, off0 + C - hist_rows:off0 + C, :]
        hist_ref[s, off0 - hist_rows:off0, :] = tail
        ts.append(_silu(xc))

    ba = ba_ref[...]
    beta_all = _sigmoid(ba)
    g_all = -jnp.exp(alog_ref[...]) * _softplus(ba + dtb_ref[...])
    if n_valid < C:
        rmask = (lax.broadcasted_iota(jnp.int32, (rows, LANES), 0) & (C - 1)) < n_valid
        beta_all = jnp.where(rmask, beta_all, 0.0)
        g_all = jnp.where(rmask, g_all, 0.0)
    if rows < LANES:
        g_pad = jnp.concatenate([g_all, jnp.zeros((LANES - rows, LANES), F32)], axis=0)
    else:
        g_pad = g_all
    r128 = lax.broadcasted_iota(jnp.int32, (LANES, LANES), 0)
    c128 = lax.broadcasted_iota(jnp.int32, (LANES, LANES), 1)
    shc = int(math.log2(C))
    tril = ((c128 <= r128) & ((c128 >> shc) == (r128 >> shc))).astype(F32)
    gc_pad = _dot(tril, g_pad, precision=lax.Precision.HIGHEST)
    gcT = gc_pad.T

    row = lax.broadcasted_iota(jnp.int32, (C, C), 0)
    col = lax.broadcasted_iota(jnp.int32, (C, C), 1)
    incl = col <= row
    strict = col < row

    probs = [(s, h) for s in range(nb) for h in range(H)]
    qn, kn, kn16, vh, bcol, gcol, decay, egc = {}, {}, {}, {}, {}, {}, {}, {}
    for p in probs:
        s, h = p
        t = ts[s]
        qh = t[:, h * Dh:(h + 1) * Dh]
        kh = t[:, TM_WIDTH + h * Dh:TM_WIDTH + (h + 1) * Dh]
        vh[p] = t[:, 2 * TM_WIDTH + h * Dh:2 * TM_WIDTH + (h + 1) * Dh]
        qn[p] = qh * lax.rsqrt(jnp.sum(qh * qh, -1, keepdims=True) + EPS) * (Dh ** -0.5)
        kn[p] = kh * lax.rsqrt(jnp.sum(kh * kh, -1, keepdims=True) + EPS)
        kn16[p] = kn[p].astype(BF16)
        bcol[p] = beta_all[s * C:(s + 1) * C, h:h + 1]
        gcol[p] = gc_pad[s * C:(s + 1) * C, H + h:H + h + 1]
        grow = gcT[H + h:H + h + 1, s * C:(s + 1) * C]
        diff = gcol[p] - grow
        decay[p] = jnp.where(incl, jnp.exp(jnp.where(incl, diff, 0.0)), 0.0)
        egc[p] = jnp.exp(gcol[p])
    kk = {p: _dot_nt(kn16[p], kn16[p]) for p in probs}
    qkr = {p: _dot_nt(qn[p].astype(BF16), kn16[p]) for p in probs}
    As = [jnp.where(strict, bcol[p] * kk[p] * decay[p], 0.0) for p in probs]
    Ts = _inv_unit_lower_many(As, C, row, col)
    rhs = [jnp.concatenate([bcol[p] * vh[p], (bcol[p] * egc[p]) * kn[p]], axis=1) for p in probs]
    sol = {p: _dot3(T, r) for p, T, r in zip(probs, Ts, rhs)}
    S = {p: S_ref[p[0], p[1]] for p in probs}
    S16 = {p: S[p].astype(BF16) for p in probs}
    kS = {p: _dot(sol[p][:, Dh:].astype(BF16), S16[p]) for p in probs}
    qS = {p: _dot((qn[p] * egc[p]).astype(BF16), S16[p]) for p in probs}
    u_new = {p: (sol[p][:, :Dh] - kS[p]).astype(BF16) for p in probs}
    qk16 = {p: jnp.where(incl, qkr[p] * decay[p], 0.0).astype(BF16) for p in probs}
    o = {p: qS[p] + _dot(qk16[p], u_new[p]) for p in probs}
    dS = {}
    for p in probs:
        g_last = gcol[p][C - 1:C, :]
        kdec = (kn[p] * jnp.exp(g_last - gcol[p])).astype(BF16)
        dS[p] = _dot_tn(kdec, u_new[p])
    for p in probs:
        s, h = p
        g_last = gcol[p][C - 1:C, :]
        S_ref[s, h] = S[p] * jnp.exp(g_last) + dS[p]
        on = o[p] * lax.rsqrt(jnp.mean(o[p] * o[p], -1, keepdims=True) + EPS) * onorm_ref[...]
        zs = z_ref[s * C:(s + 1) * C, h * Dh:(h + 1) * Dh]
        o_ref[s * C:(s + 1) * C, h * Dh:(h + 1) * Dh] = (on * _silu(zs)).astype(o_ref.dtype)


def gdn(u, c0, S0_all, S_out_prev, cw, cb, a_log, dt_bias, o_norm, *, layer, B, L, C, n_valid, nb, qkv_blk, z_blk,
        ba_blk):
    H, Dh = GDN_HEADS, HEAD_DIM
    nc = L // C
    assert B % nb == 0 and (nb == 1 or nc == 1) and nb * C <= LANES
    rows = nb * C
    soff = layer * (B // nb)
    alog = jnp.zeros((1, LANES), F32).at[0, H:2 * H].set(a_log)
    dtb = jnp.zeros((1, LANES), F32).at[0, H:2 * H].set(dt_bias)
    kern = functools.partial(_gdn_kernel, C=C, n_valid=n_valid, nb=nb)
    const2 = lambda b, n: (0, 0)
    in_specs = [
        pl.BlockSpec((rows, GDN_QKV), lambda b, n: (b * nc + n, qkv_blk)),
        pl.BlockSpec((rows, TM_WIDTH), lambda b, n: (b * nc + n, z_blk)),
        pl.BlockSpec((rows, LANES), lambda b, n: (b * nc + n, ba_blk)),
        pl.BlockSpec((CONV_W, GDN_QKV), const2),
        pl.BlockSpec((1, GDN_QKV), const2),
        pl.BlockSpec((nb, CONV_W - 1, GDN_QKV), lambda b, n: (b, 0, 0)),
        pl.BlockSpec((nb, H, Dh, Dh), lambda b, n: (soff + b, 0, 0, 0)),
        pl.BlockSpec((1, LANES), const2),
        pl.BlockSpec((1, LANES), const2),
        pl.BlockSpec((1, Dh), const2),
    ]
    args = [u, u, u, cw, cb.reshape(1, GDN_QKV), c0, S0_all, alog, dtb, o_norm.reshape(1, Dh)]
    aliases = {}
    if S_out_prev is not None:
        in_specs.append(pl.BlockSpec(memory_space=pl.ANY))
        args.append(S_out_prev)
        aliases = {len(args) - 1: 1}
    o, S = pl.pallas_call(
        kern,
        grid=(B // nb, nc),
        in_specs=in_specs,
        out_specs=[
            pl.BlockSpec((rows, TM_WIDTH), lambda b, n: (b * nc + n, 0)),
            pl.BlockSpec((nb, H, Dh, Dh), lambda b, n: (soff + b, 0, 0, 0)),
        ],
        out_shape=[
            jax.ShapeDtypeStruct((B * L, TM_WIDTH), BF16),
            jax.ShapeDtypeStruct(S0_all.shape, F32),
        ],
        scratch_shapes=[pltpu.VMEM((nb, SUBLANES + C, GDN_QKV), F32)],
        input_output_aliases=aliases,
        compiler_params=_cparams(("parallel", "arbitrary")),
        name="gdn",
    )(*args)
    return o, S


def _softmax_rows(s):
    e = jnp.exp(s - jnp.max(s, axis=-1, keepdims=True))
    return e / jnp.sum(e, axis=-1, keepdims=True)


def _xattn_kernel(q_ref, k_ref, v_ref, o_ref):
    scale = HEAD_DIM ** -0.5
    for h in range(XA_HEADS):
        cs = slice(h * HEAD_DIM, (h + 1) * HEAD_DIM)
        kh = k_ref[:, cs].astype(BF16)
        vh = v_ref[:, cs].astype(BF16)
        p = _softmax_rows(_dot_nt(q_ref[:, cs].astype(BF16), kh) * scale)
        o_ref[:, cs] = _dot(p.astype(BF16), vh).astype(o_ref.dtype)


def _xattn_cache_kernel(q_ref, k_ref, v_ref, o_ref, *, nb):
    scale = HEAD_DIM ** -0.5
    tq = q_ref.shape[0] // nb
    rows = XA_HEADS * tq
    cols = N_MEM * XA_HEADS
    own = ((lax.broadcasted_iota(jnp.int32, (rows, cols), 1) % XA_HEADS)
           == (lax.broadcasted_iota(jnp.int32, (rows, cols), 0) // tq))
    sc = []
    for s in range(nb):
        q = q_ref[s * tq:(s + 1) * tq, :]
        qs = jnp.concatenate([q[:, h * HEAD_DIM:(h + 1) * HEAD_DIM] for h in range(XA_HEADS)], axis=0)
        sc.append(_dot_nt(qs.astype(BF16), k_ref[s].astype(BF16)) * scale)
    pr = [_softmax_rows(jnp.where(own, x, NEG_INF)).astype(BF16) for x in sc]
    for s in range(nb):
        o = _dot(pr[s], v_ref[s].astype(BF16))
        for h in range(XA_HEADS):
            o_ref[s * tq:(s + 1) * tq, h * HEAD_DIM:(h + 1) * HEAD_DIM] = o[h * tq:(h + 1) * tq].astype(o_ref.dtype)


def xattn(q, k, v, *, B, Lq, tq, q_blk=0, k_blk=0, v_blk=0):
    nq = Lq // tq
    return pl.pallas_call(
        _xattn_kernel,
        grid=(B, nq),
        in_specs=[
            pl.BlockSpec((tq, XA_WIDTH), lambda b, n: (b * nq + n, q_blk)),
            pl.BlockSpec((N_MEM, XA_WIDTH), lambda b, n: (b, k_blk)),
            pl.BlockSpec((N_MEM, XA_WIDTH), lambda b, n: (b, v_blk)),
        ],
        out_specs=pl.BlockSpec((tq, XA_WIDTH), lambda b, n: (b * nq + n, 0)),
        out_shape=jax.ShapeDtypeStruct((B * Lq, XA_WIDTH), BF16),
        compiler_params=_cparams(("parallel", "arbitrary")),
        name="xattn",
    )(q, k, v)


def xattn_cache(q, k, v, *, layer, B, tq, nb):
    assert B % nb == 0
    boff = layer * (B // nb)
    spec = pl.BlockSpec((nb, N_MEM * XA_HEADS, HEAD_DIM), lambda b: (boff + b, 0, 0))
    return pl.pallas_call(
        functools.partial(_xattn_cache_kernel, nb=nb),
        grid=(B // nb,),
        in_specs=[pl.BlockSpec((nb * tq, XA_WIDTH), lambda b: (b, 0)), spec, spec],
        out_specs=pl.BlockSpec((nb * tq, XA_WIDTH), lambda b: (b, 0)),
        out_shape=jax.ShapeDtypeStruct((B * tq, XA_WIDTH), BF16),
        compiler_params=_cparams(("parallel",)),
        name="xattn_cache",
    )(q, k, v)


PEER_RANKS = PEER_TOPK + 1
GATE_ROWS = 32
PEER_TOPK_HEADS_PER_STEP = 2


def _cand_counts():
    return [PEER_RANKS // (a + 1) for a in range(PEER_RANKS)]


def _sorting_network(n):
    pairs = []
    t = 1
    while t < n:
        p = t
        while p >= 1:
            for j in range(p % t, n - p, 2 * p):
                for i in range(min(p, n - j - p)):
                    if (i + j) // (2 * t) == (i + j + p) // (2 * t):
                        pairs.append((i + j, i + j + p))
            p //= 2
        t *= 2
    return pairs


def _peer_topk_kernel(q_ref, sk_ref, pw_ref, aux_ref, sv_ref, cand_ref):
    for hh in range(pw_ref.shape[0]):
        _peer_topk_head(q_ref.at[:, hh * 2 * LANES:(hh + 1) * 2 * LANES], sk_ref.at[hh], pw_ref.at[hh],
                        aux_ref.at[hh], sv_ref.at[hh], cand_ref.at[hh])


def _peer_topk_head(q_ref, sk_ref, pw_ref, aux_ref, sv_ref, cand_ref):
    K = PEER_TOPK
    q = q_ref[...]
    s = []
    for p in range(2):
        qp = q[:, p * LANES:(p + 1) * LANES]
        s.append(_dot_nt(sk_ref[p], qp, precision=lax.Precision.HIGHEST))
    for p in range(2):
        v = [s[p][g * SUBLANES:(g + 1) * SUBLANES, :] for g in range(PEER_NKEYS // SUBLANES)]
        for (a, b) in _sorting_network(len(v)):
            v[a], v[b] = jnp.maximum(v[a], v[b]), jnp.minimum(v[a], v[b])
        v.append(jnp.full_like(v[0], NEG_INF))
        for r in range(PEER_RANKS):
            m = jnp.max(v[0], axis=0, keepdims=True)
            sv_ref[p, r:r + 1, :] = m
            took = v[0] == m
            for k in range(PEER_RANKS - 1 - r):
                v[k] = jnp.where(took, v[k + 1], v[k])
    sv1 = sv_ref[0, 0:PEER_RANKS, :]
    sv2 = sv_ref[1, 0:PEER_RANKS, :]
    cand_ref[...] = jnp.full(cand_ref.shape, NEG_INF, F32)
    off = 0
    for a, nb in enumerate(_cand_counts()):
        cand_ref[off:off + nb, :] = sv1[a:a + 1, :] + sv2[0:nb, :]
        off += nb
    cur = cand_ref[...]
    tops = []
    for r in range(PEER_RANKS):
        m = jnp.max(cur, axis=0, keepdims=True)
        tops.append(m)
        cur = jnp.where(cur == m, NEG_INF, cur)
    z = jnp.ones_like(tops[0])
    for r in range(1, K):
        z = z + jnp.exp(tops[r] - tops[0])
    thr = 0.5 * (tops[K - 1] + tops[K])
    pw_ref[0] = thr - s[0]
    pw_ref[1] = s[1]
    pw_ref[2] = jnp.exp(s[1] - sv2[0:1, :])
    aux_ref[...] = jnp.broadcast_to(thr - sv1[0:1, :] - jnp.log(z), aux_ref.shape)


def peer_topk(q, subkeys, *, tt=512):
    T = q.shape[0]
    tt = min(tt, T)
    assert T % tt == 0
    H, NK = PEER_HEADS, PEER_NKEYS
    pad8 = lambda n: -(-n // SUBLANES) * SUBLANES
    hps = PEER_TOPK_HEADS_PER_STEP
    assert H % hps == 0
    return pl.pallas_call(
        _peer_topk_kernel,
        grid=(T // tt, H // hps),
        in_specs=[
            pl.BlockSpec((tt, hps * 2 * LANES), lambda i, h: (i, h)),
            pl.BlockSpec((hps, 2, NK, LANES), lambda i, h: (h, 0, 0, 0)),
        ],
        out_specs=[
            pl.BlockSpec((hps, 3, NK, tt), lambda i, h: (h, 0, 0, i)),
            pl.BlockSpec((hps, SUBLANES, tt), lambda i, h: (h, 0, i)),
        ],
        out_shape=[
            jax.ShapeDtypeStruct((H, 3, NK, T), F32),
            jax.ShapeDtypeStruct((H, SUBLANES, T), F32),
        ],
        scratch_shapes=[pltpu.VMEM((hps, 2, pad8(PEER_RANKS), tt), F32),
                        pltpu.VMEM((hps, pad8(sum(_cand_counts())), tt), F32)],
        compiler_params=_cparams(("parallel", "arbitrary")),
        name="peer_topk",
    )(q, subkeys)


def _peer_dense_kernel(*refs, ib, emit_tables):
    first = 0 if emit_tables else 1
    offs_ref = None if emit_tables else refs[0]
    x_ref, g_ref, pw_ref, aux_ref, u_ref, v_ref, o_ref = refs[first:first + 7]
    rest = refs[first + 7:]
    xnT_ref, acc_ref, w_ref = rest[-3:]
    e = pl.program_id(1)
    H, NK = PEER_HEADS, PEER_NKEYS
    tt = x_ref.shape[0]
    strip = min(LANES, tt)
    assert tt % strip == 0

    def gates(blk, ii, dst):
        i = blk * ib + ii
        thr1 = [pw_ref[h, 0, pl.ds(i, 1), :] for h in range(H)]
        e1 = [0.5 * jnp.exp(aux_ref[h, 0:1, :] - thr1[h]) for h in range(H)]
        for tc in range(tt // strip):
            ts_ = slice(tc * strip, (tc + 1) * strip)
            for j0 in range(0, NK, GATE_ROWS):
                js = slice(j0, j0 + GATE_ROWS)
                w = None
                for h in range(H):
                    sel = pw_ref[h, 1, js, ts_] >= thr1[h][:, ts_]
                    c = jnp.where(sel, pw_ref[h, 2, js, ts_], 0.0) * e1[h][:, ts_]
                    w = c if w is None else w + c
                dst[ii * NK + j0:ii * NK + j0 + GATE_ROWS, ts_] = w
        last = jnp.sum(w[0:1, 0:1])
        return last != last

    @pl.when(e == 0)
    def _():
        xn = _rms(x_ref[...], g_ref[...])
        xnT_ref[...] = xn.T.astype(BF16)
        acc_ref[...] = jnp.zeros_like(acc_ref)

    v16 = v_ref[...].astype(BF16)
    if emit_tables:
        u16 = u_ref[...].astype(BF16)
        rest[0][...] = u16
        rest[1][...] = v16
        hT = _dot(u16, xnT_ref[...])
        for ii in range(ib):
            gates(e, ii, w_ref)
    else:
        hs = []
        for ii in range(ib):
            bad = gates(e, ii, w_ref)
            off = pl.multiple_of(jnp.where(bad, offs_ref[ii], ii * NK), NK)
            hs.append(_dot(u_ref[pl.ds(off, NK), :], xnT_ref[...]))
        hT = jnp.concatenate(hs, axis=0)
    aT = w_ref[...] * _gelu_tanh_x2(hT)
    acc_ref[...] += _dot(aT.T.astype(BF16), v16)

    @pl.when(e == pl.num_programs(1) - 1)
    def _():
        o_ref[...] = x_ref[...] + acc_ref[...]


def peer_dense(x, g, pw, aux, u_tab, v_tab, *, layer=0, emit_tables=False, tt=512, ib=None):
    T, D = x.shape
    if ib is None:
        ib = 4
    boff = layer * (PEER_NKEYS // ib)
    tt = min(tt, T)
    assert T % tt == 0 and PEER_NKEYS % ib == 0 and not (emit_tables and T != tt)
    H, NK = PEER_HEADS, PEER_NKEYS
    eb = ib * NK
    out_specs = [pl.BlockSpec((tt, D), lambda i, e, *_: (i, 0))]
    out_shape = [jax.ShapeDtypeStruct((T, D), F32)]
    args = (x, g.reshape(1, D), pw, aux, u_tab, v_tab)
    if emit_tables:
        out_specs += [pl.BlockSpec((eb, D), lambda i, e, *_: (e, 0))] * 2
        out_shape += [jax.ShapeDtypeStruct((NK * NK, D), BF16)] * 2
    else:
        args = (jnp.arange(ib, dtype=jnp.int32) * NK,) + args
    outs = pl.pallas_call(
        functools.partial(_peer_dense_kernel, ib=ib, emit_tables=emit_tables),
        grid_spec=pltpu.PrefetchScalarGridSpec(
            num_scalar_prefetch=0 if emit_tables else 1,
            grid=(T // tt, NK // ib),
            in_specs=[
                pl.BlockSpec((tt, D), lambda i, e, *_: (i, 0)),
                pl.BlockSpec((1, D), lambda i, e, *_: (0, 0)),
                pl.BlockSpec((H, 3, NK, tt), lambda i, e, *_: (0, 0, 0, i)),
                pl.BlockSpec((H, SUBLANES, tt), lambda i, e, *_: (0, 0, i)),
                pl.BlockSpec((eb, D), lambda i, e, *_: (boff + e, 0)),
                pl.BlockSpec((eb, D), lambda i, e, *_: (boff + e, 0)),
            ],
            out_specs=out_specs,
            scratch_shapes=[
                pltpu.VMEM((D, tt), BF16),
                pltpu.VMEM((tt, D), F32),
                pltpu.VMEM((eb, tt), F32),
            ],
        ),
        out_shape=out_shape,
        compiler_params=_cparams(("parallel", "arbitrary")),
        name="peer_dense",
    )(*args)
    return tuple(outs) if emit_tables else outs[0]


def _prep_weights(w_in_a, w_in_b, w_out, w_mem_kv, peer_wq, peer_u, peer_v, lru_wr, lru_wi):
    H = GDN_HEADS
    qkvz = GDN_QKV + TM_WIDTH
    wb = jnp.concatenate(
        [
            w_in_b[:, :, :qkvz],
            w_in_b[:, :, qkvz + 2 * H:],
            w_in_b[:, :, qkvz:qkvz + 2 * H],
            jnp.zeros(w_in_b.shape[:2] + (B_IN_PAD - w_in_b.shape[2],), w_in_b.dtype),
        ],
        axis=-1,
    )
    return dict(
        w_in_a=w_in_a.astype(BF16),
        w_in_b=wb.astype(BF16),
        w_out_tm=w_out[:, :TM_WIDTH].astype(BF16),
        w_out_xa=w_out[:, TM_WIDTH:].astype(BF16),
        w_mem_kv=w_mem_kv.astype(BF16),
        peer_wq=peer_wq.astype(BF16),
        peer_u=peer_u.reshape(-1, peer_u.shape[-1]),
        peer_v=peer_v.reshape(-1, peer_v.shape[-1]),
        lru_wr=lru_wr.astype(BF16),
        lru_wi=lru_wi.astype(BF16),
    )


def _trunk(x, mem, lru_h0, lru_c0, gdn_S0, gdn_c0, w, wc, *, B, L, time_major, peer_tabs16=None):
    if time_major:
        G, R, tl = 1, B, L
    else:
        G, R, tl = B, 1, min(L, 256)
    Lp = -(-L // SUBLANES) * SUBLANES
    hist = CONV_W - 1

    def to_batch_major(a):
        c = a.shape[-1]
        a = a.reshape(L, B, c).transpose(1, 0, 2)
        return jnp.pad(a, ((0, 0), (0, Lp - L), (0, 0))).reshape(B * Lp, c)

    def to_time_major(a):
        c = a.shape[-1]
        return a.reshape(B, Lp, c)[:, :L].transpose(1, 0, 2).reshape(L * B, c)

    def last_rows(u, c0, c1):
        if time_major:
            return u.reshape(L, B, -1)[L - hist:, :, c0:c1].transpose(1, 0, 2)
        return u.reshape(B, L, -1)[:, L - hist:, c0:c1]

    n_b = gdn_S0.shape[0]
    S0_all = gdn_S0.reshape((n_b * B,) + gdn_S0.shape[2:])
    S_all = jnp.zeros(S0_all.shape, F32)
    lru_h, lru_c, gdn_c, tabs16 = [], [], [], []
    for l in range(DEPTH):
        j = l // 2
        if l % 2 == 0:
            u, qmem = norm_matmul(x, w['norm_mix'][l], wc['w_in_a'][j], side_cols=(2 * TM_WIDTH, XA_WIDTH))
            c0 = lru_c0[j]
            if time_major:
                c0k = c0.transpose(1, 0, 2).reshape(1, hist * B, TM_WIDTH)
                h0k = lru_h0[j].reshape(1, B, TM_WIDTH)
            else:
                c0k = c0
                h0k = lru_h0[j].reshape(B, 1, TM_WIDTH)
            tm, h = rglru(u, c0k, h0k, w['lru_conv_w'][j], w['lru_conv_b'][j], wc['lru_wr'][j], w['lru_br'][j],
                          wc['lru_wi'][j], w['lru_bi'][j], w['lru_lambda'][j], G=G, L=L, R=R, tl=tl)
            lru_h.append(h.reshape(B, TM_WIDTH))
            lru_c.append(last_rows(u, TM_WIDTH, 2 * TM_WIDTH))
        else:
            u, qmem = norm_matmul(x, w['norm_mix'][l], wc['w_in_b'][j], side_cols=(GDN_QKV + TM_WIDTH, XA_WIDTH))
            C = min(GDN_CHUNK, Lp)
            nb = GDN_SEQS_PER_STEP if (Lp == C and B % GDN_SEQS_PER_STEP == 0) else 1
            u_bm = to_batch_major(u) if time_major else u
            o_bm, S_all = gdn(u_bm, gdn_c0[j], S0_all, S_all, w['gdn_conv_w'][j], w['gdn_conv_b'][j],
                              w['gdn_a_log'][j], w['gdn_dt_bias'][j], w['gdn_o_norm'][j], layer=j, B=B, L=Lp, C=C,
                              n_valid=min(L, C), nb=nb, qkv_blk=0, z_blk=GDN_QKV // TM_WIDTH,
                              ba_blk=(GDN_QKV + TM_WIDTH + XA_WIDTH) // LANES)
            tm = to_time_major(o_bm) if time_major else o_bm
            gdn_c.append(last_rows(u, 0, GDN_QKV))
        if time_major:
            q_bm = to_batch_major(qmem)
            xa = to_time_major(xattn_cache(q_bm, mem[0], mem[1], layer=l, B=B, tq=Lp,
                                           nb=XATTN_SEQS_PER_STEP if B % XATTN_SEQS_PER_STEP == 0 else 1))
        else:
            xa = xattn(qmem, mem[l], mem[l], B=B, Lq=L, tq=min(L, 512), q_blk=0, k_blk=0, v_blk=1)
        x = out_proj(tm, xa, wc['w_out_tm'][l], wc['w_out_xa'][l], x)
        q = norm_matmul(x, w['norm_ffn'][l], wc['peer_wq'][l])
        pw, aux = peer_topk(q, w['peer_subkeys'][l])
        if peer_tabs16 is None:
            x, u16, v16 = peer_dense(x, w['norm_ffn'][l], pw, aux, wc['peer_u'], wc['peer_v'], layer=l,
                                     emit_tables=True)
            tabs16.append((u16, v16))
        else:
            x = peer_dense(x, w['norm_ffn'][l], pw, aux, peer_tabs16[l][0], peer_tabs16[l][1])
    y = rmsnorm_rows(x, w['norm_final'])
    return (y, jnp.stack(lru_h), jnp.stack(lru_c), S_all.reshape(gdn_S0.shape), jnp.stack(gdn_c)), tabs16


def kernel(x_prompt, x_sample, state_rglru_h, state_rglru_conv, state_gdn_S, state_gdn_conv, cache_mem_k, cache_mem_v, mem_prompt, norm_mix, norm_ffn, norm_final, w_in_a, w_in_b, w_out, lru_conv_w, lru_conv_b, lru_wr, lru_br, lru_wi, lru_bi, lru_lambda, gdn_conv_w, gdn_conv_b, gdn_a_log, gdn_dt_bias, gdn_o_norm, norm_mem, w_mem_kv, peer_wq, peer_subkeys, peer_u, peer_v):
    w = dict(norm_mix=norm_mix, norm_ffn=norm_ffn, norm_final=norm_final, lru_conv_w=lru_conv_w,
             lru_conv_b=lru_conv_b, lru_br=lru_br, lru_bi=lru_bi, lru_lambda=lru_lambda, gdn_conv_w=gdn_conv_w,
             gdn_conv_b=gdn_conv_b, gdn_a_log=gdn_a_log, gdn_dt_bias=gdn_dt_bias, gdn_o_norm=gdn_o_norm,
             peer_subkeys=peer_subkeys)
    wc = _prep_weights(w_in_a, w_in_b, w_out, w_mem_kv, peer_wq, peer_u, peer_v, lru_wr, lru_wi)
    Bp, Lp_, D = x_prompt.shape
    Bs, Ls, _ = x_sample.shape
    n_a, n_b = state_rglru_h.shape[0], state_gdn_S.shape[0]

    xs = x_sample.transpose(1, 0, 2).reshape(Ls * Bs, D)
    rows_hd = (DEPTH * Bs, N_MEM * XA_HEADS, HEAD_DIM)
    (y_s, s_h, s_ca, s_S, s_cb), tabs16 = _trunk(
        xs, (cache_mem_k.reshape(rows_hd), cache_mem_v.reshape(rows_hd)), state_rglru_h, state_rglru_conv,
        state_gdn_S, state_gdn_conv, w, wc, B=Bs, L=Ls, time_major=True)
    y_s = y_s.reshape(Ls, Bs, D).transpose(1, 0, 2)

    mem2 = mem_prompt.reshape(Bp * N_MEM, D)
    kv = [norm_matmul(mem2, norm_mem[l], wc['w_mem_kv'][l]) for l in range(DEPTH)]
    (y_p, p_h, p_ca, p_S, p_cb), _ = _trunk(
        x_prompt.reshape(Bp * Lp_, D), kv,
        jnp.zeros((n_a, Bp, TM_WIDTH), F32), jnp.zeros((n_a, Bp, CONV_W - 1, TM_WIDTH), F32),
        jnp.zeros((n_b, Bp, GDN_HEADS, HEAD_DIM, HEAD_DIM), F32), jnp.zeros((n_b, Bp, CONV_W - 1, GDN_QKV), F32),
        w, wc, B=Bp, L=Lp_, time_major=False, peer_tabs16=tabs16)
    p_mem_k = jnp.stack([a[:, :XA_WIDTH].reshape(Bp, N_MEM, XA_HEADS, HEAD_DIM) for a in kv])
    p_mem_v = jnp.stack([a[:, XA_WIDTH:].reshape(Bp, N_MEM, XA_HEADS, HEAD_DIM) for a in kv])
    return (y_p.reshape(Bp, Lp_, D), y_s, p_h, p_ca, p_S, p_cb, p_mem_k, p_mem_v, s_h, s_ca, s_S, s_cb)
```

```python
import functools
import math

import jax
import jax.numpy as jnp
from jax import lax
from jax.experimental import pallas as pl
from jax.experimental.pallas import tpu as pltpu

F32 = jnp.float32
BF16 = jnp.bfloat16

D_MODEL = 2048
DEPTH = 4
HEAD_DIM = 128
XA_HEADS = 4
XA_WIDTH = XA_HEADS * HEAD_DIM
TM_WIDTH = D_MODEL - XA_WIDTH
N_MEM = 256
CONV_W = 4
EPS = 1e-6
LRU_BLOCKS = TM_WIDTH // HEAD_DIM
LRU_C = 8.0
GDN_HEADS = TM_WIDTH // HEAD_DIM
GDN_QKV = 3 * TM_WIDTH
GDN_CHUNK = 64
GDN_SEQS_PER_STEP = 4
XATTN_SEQS_PER_STEP = 4
PEER_HEADS = 8
PEER_NKEYS = 128
PEER_TOPK = 16
B_IN_PAD = 7168

SUBLANES = 8
LANES = 128
VMEM_LIMIT = 52 * 1024 * 1024

NEG_INF = float("-inf")


def _cparams(sem):
    return pltpu.CompilerParams(dimension_semantics=sem, vmem_limit_bytes=VMEM_LIMIT)


def _dot(a, b, precision=None):
    return jnp.dot(a, b, preferred_element_type=F32, precision=precision)


def _mm(a, b):
    return jnp.dot(a.astype(BF16), b.astype(BF16), preferred_element_type=F32)


def _split_bf16(a):
    hi = a.astype(BF16)
    return hi, (a - hi.astype(F32)).astype(BF16)


def _dot3(a, b):
    ah, al = _split_bf16(a)
    bh, bl = _split_bf16(b)
    return _dot(ah, bh) + (_dot(ah, bl) + _dot(al, bh))


def _dot_nt(a, b, precision=None):
    return lax.dot_general(a, b, (((1,), (1,)), ((), ())), preferred_element_type=F32, precision=precision)


def _dot_tn(a, b, precision=None):
    return lax.dot_general(a, b, (((0,), (0,)), ((), ())), preferred_element_type=F32, precision=precision)


def _sigmoid(x):
    return 1.0 / (1.0 + jnp.exp(-x))


def _silu(x):
    return x * _sigmoid(x)


def _gelu_tanh(x):
    c = math.sqrt(2.0 / math.pi)
    return 0.5 * x * (1.0 + jnp.tanh(c * (x + 0.044715 * (x * x * x))))


def _gelu_tanh_x2(x):
    c = math.sqrt(2.0 / math.pi)
    return x * (1.0 + jnp.tanh(x * (c + (0.044715 * c) * (x * x))))


def _softplus(x):
    return jnp.maximum(x, 0.0) + jnp.log(1.0 + jnp.exp(-jnp.abs(x)))


def _rms(x, g):
    ms = jnp.mean(x * x, axis=-1, keepdims=True)
    return x * lax.rsqrt(ms + EPS) * g


def _norm_matmul_kernel(x_ref, g_ref, w_ref, o_ref, *rest, side):
    xn_ref = rest[-1]

    @pl.when(pl.program_id(1) == 0)
    def _():
        xn_ref[...] = _rms(x_ref[...], g_ref[...]).astype(BF16)

    acc = _dot(xn_ref[...], w_ref[...])
    o_ref[...] = acc
    if side is not None:
        tile, off, width = side

        @pl.when(pl.program_id(1) == tile)
        def _():
            rest[0][...] = acc[:, off:off + width]


def _matmul_tiles(m, n, tm, tn):
    tm = min(tm, m)
    while m % tm:
        tm //= 2
    while n % tn:
        tn //= 2
    return tm, tn


def norm_matmul(x, g, w, *, side_cols=None, tm=1024, tn=1024):
    m, k = x.shape
    n = w.shape[1]
    tm, tn = _matmul_tiles(m, n, tm, tn)
    assert tm % SUBLANES == 0 and tn % LANES == 0
    out_specs = [pl.BlockSpec((tm, tn), lambda i, j: (i, j))]
    out_shape = [jax.ShapeDtypeStruct((m, n), F32)]
    side = None
    if side_cols is not None:
        start, width = side_cols
        side = (start // tn, start % tn, width)
        assert start % tn + width <= tn
        out_specs.append(pl.BlockSpec((tm, width), lambda i, j: (i, 0)))
        out_shape.append(jax.ShapeDtypeStruct((m, width), F32))
    outs = pl.pallas_call(
        functools.partial(_norm_matmul_kernel, side=side),
        grid=(m // tm, n // tn),
        in_specs=[
            pl.BlockSpec((tm, k), lambda i, j: (i, 0)),
            pl.BlockSpec((1, k), lambda i, j: (0, 0)),
            pl.BlockSpec((k, tn), lambda i, j: (0, j)),
        ],
        out_specs=out_specs,
        out_shape=out_shape,
        scratch_shapes=[pltpu.VMEM((tm, k), BF16)],
        compiler_params=_cparams(("parallel", "arbitrary")),
        name="norm_matmul",
    )(x, g.reshape(1, k), w)
    return outs[0] if side_cols is None else tuple(outs)


def _out_proj_kernel(a_ref, b_ref, wa_ref, wb_ref, r_ref, o_ref):
    acc = _dot(a_ref[...], wa_ref[...]) + _dot(b_ref[...], wb_ref[...])
    o_ref[...] = r_ref[...] + acc


def out_proj(a, b, wa, wb, r, *, tm=1024, tn=1024):
    m, ka = a.shape
    kb = b.shape[1]
    n = wa.shape[1]
    tm, tn = _matmul_tiles(m, n, tm, tn)
    assert tm % SUBLANES == 0 and tn % LANES == 0
    return pl.pallas_call(
        _out_proj_kernel,
        grid=(m // tm, n // tn),
        in_specs=[
            pl.BlockSpec((tm, ka), lambda i, j: (i, 0)),
            pl.BlockSpec((tm, kb), lambda i, j: (i, 0)),
            pl.BlockSpec((ka, tn), lambda i, j: (0, j)),
            pl.BlockSpec((kb, tn), lambda i, j: (0, j)),
            pl.BlockSpec((tm, tn), lambda i, j: (i, j)),
        ],
        out_specs=pl.BlockSpec((tm, tn), lambda i, j: (i, j)),
        out_shape=jax.ShapeDtypeStruct((m, n), F32),
        compiler_params=_cparams(("parallel", "arbitrary")),
        name="out_proj",
    )(a, b, wa, wb, r)


def _rmsnorm_kernel(x_ref, g_ref, o_ref):
    o_ref[...] = _rms(x_ref[...], g_ref[...])


def rmsnorm_rows(x, g, *, tm=512):
    m, k = x.shape
    tm = min(tm, m)
    assert m % tm == 0
    return pl.pallas_call(
        _rmsnorm_kernel,
        grid=(m // tm,),
        in_specs=[pl.BlockSpec((tm, k), lambda i: (i, 0)), pl.BlockSpec((1, k), lambda i: (0, 0))],
        out_specs=pl.BlockSpec((tm, k), lambda i: (i, 0)),
        out_shape=jax.ShapeDtypeStruct((m, k), F32),
        compiler_params=_cparams(("parallel",)),
        name="final_rmsnorm",
    )(x, g.reshape(1, k))


def _rglru_kernel(gate_ref, xr_ref, cw_ref, cb_ref, c0_ref, h0_ref, wr_ref, br_ref, wi_ref, bi_ref, lam_ref,
                  y_ref, hout_ref, hist_ref, a_ref, b_ref, hs_ref, h_ref, *, tl, R):
    n = pl.program_id(1)
    rows = tl * R
    hist_rows = (CONV_W - 1) * R
    off0 = -(-hist_rows // SUBLANES) * SUBLANES

    @pl.when(n == 0)
    def _():
        hist_ref[off0 - hist_rows:off0, :] = c0_ref[0]
        h_ref[...] = h0_ref[0]

    x = xr_ref[...]
    hist_ref[off0:off0 + rows, :] = x
    xc = cb_ref[...]
    for j in range(CONV_W):
        s = off0 - hist_rows + j * R
        xc = xc + hist_ref[s:s + rows, :] * cw_ref[j:j + 1, :]
    tail = hist_ref[off0 + rows - hist_rows:off0 + rows, :]
    hist_ref[off0 - hist_rows:off0, :] = tail

    sp = _softplus(-lam_ref[...])
    for blk in range(LRU_BLOCKS):
        cs = slice(blk * HEAD_DIM, (blk + 1) * HEAD_DIM)
        xb = xc[:, cs]
        xb16 = xb.astype(BF16)
        r = _sigmoid(_dot(xb16, wr_ref[blk]) + br_ref[blk])
        i = _sigmoid(_dot(xb16, wi_ref[blk]) + bi_ref[blk])
        log_a = (-LRU_C) * r * sp[:, cs]
        a = jnp.exp(log_a)
        a_ref[:, cs] = a
        b_ref[:, cs] = jnp.sqrt(1.0 - a * a) * (i * xb)

    if R % SUBLANES == 0:
        def step(l, carry):
            rs = pl.ds(pl.multiple_of(l * R, SUBLANES), R)
            h = a_ref[rs, :] * h_ref[...] + b_ref[rs, :]
            h_ref[...] = h
            hs_ref[rs, :] = h
            return carry

        lax.fori_loop(0, tl, step, 0)
    else:
        def step(l, h):
            rs = pl.ds(l * R, R)
            h = a_ref[rs, :] * h + b_ref[rs, :]
            hs_ref[rs, :] = h
            return h

        h_ref[...] = lax.fori_loop(0, tl, step, h_ref[...], unroll=SUBLANES)
    y_ref[...] = (hs_ref[...] * _gelu_tanh(gate_ref[...])).astype(y_ref.dtype)
    hout_ref[0] = h_ref[...]


def rglru(u, c0, h0, cw, cb, wr, br, wi, bi, lam, *, G, L, R, tl):
    W = TM_WIDTH
    assert L % tl == 0
    rows = tl * R
    nt = L // tl
    hist_rows = (CONV_W - 1) * R
    off0 = -(-hist_rows // SUBLANES) * SUBLANES
    kern = functools.partial(_rglru_kernel, tl=tl, R=R)
    const2 = lambda g, n: (0, 0)
    const3 = lambda g, n: (0, 0, 0)
    y, hout = pl.pallas_call(
        kern,
        grid=(G, nt),
        in_specs=[
            pl.BlockSpec((rows, W), lambda g, n: (g * nt + n, 0)),
            pl.BlockSpec((rows, W), lambda g, n: (g * nt + n, 1)),
            pl.BlockSpec((CONV_W, W), const2),
            pl.BlockSpec((1, W), const2),
            pl.BlockSpec((1, hist_rows, W), lambda g, n: (g, 0, 0)),
            pl.BlockSpec((1, R, W), lambda g, n: (g, 0, 0)),
            pl.BlockSpec((LRU_BLOCKS, HEAD_DIM, HEAD_DIM), const3),
            pl.BlockSpec((LRU_BLOCKS, 1, HEAD_DIM), const3),
            pl.BlockSpec((LRU_BLOCKS, HEAD_DIM, HEAD_DIM), const3),
            pl.BlockSpec((LRU_BLOCKS, 1, HEAD_DIM), const3),
            pl.BlockSpec((1, W), const2),
        ],
        out_specs=[
            pl.BlockSpec((rows, W), lambda g, n: (g * nt + n, 0)),
            pl.BlockSpec((1, R, W), lambda g, n: (g, 0, 0)),
        ],
        out_shape=[
            jax.ShapeDtypeStruct((G * L * R, W), BF16),
            jax.ShapeDtypeStruct((G, R, W), F32),
        ],
        scratch_shapes=[
            pltpu.VMEM((off0 + rows, W), F32),
            pltpu.VMEM((rows, W), F32),
            pltpu.VMEM((rows, W), F32),
            pltpu.VMEM((rows, W), F32),
            pltpu.VMEM((R, W), F32),
        ],
        compiler_params=_cparams(("parallel", "arbitrary")),
        name="rglru",
    )(u, u, cw, cb.reshape(1, W), c0, h0, wr, br.reshape(LRU_BLOCKS, 1, HEAD_DIM), wi,
      bi.reshape(LRU_BLOCKS, 1, HEAD_DIM), lam.reshape(1, W))
    return y, hout


def _inv_unit_lower_many(As, C, row, col):
    eye = (row == col).astype(F32)
    base = min(SUBLANES, C)
    sh = int(math.log2(base))
    dmask = (row >> sh) == (col >> sh)
    Ps = [jnp.where(dmask, A, 0.0) for A in As]
    Ts = [eye - P for P in Ps]
    span = 2
    while span < base:
        Ps = [_dot3(P, P) for P in Ps]
        Ts = [_dot3(T, eye + P) for T, P in zip(Ts, Ps)]
        span *= 2
    s = base
    while s < C:
        sh = int(math.log2(s))
        off = ((row >> (sh + 1)) == (col >> (sh + 1))) & ((row >> sh) != (col >> sh))
        Ms = [_dot3(jnp.where(off, A, 0.0), T) for A, T in zip(As, Ts)]
        Ts = [T - _dot3(T, M) for T, M in zip(Ts, Ms)]
        s *= 2
    return Ts


def _gdn_kernel(qkv_ref, z_ref, ba_ref, cw_ref, cb_ref, c0_ref, S0_ref, alog_ref, dtb_ref, onorm_ref,
                *rest, C, n_valid, nb):
    o_ref, S_ref, hist_ref = rest[-3:]
    n = pl.program_id(1)
    H, Dh = GDN_HEADS, HEAD_DIM
    hist_rows = CONV_W - 1
    off0 = SUBLANES
    rows = nb * C

    @pl.when(n == 0)
    def _():
        for s in range(nb):
            hist_ref[s, off0 - hist_rows:off0, :] = c0_ref[s]
        S_ref[...] = S0_ref[...]

    ts = []
    for s in range(nb):
        hist_ref[s, off0:off0 + C, :] = qkv_ref[s * C:(s + 1) * C, :]
        xc = cb_ref[...]
        for j in range(CONV_W):
            r0 = off0 - hist_rows + j
            xc = xc + hist_ref[s, r0:r0 + C, :] * cw_ref[j:j + 1, :]
        tail = hist_ref[s, off0 + C - hist_rows:off0 + C, :]
        hist_ref[s, off0 - hist_rows:off0, :] = tail
        ts.append(_silu(xc))

    ba = ba_ref[...]
    beta_all = _sigmoid(ba)
    g_all = -jnp.exp(alog_ref[...]) * _softplus(ba + dtb_ref[...])
    if n_valid < C:
        rmask = (lax.broadcasted_iota(jnp.int32, (rows, LANES), 0) & (C - 1)) < n_valid
        beta_all = jnp.where(rmask, beta_all, 0.0)
        g_all = jnp.where(rmask, g_all, 0.0)
    if rows < LANES:
        g_pad = jnp.concatenate([g_all, jnp.zeros((LANES - rows, LANES), F32)], axis=0)
    else:
        g_pad = g_all
    r128 = lax.broadcasted_iota(jnp.int32, (LANES, LANES), 0)
    c128 = lax.broadcasted_iota(jnp.int32, (LANES, LANES), 1)
    shc = int(math.log2(C))
    tril = ((c128 <= r128) & ((c128 >> shc) == (r128 >> shc))).astype(F32)
    gc_pad = _dot(tril, g_pad, precision=lax.Precision.HIGHEST)
    gcT = gc_pad.T

    row = lax.broadcasted_iota(jnp.int32, (C, C), 0)
    col = lax.broadcasted_iota(jnp.int32, (C, C), 1)
    incl = col <= row
    strict = col < row

    probs = [(s, h) for s in range(nb) for h in range(H)]
    qn, kn, kn16, vh, bcol, gcol, decay, egc = {}, {}, {}, {}, {}, {}, {}, {}
    for p in probs:
        s, h = p
        t = ts[s]
        qh = t[:, h * Dh:(h + 1) * Dh]
        kh = t[:, TM_WIDTH + h * Dh:TM_WIDTH + (h + 1) * Dh]
        vh[p] = t[:, 2 * TM_WIDTH + h * Dh:2 * TM_WIDTH + (h + 1) * Dh]
        qn[p] = qh * lax.rsqrt(jnp.sum(qh * qh, -1, keepdims=True) + EPS) * (Dh ** -0.5)
        kn[p] = kh * lax.rsqrt(jnp.sum(kh * kh, -1, keepdims=True) + EPS)
        kn16[p] = kn[p].astype(BF16)
        bcol[p] = beta_all[s * C:(s + 1) * C, h:h + 1]
        gcol[p] = gc_pad[s * C:(s + 1) * C, H + h:H + h + 1]
        grow = gcT[H + h:H + h + 1, s * C:(s + 1) * C]
        diff = gcol[p] - grow
        decay[p] = jnp.where(incl, jnp.exp(jnp.where(incl, diff, 0.0)), 0.0)
        egc[p] = jnp.exp(gcol[p])
    kk = {p: _dot_nt(kn16[p], kn16[p]) for p in probs}
    qkr = {p: _dot_nt(qn[p].astype(BF16), kn16[p]) for p in probs}
    As = [jnp.where(strict, bcol[p] * kk[p] * decay[p], 0.0) for p in probs]
    Ts = _inv_unit_lower_many(As, C, row, col)
    rhs = [jnp.concatenate([bcol[p] * vh[p], (bcol[p] * egc[p]) * kn[p]], axis=1) for p in probs]
    sol = {p: _dot3(T, r) for p, T, r in zip(probs, Ts, rhs)}
    S = {p: S_ref[p[0], p[1]] for p in probs}
    S16 = {p: S[p].astype(BF16) for p in probs}
    kS = {p: _dot(sol[p][:, Dh:].astype(BF16), S16[p]) for p in probs}
    qS = {p: _dot((qn[p] * egc[p]).astype(BF16), S16[p]) for p in probs}
    u_new = {p: (sol[p][:, :Dh] - kS[p]).astype(BF16) for p in probs}
    qk16 = {p: jnp.where(incl, qkr[p] * decay[p], 0.0).astype(BF16) for p in probs}
    o = {p: qS[p] + _dot(qk16[p], u_new[p]) for p in probs}
    dS = {}
    for p in probs:
        g_last = gcol[p][C - 1:C, :]
        kdec = (kn[p] * jnp.exp(g_last - gcol[p])).astype(BF16)
        dS[p] = _dot_tn(kdec, u_new[p])
    for p in probs:
        s, h = p
        g_last = gcol[p][C - 1:C, :]
        S_ref[s, h] = S[p] * jnp.exp(g_last) + dS[p]
        on = o[p] * lax.rsqrt(jnp.mean(o[p] * o[p], -1, keepdims=True) + EPS) * onorm_ref[...]
        zs = z_ref[s * C:(s + 1) * C, h * Dh:(h + 1) * Dh]
        o_ref[s * C:(s + 1) * C, h * Dh:(h + 1) * Dh] = (on * _silu(zs)).astype(o_ref.dtype)


def gdn(u, c0, S0_all, S_out_prev, cw, cb, a_log, dt_bias, o_norm, *, layer, B, L, C, n_valid, nb, qkv_blk, z_blk,
        ba_blk):
    H, Dh = GDN_HEADS, HEAD_DIM
    nc = L // C
    assert B % nb == 0 and (nb == 1 or nc == 1) and nb * C <= LANES
    rows = nb * C
    soff = layer * (B // nb)
    alog = jnp.zeros((1, LANES), F32).at[0, H:2 * H].set(a_log)
    dtb = jnp.zeros((1, LANES), F32).at[0, H:2 * H].set(dt_bias)
    kern = functools.partial(_gdn_kernel, C=C, n_valid=n_valid, nb=nb)
    const2 = lambda b, n: (0, 0)
    in_specs = [
        pl.BlockSpec((rows, GDN_QKV), lambda b, n: (b * nc + n, qkv_blk)),
        pl.BlockSpec((rows, TM_WIDTH), lambda b, n: (b * nc + n, z_blk)),
        pl.BlockSpec((rows, LANES), lambda b, n: (b * nc + n, ba_blk)),
        pl.BlockSpec((CONV_W, GDN_QKV), const2),
        pl.BlockSpec((1, GDN_QKV), const2),
        pl.BlockSpec((nb, CONV_W - 1, GDN_QKV), lambda b, n: (b, 0, 0)),
        pl.BlockSpec((nb, H, Dh, Dh), lambda b, n: (soff + b, 0, 0, 0)),
        pl.BlockSpec((1, LANES), const2),
        pl.BlockSpec((1, LANES), const2),
        pl.BlockSpec((1, Dh), const2),
    ]
    args = [u, u, u, cw, cb.reshape(1, GDN_QKV), c0, S0_all, alog, dtb, o_norm.reshape(1, Dh)]
    aliases = {}
    if S_out_prev is not None:
        in_specs.append(pl.BlockSpec(memory_space=pl.ANY))
        args.append(S_out_prev)
        aliases = {len(args) - 1: 1}
    o, S = pl.pallas_call(
        kern,
        grid=(B // nb, nc),
        in_specs=in_specs,
        out_specs=[
            pl.BlockSpec((rows, TM_WIDTH), lambda b, n: (b * nc + n, 0)),
            pl.BlockSpec((nb, H, Dh, Dh), lambda b, n: (soff + b, 0, 0, 0)),
        ],
        out_shape=[
            jax.ShapeDtypeStruct((B * L, TM_WIDTH), BF16),
            jax.ShapeDtypeStruct(S0_all.shape, F32),
        ],
        scratch_shapes=[pltpu.VMEM((nb, SUBLANES + C, GDN_QKV), F32)],
        input_output_aliases=aliases,
        compiler_params=_cparams(("parallel", "arbitrary")),
        name="gdn",
    )(*args)
    return o, S


def _softmax_rows(s):
    e = jnp.exp(s - jnp.max(s, axis=-1, keepdims=True))
    return e / jnp.sum(e, axis=-1, keepdims=True)


def _xattn_kernel(q_ref, k_ref, v_ref, o_ref):
    scale = HEAD_DIM ** -0.5
    for h in range(XA_HEADS):
        cs = slice(h * HEAD_DIM, (h + 1) * HEAD_DIM)
        kh = k_ref[:, cs].astype(BF16)
        vh = v_ref[:, cs].astype(BF16)
        p = _softmax_rows(_dot_nt(q_ref[:, cs].astype(BF16), kh) * scale)
        o_ref[:, cs] = _dot(p.astype(BF16), vh).astype(o_ref.dtype)


def _xattn_cache_kernel(q_ref, k_ref, v_ref, o_ref, *, nb):
    scale = HEAD_DIM ** -0.5
    tq = q_ref.shape[0] // nb
    rows = XA_HEADS * tq
    cols = N_MEM * XA_HEADS
    own = ((lax.broadcasted_iota(jnp.int32, (rows, cols), 1) % XA_HEADS)
           == (lax.broadcasted_iota(jnp.int32, (rows, cols), 0) // tq))
    sc = []
    for s in range(nb):
        q = q_ref[s * tq:(s + 1) * tq, :]
        qs = jnp.concatenate([q[:, h * HEAD_DIM:(h + 1) * HEAD_DIM] for h in range(XA_HEADS)], axis=0)
        sc.append(_dot_nt(qs.astype(BF16), k_ref[s].astype(BF16)) * scale)
    pr = [_softmax_rows(jnp.where(own, x, NEG_INF)).astype(BF16) for x in sc]
    for s in range(nb):
        o = _dot(pr[s], v_ref[s].astype(BF16))
        for h in range(XA_HEADS):
            o_ref[s * tq:(s + 1) * tq, h * HEAD_DIM:(h + 1) * HEAD_DIM] = o[h * tq:(h + 1) * tq].astype(o_ref.dtype)


def xattn(q, k, v, *, B, Lq, tq, q_blk=0, k_blk=0, v_blk=0):
    nq = Lq // tq
    return pl.pallas_call(
        _xattn_kernel,
        grid=(B, nq),
        in_specs=[
            pl.BlockSpec((tq, XA_WIDTH), lambda b, n: (b * nq + n, q_blk)),
            pl.BlockSpec((N_MEM, XA_WIDTH), lambda b, n: (b, k_blk)),
            pl.BlockSpec((N_MEM, XA_WIDTH), lambda b, n: (b, v_blk)),
        ],
        out_specs=pl.BlockSpec((tq, XA_WIDTH), lambda b, n: (b * nq + n, 0)),
        out_shape=jax.ShapeDtypeStruct((B * Lq, XA_WIDTH), BF16),
        compiler_params=_cparams(("parallel", "arbitrary")),
        name="xattn",
    )(q, k, v)


def xattn_cache(q, k, v, *, layer, B, tq, nb):
    assert B % nb == 0
    boff = layer * (B // nb)
    spec = pl.BlockSpec((nb, N_MEM * XA_HEADS, HEAD_DIM), lambda b: (boff + b, 0, 0))
    return pl.pallas_call(
        functools.partial(_xattn_cache_kernel, nb=nb),
        grid=(B // nb,),
        in_specs=[pl.BlockSpec((nb * tq, XA_WIDTH), lambda b: (b, 0)), spec, spec],
        out_specs=pl.BlockSpec((nb * tq, XA_WIDTH), lambda b: (b, 0)),
        out_shape=jax.ShapeDtypeStruct((B * tq, XA_WIDTH), BF16),
        compiler_params=_cparams(("parallel",)),
        name="xattn_cache",
    )(q, k, v)


PEER_RANKS = PEER_TOPK + 1
GATE_ROWS = 32
PEER_TOPK_HEADS_PER_STEP = 2


def _cand_counts():
    return [PEER_RANKS // (a + 1) for a in range(PEER_RANKS)]


def _sorting_network(n):
    pairs = []
    t = 1
    while t < n:
        p = t
        while p >= 1:
            for j in range(p % t, n - p, 2 * p):
                for i in range(min(p, n - j - p)):
                    if (i + j) // (2 * t) == (i + j + p) // (2 * t):
                        pairs.append((i + j, i + j + p))
            p //= 2
        t *= 2
    return pairs


def _peer_topk_kernel(q_ref, sk_ref, pw_ref, aux_ref, sv_ref, cand_ref):
    for hh in range(pw_ref.shape[0]):
        _peer_topk_head(q_ref.at[:, hh * 2 * LANES:(hh + 1) * 2 * LANES], sk_ref.at[hh], pw_ref.at[hh],
                        aux_ref.at[hh], sv_ref.at[hh], cand_ref.at[hh])


def _peer_topk_head(q_ref, sk_ref, pw_ref, aux_ref, sv_ref, cand_ref):
    K = PEER_TOPK
    q = q_ref[...]
    s = []
    for p in range(2):
        qp = q[:, p * LANES:(p + 1) * LANES]
        s.append(_dot_nt(sk_ref[p], qp, precision=lax.Precision.HIGHEST))
    for p in range(2):
        v = [s[p][g * SUBLANES:(g + 1) * SUBLANES, :] for g in range(PEER_NKEYS // SUBLANES)]
        for (a, b) in _sorting_network(len(v)):
            v[a], v[b] = jnp.maximum(v[a], v[b]), jnp.minimum(v[a], v[b])
        v.append(jnp.full_like(v[0], NEG_INF))
        for r in range(PEER_RANKS):
            m = jnp.max(v[0], axis=0, keepdims=True)
            sv_ref[p, r:r + 1, :] = m
            took = v[0] == m
            for k in range(PEER_RANKS - 1 - r):
                v[k] = jnp.where(took, v[k + 1], v[k])
    sv1 = sv_ref[0, 0:PEER_RANKS, :]
    sv2 = sv_ref[1, 0:PEER_RANKS, :]
    cand_ref[...] = jnp.full(cand_ref.shape, NEG_INF, F32)
    off = 0
    for a, nb in enumerate(_cand_counts()):
        cand_ref[off:off + nb, :] = sv1[a:a + 1, :] + sv2[0:nb, :]
        off += nb
    cur = cand_ref[...]
    tops = []
    for r in range(PEER_RANKS):
        m = jnp.max(cur, axis=0, keepdims=True)
        tops.append(m)
        cur = jnp.where(cur == m, NEG_INF, cur)
    z = jnp.ones_like(tops[0])
    for r in range(1, K):
        z = z + jnp.exp(tops[r] - tops[0])
    thr = 0.5 * (tops[K - 1] + tops[K])
    pw_ref[0] = thr - s[0]
    pw_ref[1] = s[1]
    pw_ref[2] = jnp.exp(s[1] - sv2[0:1, :])
    aux_ref[...] = jnp.broadcast_to(thr - sv1[0:1, :] - jnp.log(z), aux_ref.shape)


def peer_topk(q, subkeys, *, tt=512):
    T = q.shape[0]
    tt = min(tt, T)
    assert T % tt == 0
    H, NK = PEER_HEADS, PEER_NKEYS
    pad8 = lambda n: -(-n // SUBLANES) * SUBLANES
    hps = PEER_TOPK_HEADS_PER_STEP
    assert H % hps == 0
    return pl.pallas_call(
        _peer_topk_kernel,
        grid=(T // tt, H // hps),
        in_specs=[
            pl.BlockSpec((tt, hps * 2 * LANES), lambda i, h: (i, h)),
            pl.BlockSpec((hps, 2, NK, LANES), lambda i, h: (h, 0, 0, 0)),
        ],
        out_specs=[
            pl.BlockSpec((hps, 3, NK, tt), lambda i, h: (h, 0, 0, i)),
            pl.BlockSpec((hps, SUBLANES, tt), lambda i, h: (h, 0, i)),
        ],
        out_shape=[
            jax.ShapeDtypeStruct((H, 3, NK, T), F32),
            jax.ShapeDtypeStruct((H, SUBLANES, T), F32),
        ],
        scratch_shapes=[pltpu.VMEM((hps, 2, pad8(PEER_RANKS), tt), F32),
                        pltpu.VMEM((hps, pad8(sum(_cand_counts())), tt), F32)],
        compiler_params=_cparams(("parallel", "arbitrary")),
        name="peer_topk",
    )(q, subkeys)


def _peer_dense_kernel(*refs, ib, emit_tables):
    first = 0 if emit_tables else 1
    offs_ref = None if emit_tables else refs[0]
    x_ref, g_ref, pw_ref, aux_ref, u_ref, v_ref, o_ref = refs[first:first + 7]
    rest = refs[first + 7:]
    xnT_ref, acc_ref, w_ref = rest[-3:]
    e = pl.program_id(1)
    H, NK = PEER_HEADS, PEER_NKEYS
    tt = x_ref.shape[0]
    strip = min(LANES, tt)
    assert tt % strip == 0

    def gates(blk, ii, dst):
        i = blk * ib + ii
        thr1 = [pw_ref[h, 0, pl.ds(i, 1), :] for h in range(H)]
        e1 = [0.5 * jnp.exp(aux_ref[h, 0:1, :] - thr1[h]) for h in range(H)]
        for tc in range(tt // strip):
            ts_ = slice(tc * strip, (tc + 1) * strip)
            for j0 in range(0, NK, GATE_ROWS):
                js = slice(j0, j0 + GATE_ROWS)
                w = None
                for h in range(H):
                    sel = pw_ref[h, 1, js, ts_] >= thr1[h][:, ts_]
                    c = jnp.where(sel, pw_ref[h, 2, js, ts_], 0.0) * e1[h][:, ts_]
                    w = c if w is None else w + c
                dst[ii * NK + j0:ii * NK + j0 + GATE_ROWS, ts_] = w
        last = jnp.sum(w[0:1, 0:1])
        return last != last

    @pl.when(e == 0)
    def _():
        xn = _rms(x_ref[...], g_ref[...])
        xnT_ref[...] = xn.T.astype(BF16)
        acc_ref[...] = jnp.zeros_like(acc_ref)

    v16 = v_ref[...].astype(BF16)
    if emit_tables:
        u16 = u_ref[...].astype(BF16)
        rest[0][...] = u16
        rest[1][...] = v16
        hT = _dot(u16, xnT_ref[...])
        for ii in range(ib):
            gates(e, ii, w_ref)
    else:
        hs = []
        for ii in range(ib):
            bad = gates(e, ii, w_ref)
            off = pl.multiple_of(jnp.where(bad, offs_ref[ii], ii * NK), NK)
            hs.append(_dot(u_ref[pl.ds(off, NK), :], xnT_ref[...]))
        hT = jnp.concatenate(hs, axis=0)
    aT = w_ref[...] * _gelu_tanh_x2(hT)
    acc_ref[...] += _dot(aT.T.astype(BF16), v16)

    @pl.when(e == pl.num_programs(1) - 1)
    def _():
        o_ref[...] = x_ref[...] + acc_ref[...]


def peer_dense(x, g, pw, aux, u_tab, v_tab, *, layer=0, emit_tables=False, tt=512, ib=None):
    T, D = x.shape
    if ib is None:
        ib = 4 if emit_tables else 8
    boff = layer * (PEER_NKEYS // ib)
    tt = min(tt, T)
    assert T % tt == 0 and PEER_NKEYS % ib == 0 and not (emit_tables and T != tt)
    H, NK = PEER_HEADS, PEER_NKEYS
    eb = ib * NK
    once = {} if emit_tables else dict(pipeline_mode=pl.Buffered(1))
    out_specs = [pl.BlockSpec((tt, D), lambda i, e, *_: (i, 0))]
    out_shape = [jax.ShapeDtypeStruct((T, D), F32)]
    args = (x, g.reshape(1, D), pw, aux, u_tab, v_tab)
    if emit_tables:
        out_specs += [pl.BlockSpec((eb, D), lambda i, e, *_: (e, 0))] * 2
        out_shape += [jax.ShapeDtypeStruct((NK * NK, D), BF16)] * 2
    else:
        args = (jnp.arange(ib, dtype=jnp.int32) * NK,) + args
    outs = pl.pallas_call(
        functools.partial(_peer_dense_kernel, ib=ib, emit_tables=emit_tables),
        grid_spec=pltpu.PrefetchScalarGridSpec(
            num_scalar_prefetch=0 if emit_tables else 1,
            grid=(T // tt, NK // ib),
            in_specs=[
                pl.BlockSpec((tt, D), lambda i, e, *_: (i, 0), **once),
                pl.BlockSpec((1, D), lambda i, e, *_: (0, 0)),
                pl.BlockSpec((H, 3, NK, tt), lambda i, e, *_: (0, 0, 0, i), **once),
                pl.BlockSpec((H, SUBLANES, tt), lambda i, e, *_: (0, 0, i)),
                pl.BlockSpec((eb, D), lambda i, e, *_: (boff + e, 0)),
                pl.BlockSpec((eb, D), lambda i, e, *_: (boff + e, 0)),
            ],
            out_specs=out_specs,
            scratch_shapes=[
                pltpu.VMEM((D, tt), BF16),
                pltpu.VMEM((tt, D), F32),
                pltpu.VMEM((eb, tt), F32),
            ],
        ),
        out_shape=out_shape,
        compiler_params=_cparams(("parallel", "arbitrary")),
        name="peer_dense",
    )(*args)
    return tuple(outs) if emit_tables else outs[0]


def _prep_weights(w_in_a, w_in_b, w_out, w_mem_kv, peer_wq, peer_u, peer_v, lru_wr, lru_wi):
    H = GDN_HEADS
    qkvz = GDN_QKV + TM_WIDTH
    wb = jnp.concatenate(
        [
            w_in_b[:, :, :qkvz],
            w_in_b[:, :, qkvz + 2 * H:],
            w_in_b[:, :, qkvz:qkvz + 2 * H],
            jnp.zeros(w_in_b.shape[:2] + (B_IN_PAD - w_in_b.shape[2],), w_in_b.dtype),
        ],
        axis=-1,
    )
    return dict(
        w_in_a=w_in_a.astype(BF16),
        w_in_b=wb.astype(BF16),
        w_out_tm=w_out[:, :TM_WIDTH].astype(BF16),
        w_out_xa=w_out[:, TM_WIDTH:].astype(BF16),
        w_mem_kv=w_mem_kv.astype(BF16),
        peer_wq=peer_wq.astype(BF16),
        peer_u=peer_u.reshape(-1, peer_u.shape[-1]),
        peer_v=peer_v.reshape(-1, peer_v.shape[-1]),
        lru_wr=lru_wr.astype(BF16),
        lru_wi=lru_wi.astype(BF16),
    )


def _trunk(x, mem, lru_h0, lru_c0, gdn_S0, gdn_c0, w, wc, *, B, L, time_major, peer_tabs16=None):
    if time_major:
        G, R, tl = 1, B, L
    else:
        G, R, tl = B, 1, min(L, 256)
    Lp = -(-L // SUBLANES) * SUBLANES
    hist = CONV_W - 1

    def to_batch_major(a):
        c = a.shape[-1]
        a = a.reshape(L, B, c).transpose(1, 0, 2)
        return jnp.pad(a, ((0, 0), (0, Lp - L), (0, 0))).reshape(B * Lp, c)

    def to_time_major(a):
        c = a.shape[-1]
        return a.reshape(B, Lp, c)[:, :L].transpose(1, 0, 2).reshape(L * B, c)

    def last_rows(u, c0, c1):
        if time_major:
            return u.reshape(L, B, -1)[L - hist:, :, c0:c1].transpose(1, 0, 2)
        return u.reshape(B, L, -1)[:, L - hist:, c0:c1]

    n_b = gdn_S0.shape[0]
    S0_all = gdn_S0.reshape((n_b * B,) + gdn_S0.shape[2:])
    S_all = jnp.zeros(S0_all.shape, F32)
    lru_h, lru_c, gdn_c, tabs16 = [], [], [], []
    for l in range(DEPTH):
        j = l // 2
        if l % 2 == 0:
            u, qmem = norm_matmul(x, w['norm_mix'][l], wc['w_in_a'][j], side_cols=(2 * TM_WIDTH, XA_WIDTH))
            c0 = lru_c0[j]
            if time_major:
                c0k = c0.transpose(1, 0, 2).reshape(1, hist * B, TM_WIDTH)
                h0k = lru_h0[j].reshape(1, B, TM_WIDTH)
            else:
                c0k = c0
                h0k = lru_h0[j].reshape(B, 1, TM_WIDTH)
            tm, h = rglru(u, c0k, h0k, w['lru_conv_w'][j], w['lru_conv_b'][j], wc['lru_wr'][j], w['lru_br'][j],
                          wc['lru_wi'][j], w['lru_bi'][j], w['lru_lambda'][j], G=G, L=L, R=R, tl=tl)
            lru_h.append(h.reshape(B, TM_WIDTH))
            lru_c.append(last_rows(u, TM_WIDTH, 2 * TM_WIDTH))
        else:
            u, qmem = norm_matmul(x, w['norm_mix'][l], wc['w_in_b'][j], side_cols=(GDN_QKV + TM_WIDTH, XA_WIDTH))
            C = min(GDN_CHUNK, Lp)
            nb = GDN_SEQS_PER_STEP if (Lp == C and B % GDN_SEQS_PER_STEP == 0) else 1
            u_bm = to_batch_major(u) if time_major else u
            o_bm, S_all = gdn(u_bm, gdn_c0[j], S0_all, S_all, w['gdn_conv_w'][j], w['gdn_conv_b'][j],
                              w['gdn_a_log'][j], w['gdn_dt_bias'][j], w['gdn_o_norm'][j], layer=j, B=B, L=Lp, C=C,
                              n_valid=min(L, C), nb=nb, qkv_blk=0, z_blk=GDN_QKV // TM_WIDTH,
                              ba_blk=(GDN_QKV + TM_WIDTH + XA_WIDTH) // LANES)
            tm = to_time_major(o_bm) if time_major else o_bm
            gdn_c.append(last_rows(u, 0, GDN_QKV))
        if time_major:
            q_bm = to_batch_major(qmem)
            xa = to_time_major(xattn_cache(q_bm, mem[0], mem[1], layer=l, B=B, tq=Lp,
                                           nb=XATTN_SEQS_PER_STEP if B % XATTN_SEQS_PER_STEP == 0 else 1))
        else:
            xa = xattn(qmem, mem[l], mem[l], B=B, Lq=L, tq=min(L, 512), q_blk=0, k_blk=0, v_blk=1)
        x = out_proj(tm, xa, wc['w_out_tm'][l], wc['w_out_xa'][l], x)
        q = norm_matmul(x, w['norm_ffn'][l], wc['peer_wq'][l])
        pw, aux = peer_topk(q, w['peer_subkeys'][l])
        if peer_tabs16 is None:
            x, u16, v16 = peer_dense(x, w['norm_ffn'][l], pw, aux, wc['peer_u'], wc['peer_v'], layer=l,
                                     emit_tables=True)
            tabs16.append((u16, v16))
        else:
            x = peer_dense(x, w['norm_ffn'][l], pw, aux, peer_tabs16[l][0], peer_tabs16[l][1])
    y = rmsnorm_rows(x, w['norm_final'])
    return (y, jnp.stack(lru_h), jnp.stack(lru_c), S_all.reshape(gdn_S0.shape), jnp.stack(gdn_c)), tabs16


def kernel(x_prompt, x_sample, state_rglru_h, state_rglru_conv, state_gdn_S, state_gdn_conv, cache_mem_k, cache_mem_v, mem_prompt, norm_mix, norm_ffn, norm_final, w_in_a, w_in_b, w_out, lru_conv_w, lru_conv_b, lru_wr, lru_br, lru_wi, lru_bi, lru_lambda, gdn_conv_w, gdn_conv_b, gdn_a_log, gdn_dt_bias, gdn_o_norm, norm_mem, w_mem_kv, peer_wq, peer_subkeys, peer_u, peer_v):
    w = dict(norm_mix=norm_mix, norm_ffn=norm_ffn, norm_final=norm_final, lru_conv_w=lru_conv_w,
             lru_conv_b=lru_conv_b, lru_br=lru_br, lru_bi=lru_bi, lru_lambda=lru_lambda, gdn_conv_w=gdn_conv_w,
             gdn_conv_b=gdn_conv_b, gdn_a_log=gdn_a_log, gdn_dt_bias=gdn_dt_bias, gdn_o_norm=gdn_o_norm,
             peer_subkeys=peer_subkeys)
    wc = _prep_weights(w_in_a, w_in_b, w_out, w_mem_kv, peer_wq, peer_u, peer_v, lru_wr, lru_wi)
    Bp, Lp_, D = x_prompt.shape
    Bs, Ls, _ = x_sample.shape
    n_a, n_b = state_rglru_h.shape[0], state_gdn_S.shape[0]

    xs = x_sample.transpose(1, 0, 2).reshape(Ls * Bs, D)
    rows_hd = (DEPTH * Bs, N_MEM * XA_HEADS, HEAD_DIM)
    (y_s, s_h, s_ca, s_S, s_cb), tabs16 = _trunk(
        xs, (cache_mem_k.reshape(rows_hd), cache_mem_v.reshape(rows_hd)), state_rglru_h, state_rglru_conv,
        state_gdn_S, state_gdn_conv, w, wc, B=Bs, L=Ls, time_major=True)
    y_s = y_s.reshape(Ls, Bs, D).transpose(1, 0, 2)

    mem2 = mem_prompt.reshape(Bp * N_MEM, D)
    kv = [norm_matmul(mem2, norm_mem[l], wc['w_mem_kv'][l]) for l in range(DEPTH)]
    (y_p, p_h, p_ca, p_S, p_cb), _ = _trunk(
        x_prompt.reshape(Bp * Lp_, D), kv,
        jnp.zeros((n_a, Bp, TM_WIDTH), F32), jnp.zeros((n_a, Bp, CONV_W - 1, TM_WIDTH), F32),
        jnp.zeros((n_b, Bp, GDN_HEADS, HEAD_DIM, HEAD_DIM), F32), jnp.zeros((n_b, Bp, CONV_W - 1, GDN_QKV), F32),
        w, wc, B=Bp, L=Lp_, time_major=False, peer_tabs16=tabs16)
    p_mem_k = jnp.stack([a[:, :XA_WIDTH].reshape(Bp, N_MEM, XA_HEADS, HEAD_DIM) for a in kv])
    p_mem_v = jnp.stack([a[:, XA_WIDTH:].reshape(Bp, N_MEM, XA_HEADS, HEAD_DIM) for a in kv])
    return (y_p.reshape(Bp, Lp_, D), y_s, p_h, p_ca, p_S, p_cb, p_mem_k, p_mem_v, s_h, s_ca, s_S, s_cb)
```

```python
import functools
import math

import jax
import jax.numpy as jnp
from jax import lax
from jax.experimental import pallas as pl
from jax.experimental.pallas import tpu as pltpu

F32 = jnp.float32
BF16 = jnp.bfloat16

D_MODEL = 2048
DEPTH = 4
HEAD_DIM = 128
XA_HEADS = 4
XA_WIDTH = XA_HEADS * HEAD_DIM
TM_WIDTH = D_MODEL - XA_WIDTH
N_MEM = 256
CONV_W = 4
EPS = 1e-6
LRU_BLOCKS = TM_WIDTH // HEAD_DIM
LRU_C = 8.0
GDN_HEADS = TM_WIDTH // HEAD_DIM
GDN_QKV = 3 * TM_WIDTH
GDN_CHUNK = 64
GDN_SEQS_PER_STEP = 4
XATTN_SEQS_PER_STEP = 4
PEER_HEADS = 8
PEER_NKEYS = 128
PEER_TOPK = 16
B_IN_PAD = 7168

SUBLANES = 8
LANES = 128
VMEM_LIMIT = 52 * 1024 * 1024

NEG_INF = float("-inf")


def _cparams(sem):
    return pltpu.CompilerParams(dimension_semantics=sem, vmem_limit_bytes=VMEM_LIMIT)


def _dot(a, b, precision=None):
    return jnp.dot(a, b, preferred_element_type=F32, precision=precision)


def _mm(a, b):
    return jnp.dot(a.astype(BF16), b.astype(BF16), preferred_element_type=F32)


def _split_bf16(a):
    hi = a.astype(BF16)
    return hi, (a - hi.astype(F32)).astype(BF16)


def _dot3(a, b):
    ah, al = _split_bf16(a)
    bh, bl = _split_bf16(b)
    return _dot(ah, bh) + (_dot(ah, bl) + _dot(al, bh))


def _dot_nt(a, b, precision=None):
    return lax.dot_general(a, b, (((1,), (1,)), ((), ())), preferred_element_type=F32, precision=precision)


def _dot_tn(a, b, precision=None):
    return lax.dot_general(a, b, (((0,), (0,)), ((), ())), preferred_element_type=F32, precision=precision)


def _sigmoid(x):
    return 1.0 / (1.0 + jnp.exp(-x))


def _silu(x):
    return x * _sigmoid(x)


def _gelu_tanh(x):
    c = math.sqrt(2.0 / math.pi)
    return 0.5 * x * (1.0 + jnp.tanh(c * (x + 0.044715 * (x * x * x))))


def _gelu_tanh_x2(x):
    c = math.sqrt(2.0 / math.pi)
    return x * (1.0 + jnp.tanh(x * (c + (0.044715 * c) * (x * x))))


def _softplus(x):
    return jnp.maximum(x, 0.0) + jnp.log(1.0 + jnp.exp(-jnp.abs(x)))


def _rms(x, g):
    ms = jnp.mean(x * x, axis=-1, keepdims=True)
    return x * lax.rsqrt(ms + EPS) * g


def _norm_matmul_kernel(x_ref, g_ref, w_ref, o_ref, *rest, side):
    xn_ref = rest[-1]

    @pl.when(pl.program_id(1) == 0)
    def _():
        xn_ref[...] = _rms(x_ref[...], g_ref[...]).astype(BF16)

    acc = _dot(xn_ref[...], w_ref[...])
    o_ref[...] = acc
    if side is not None:
        tile, off, width = side

        @pl.when(pl.program_id(1) == tile)
        def _():
            rest[0][...] = acc[:, off:off + width]


def _matmul_tiles(m, n, tm, tn):
    tm = min(tm, m)
    while m % tm:
        tm //= 2
    while n % tn:
        tn //= 2
    return tm, tn


def norm_matmul(x, g, w, *, side_cols=None, tm=1024, tn=1024):
    m, k = x.shape
    n = w.shape[1]
    tm, tn = _matmul_tiles(m, n, tm, tn)
    assert tm % SUBLANES == 0 and tn % LANES == 0
    out_specs = [pl.BlockSpec((tm, tn), lambda i, j: (i, j))]
    out_shape = [jax.ShapeDtypeStruct((m, n), F32)]
    side = None
    if side_cols is not None:
        start, width = side_cols
        side = (start // tn, start % tn, width)
        assert start % tn + width <= tn
        out_specs.append(pl.BlockSpec((tm, width), lambda i, j: (i, 0)))
        out_shape.append(jax.ShapeDtypeStruct((m, width), F32))
    outs = pl.pallas_call(
        functools.partial(_norm_matmul_kernel, side=side),
        grid=(m // tm, n // tn),
        in_specs=[
            pl.BlockSpec((tm, k), lambda i, j: (i, 0)),
            pl.BlockSpec((1, k), lambda i, j: (0, 0)),
            pl.BlockSpec((k, tn), lambda i, j: (0, j)),
        ],
        out_specs=out_specs,
        out_shape=out_shape,
        scratch_shapes=[pltpu.VMEM((tm, k), BF16)],
        compiler_params=_cparams(("parallel", "arbitrary")),
        name="norm_matmul",
    )(x, g.reshape(1, k), w)
    return outs[0] if side_cols is None else tuple(outs)


def _out_proj_kernel(a_ref, b_ref, wa_ref, wb_ref, r_ref, o_ref):
    acc = _dot(a_ref[...], wa_ref[...]) + _dot(b_ref[...], wb_ref[...])
    o_ref[...] = r_ref[...] + acc


def out_proj(a, b, wa, wb, r, *, tm=1024, tn=1024):
    m, ka = a.shape
    kb = b.shape[1]
    n = wa.shape[1]
    tm, tn = _matmul_tiles(m, n, tm, tn)
    assert tm % SUBLANES == 0 and tn % LANES == 0
    return pl.pallas_call(
        _out_proj_kernel,
        grid=(m // tm, n // tn),
        in_specs=[
            pl.BlockSpec((tm, ka), lambda i, j: (i, 0)),
            pl.BlockSpec((tm, kb), lambda i, j: (i, 0)),
            pl.BlockSpec((ka, tn), lambda i, j: (0, j)),
            pl.BlockSpec((kb, tn), lambda i, j: (0, j)),
            pl.BlockSpec((tm, tn), lambda i, j: (i, j)),
        ],
        out_specs=pl.BlockSpec((tm, tn), lambda i, j: (i, j)),
        out_shape=jax.ShapeDtypeStruct((m, n), F32),
        compiler_params=_cparams(("parallel", "arbitrary")),
        name="out_proj",
    )(a, b, wa, wb, r)


def _rmsnorm_kernel(x_ref, g_ref, o_ref):
    o_ref[...] = _rms(x_ref[...], g_ref[...])


def rmsnorm_rows(x, g, *, tm=512):
    m, k = x.shape
    tm = min(tm, m)
    assert m % tm == 0
    return pl.pallas_call(
        _rmsnorm_kernel,
        grid=(m // tm,),
        in_specs=[pl.BlockSpec((tm, k), lambda i: (i, 0)), pl.BlockSpec((1, k), lambda i: (0, 0))],
        out_specs=pl.BlockSpec((tm, k), lambda i: (i, 0)),
        out_shape=jax.ShapeDtypeStruct((m, k), F32),
        compiler_params=_cparams(("parallel",)),
        name="final_rmsnorm",
    )(x, g.reshape(1, k))


def _rglru_kernel(gate_ref, xr_ref, cw_ref, cb_ref, c0_ref, h0_ref, wr_ref, br_ref, wi_ref, bi_ref, lam_ref,
                  y_ref, hout_ref, hist_ref, a_ref, b_ref, hs_ref, h_ref, *, tl, R):
    n = pl.program_id(1)
    rows = tl * R
    hist_rows = (CONV_W - 1) * R
    off0 = -(-hist_rows // SUBLANES) * SUBLANES

    @pl.when(n == 0)
    def _():
        hist_ref[off0 - hist_rows:off0, :] = c0_ref[0]
        h_ref[...] = h0_ref[0]

    x = xr_ref[...]
    hist_ref[off0:off0 + rows, :] = x
    xc = cb_ref[...]
    for j in range(CONV_W):
        s = off0 - hist_rows + j * R
        xc = xc + hist_ref[s:s + rows, :] * cw_ref[j:j + 1, :]
    tail = hist_ref[off0 + rows - hist_rows:off0 + rows, :]
    hist_ref[off0 - hist_rows:off0, :] = tail

    sp = _softplus(-lam_ref[...])
    for blk in range(LRU_BLOCKS):
        cs = slice(blk * HEAD_DIM, (blk + 1) * HEAD_DIM)
        xb = xc[:, cs]
        xb16 = xb.astype(BF16)
        r = _sigmoid(_dot(xb16, wr_ref[blk]) + br_ref[blk])
        i = _sigmoid(_dot(xb16, wi_ref[blk]) + bi_ref[blk])
        log_a = (-LRU_C) * r * sp[:, cs]
        a = jnp.exp(log_a)
        a_ref[:, cs] = a
        b_ref[:, cs] = jnp.sqrt(1.0 - a * a) * (i * xb)

    if R % SUBLANES == 0:
        def step(l, carry):
            rs = pl.ds(pl.multiple_of(l * R, SUBLANES), R)
            h = a_ref[rs, :] * h_ref[...] + b_ref[rs, :]
            h_ref[...] = h
            hs_ref[rs, :] = h
            return carry

        lax.fori_loop(0, tl, step, 0)
    else:
        def step(l, h):
            rs = pl.ds(l * R, R)
            h = a_ref[rs, :] * h + b_ref[rs, :]
            hs_ref[rs, :] = h
            return h

        h_ref[...] = lax.fori_loop(0, tl, step, h_ref[...], unroll=SUBLANES)
    y_ref[...] = (hs_ref[...] * _gelu_tanh(gate_ref[...])).astype(y_ref.dtype)
    hout_ref[0] = h_ref[...]


def rglru(u, c0, h0, cw, cb, wr, br, wi, bi, lam, *, G, L, R, tl):
    W = TM_WIDTH
    assert L % tl == 0
    rows = tl * R
    nt = L // tl
    hist_rows = (CONV_W - 1) * R
    off0 = -(-hist_rows // SUBLANES) * SUBLANES
    kern = functools.partial(_rglru_kernel, tl=tl, R=R)
    const2 = lambda g, n: (0, 0)
    const3 = lambda g, n: (0, 0, 0)
    y, hout = pl.pallas_call(
        kern,
        grid=(G, nt),
        in_specs=[
            pl.BlockSpec((rows, W), lambda g, n: (g * nt + n, 0)),
            pl.BlockSpec((rows, W), lambda g, n: (g * nt + n, 1)),
            pl.BlockSpec((CONV_W, W), const2),
            pl.BlockSpec((1, W), const2),
            pl.BlockSpec((1, hist_rows, W), lambda g, n: (g, 0, 0)),
            pl.BlockSpec((1, R, W), lambda g, n: (g, 0, 0)),
            pl.BlockSpec((LRU_BLOCKS, HEAD_DIM, HEAD_DIM), const3),
            pl.BlockSpec((LRU_BLOCKS, 1, HEAD_DIM), const3),
            pl.BlockSpec((LRU_BLOCKS, HEAD_DIM, HEAD_DIM), const3),
            pl.BlockSpec((LRU_BLOCKS, 1, HEAD_DIM), const3),
            pl.BlockSpec((1, W), const2),
        ],
        out_specs=[
            pl.BlockSpec((rows, W), lambda g, n: (g * nt + n, 0)),
            pl.BlockSpec((1, R, W), lambda g, n: (g, 0, 0)),
        ],
        out_shape=[
            jax.ShapeDtypeStruct((G * L * R, W), BF16),
            jax.ShapeDtypeStruct((G, R, W), F32),
        ],
        scratch_shapes=[
            pltpu.VMEM((off0 + rows, W), F32),
            pltpu.VMEM((rows, W), F32),
            pltpu.VMEM((rows, W), F32),
            pltpu.VMEM((rows, W), F32),
            pltpu.VMEM((R, W), F32),
        ],
        compiler_params=_cparams(("parallel", "arbitrary")),
        name="rglru",
    )(u, u, cw, cb.reshape(1, W), c0, h0, wr, br.reshape(LRU_BLOCKS, 1, HEAD_DIM), wi,
      bi.reshape(LRU_BLOCKS, 1, HEAD_DIM), lam.reshape(1, W))
    return y, hout


def _inv_unit_lower_many(As, C, row, col):
    eye = (row == col).astype(F32)
    base = min(SUBLANES, C)
    sh = int(math.log2(base))
    dmask = (row >> sh) == (col >> sh)
    Ps = [jnp.where(dmask, A, 0.0) for A in As]
    Ts = [eye - P for P in Ps]
    span = 2
    while span < base:
        Ps = [_dot3(P, P) for P in Ps]
        Ts = [_dot3(T, eye + P) for T, P in zip(Ts, Ps)]
        span *= 2
    s = base
    while s < C:
        sh = int(math.log2(s))
        off = ((row >> (sh + 1)) == (col >> (sh + 1))) & ((row >> sh) != (col >> sh))
        Ms = [_dot3(jnp.where(off, A, 0.0), T) for A, T in zip(As, Ts)]
        Ts = [T - _dot3(T, M) for T, M in zip(Ts, Ms)]
        s *= 2
    return Ts


def _gdn_kernel(qkv_ref, z_ref, ba_ref, cw_ref, cb_ref, c0_ref, S0_ref, alog_ref, dtb_ref, onorm_ref,
                *rest, C, n_valid, nb):
    o_ref, S_ref, hist_ref = rest[-3:]
    n = pl.program_id(1)
    H, Dh = GDN_HEADS, HEAD_DIM
    hist_rows = CONV_W - 1
    off0 = SUBLANES
    rows = nb * C

    @pl.when(n == 0)
    def _():
        for s in range(nb):
            hist_ref[s, off0 - hist_rows:off0, :] = c0_ref[s]
        S_ref[...] = S0_ref[...]

    ts = []
    for s in range(nb):
        hist_ref[s, off0:off0 + C, :] = qkv_ref[s * C:(s + 1) * C, :]
        xc = cb_ref[...]
        for j in range(CONV_W):
            r0 = off0 - hist_rows + j
            xc = xc + hist_ref[s, r0:r0 + C, :] * cw_ref[j:j + 1, :]
        tail = hist_ref[s, off0 + C - hist_rows:off0 + C, :]
        hist_ref[s, off0 - hist_rows:off0, :] = tail
        ts.append(_silu(xc))

    ba = ba_ref[...]
    beta_all = _sigmoid(ba)
    g_all = -jnp.exp(alog_ref[...]) * _softplus(ba + dtb_ref[...])
    if n_valid < C:
        rmask = (lax.broadcasted_iota(jnp.int32, (rows, LANES), 0) & (C - 1)) < n_valid
        beta_all = jnp.where(rmask, beta_all, 0.0)
        g_all = jnp.where(rmask, g_all, 0.0)
    if rows < LANES:
        g_pad = jnp.concatenate([g_all, jnp.zeros((LANES - rows, LANES), F32)], axis=0)
    else:
        g_pad = g_all
    r128 = lax.broadcasted_iota(jnp.int32, (LANES, LANES), 0)
    c128 = lax.broadcasted_iota(jnp.int32, (LANES, LANES), 1)
    shc = int(math.log2(C))
    tril = ((c128 <= r128) & ((c128 >> shc) == (r128 >> shc))).astype(F32)
    gc_pad = _dot(tril, g_pad, precision=lax.Precision.HIGHEST)
    gcT = gc_pad.T

    row = lax.broadcasted_iota(jnp.int32, (C, C), 0)
    col = lax.broadcasted_iota(jnp.int32, (C, C), 1)
    incl = col <= row
    strict = col < row

    probs = [(s, h) for s in range(nb) for h in range(H)]
    qn, kn, kn16, vh, bcol, gcol, decay, egc = {}, {}, {}, {}, {}, {}, {}, {}
    for p in probs:
        s, h = p
        t = ts[s]
        qh = t[:, h * Dh:(h + 1) * Dh]
        kh = t[:, TM_WIDTH + h * Dh:TM_WIDTH + (h + 1) * Dh]
        vh[p] = t[:, 2 * TM_WIDTH + h * Dh:2 * TM_WIDTH + (h + 1) * Dh]
        qn[p] = qh * lax.rsqrt(jnp.sum(qh * qh, -1, keepdims=True) + EPS) * (Dh ** -0.5)
        kn[p] = kh * lax.rsqrt(jnp.sum(kh * kh, -1, keepdims=True) + EPS)
        kn16[p] = kn[p].astype(BF16)
        bcol[p] = beta_all[s * C:(s + 1) * C, h:h + 1]
        gcol[p] = gc_pad[s * C:(s + 1) * C, H + h:H + h + 1]
        grow = gcT[H + h:H + h + 1, s * C:(s + 1) * C]
        diff = gcol[p] - grow
        decay[p] = jnp.where(incl, jnp.exp(jnp.where(incl, diff, 0.0)), 0.0)
        egc[p] = jnp.exp(gcol[p])
    kk = {p: _dot_nt(kn16[p], kn16[p]) for p in probs}
    qkr = {p: _dot_nt(qn[p].astype(BF16), kn16[p]) for p in probs}
    As = [jnp.where(strict, bcol[p] * kk[p] * decay[p], 0.0) for p in probs]
    Ts = _inv_unit_lower_many(As, C, row, col)
    rhs = [jnp.concatenate([bcol[p] * vh[p], (bcol[p] * egc[p]) * kn[p]], axis=1) for p in probs]
    sol = {p: _dot3(T, r) for p, T, r in zip(probs, Ts, rhs)}
    S = {p: S_ref[p[0], p[1]] for p in probs}
    S16 = {p: S[p].astype(BF16) for p in probs}
    kS = {p: _dot(sol[p][:, Dh:].astype(BF16), S16[p]) for p in probs}
    qS = {p: _dot((qn[p] * egc[p]).astype(BF16), S16[p]) for p in probs}
    u_new = {p: (sol[p][:, :Dh] - kS[p]).astype(BF16) for p in probs}
    qk16 = {p: jnp.where(incl, qkr[p] * decay[p], 0.0).astype(BF16) for p in probs}
    o = {p: qS[p] + _dot(qk16[p], u_new[p]) for p in probs}
    dS = {}
    for p in probs:
        g_last = gcol[p][C - 1:C, :]
        kdec = (kn[p] * jnp.exp(g_last - gcol[p])).astype(BF16)
        dS[p] = _dot_tn(kdec, u_new[p])
    for p in probs:
        s, h = p
        g_last = gcol[p][C - 1:C, :]
        S_ref[s, h] = S[p] * jnp.exp(g_last) + dS[p]
        on = o[p] * lax.rsqrt(jnp.mean(o[p] * o[p], -1, keepdims=True) + EPS) * onorm_ref[...]
        zs = z_ref[s * C:(s + 1) * C, h * Dh:(h + 1) * Dh]
        o_ref[s * C:(s + 1) * C, h * Dh:(h + 1) * Dh] = (on * _silu(zs)).astype(o_ref.dtype)


def gdn(u, c0, S0_all, S_out_prev, cw, cb, a_log, dt_bias, o_norm, *, layer, B, L, C, n_valid, nb, qkv_blk, z_blk,
        ba_blk):
    H, Dh = GDN_HEADS, HEAD_DIM
    nc = L // C
    assert B % nb == 0 and (nb == 1 or nc == 1) and nb * C <= LANES
    rows = nb * C
    soff = layer * (B // nb)
    alog = jnp.zeros((1, LANES), F32).at[0, H:2 * H].set(a_log)
    dtb = jnp.zeros((1, LANES), F32).at[0, H:2 * H].set(dt_bias)
    kern = functools.partial(_gdn_kernel, C=C, n_valid=n_valid, nb=nb)
    const2 = lambda b, n: (0, 0)
    in_specs = [
        pl.BlockSpec((rows, GDN_QKV), lambda b, n: (b * nc + n, qkv_blk)),
        pl.BlockSpec((rows, TM_WIDTH), lambda b, n: (b * nc + n, z_blk)),
        pl.BlockSpec((rows, LANES), lambda b, n: (b * nc + n, ba_blk)),
        pl.BlockSpec((CONV_W, GDN_QKV), const2),
        pl.BlockSpec((1, GDN_QKV), const2),
        pl.BlockSpec((nb, CONV_W - 1, GDN_QKV), lambda b, n: (b, 0, 0)),
        pl.BlockSpec((nb, H, Dh, Dh), lambda b, n: (soff + b, 0, 0, 0)),
        pl.BlockSpec((1, LANES), const2),
        pl.BlockSpec((1, LANES), const2),
        pl.BlockSpec((1, Dh), const2),
    ]
    args = [u, u, u, cw, cb.reshape(1, GDN_QKV), c0, S0_all, alog, dtb, o_norm.reshape(1, Dh)]
    aliases = {}
    if S_out_prev is not None:
        in_specs.append(pl.BlockSpec(memory_space=pl.ANY))
        args.append(S_out_prev)
        aliases = {len(args) - 1: 1}
    o, S = pl.pallas_call(
        kern,
        grid=(B // nb, nc),
        in_specs=in_specs,
        out_specs=[
            pl.BlockSpec((rows, TM_WIDTH), lambda b, n: (b * nc + n, 0)),
            pl.BlockSpec((nb, H, Dh, Dh), lambda b, n: (soff + b, 0, 0, 0)),
        ],
        out_shape=[
            jax.ShapeDtypeStruct((B * L, TM_WIDTH), BF16),
            jax.ShapeDtypeStruct(S0_all.shape, F32),
        ],
        scratch_shapes=[pltpu.VMEM((nb, SUBLANES + C, GDN_QKV), F32)],
        input_output_aliases=aliases,
        compiler_params=_cparams(("parallel", "arbitrary")),
        name="gdn",
    )(*args)
    return o, S


def _softmax_rows(s):
    e = jnp.exp(s - jnp.max(s, axis=-1, keepdims=True))
    return e / jnp.sum(e, axis=-1, keepdims=True)


def _xattn_kernel(q_ref, k_ref, v_ref, o_ref):
    scale = HEAD_DIM ** -0.5
    for h in range(XA_HEADS):
        cs = slice(h * HEAD_DIM, (h + 1) * HEAD_DIM)
        kh = k_ref[:, cs].astype(BF16)
        vh = v_ref[:, cs].astype(BF16)
        p = _softmax_rows(_dot_nt(q_ref[:, cs].astype(BF16), kh) * scale)
        o_ref[:, cs] = _dot(p.astype(BF16), vh).astype(o_ref.dtype)


def _xattn_cache_kernel(q_ref, k_ref, v_ref, o_ref, *, nb):
    scale = HEAD_DIM ** -0.5
    tq = q_ref.shape[0] // nb
    rows = XA_HEADS * tq
    cols = N_MEM * XA_HEADS
    own = ((lax.broadcasted_iota(jnp.int32, (rows, cols), 1) % XA_HEADS)
           == (lax.broadcasted_iota(jnp.int32, (rows, cols), 0) // tq))
    sc = []
    for s in range(nb):
        q = q_ref[s * tq:(s + 1) * tq, :]
        qs = jnp.concatenate([q[:, h * HEAD_DIM:(h + 1) * HEAD_DIM] for h in range(XA_HEADS)], axis=0)
        sc.append(_dot_nt(qs.astype(BF16), k_ref[s].astype(BF16)) * scale)
    pr = [_softmax_rows(jnp.where(own, x, NEG_INF)).astype(BF16) for x in sc]
    for s in range(nb):
        o = _dot(pr[s], v_ref[s].astype(BF16))
        for h in range(XA_HEADS):
            o_ref[s * tq:(s + 1) * tq, h * HEAD_DIM:(h + 1) * HEAD_DIM] = o[h * tq:(h + 1) * tq].astype(o_ref.dtype)


def xattn(q, k, v, *, B, Lq, tq, q_blk=0, k_blk=0, v_blk=0):
    nq = Lq // tq
    return pl.pallas_call(
        _xattn_kernel,
        grid=(B, nq),
        in_specs=[
            pl.BlockSpec((tq, XA_WIDTH), lambda b, n: (b * nq + n, q_blk)),
            pl.BlockSpec((N_MEM, XA_WIDTH), lambda b, n: (b, k_blk)),
            pl.BlockSpec((N_MEM, XA_WIDTH), lambda b, n: (b, v_blk)),
        ],
        out_specs=pl.BlockSpec((tq, XA_WIDTH), lambda b, n: (b * nq + n, 0)),
        out_shape=jax.ShapeDtypeStruct((B * Lq, XA_WIDTH), BF16),
        compiler_params=_cparams(("parallel", "arbitrary")),
        name="xattn",
    )(q, k, v)


def xattn_cache(q, k, v, *, layer, B, tq, nb):
    assert B % nb == 0
    boff = layer * (B // nb)
    spec = pl.BlockSpec((nb, N_MEM * XA_HEADS, HEAD_DIM), lambda b: (boff + b, 0, 0))
    return pl.pallas_call(
        functools.partial(_xattn_cache_kernel, nb=nb),
        grid=(B // nb,),
        in_specs=[pl.BlockSpec((nb * tq, XA_WIDTH), lambda b: (b, 0)), spec, spec],
        out_specs=pl.BlockSpec((nb * tq, XA_WIDTH), lambda b: (b, 0)),
        out_shape=jax.ShapeDtypeStruct((B * tq, XA_WIDTH), BF16),
        compiler_params=_cparams(("parallel",)),
        name="xattn_cache",
    )(q, k, v)


PEER_RANKS = PEER_TOPK + 1
GATE_ROWS = 32
PEER_TOPK_HEADS_PER_STEP = 2


def _cand_counts():
    return [PEER_RANKS // (a + 1) for a in range(PEER_RANKS)]


def _sorting_network(n):
    pairs = []
    t = 1
    while t < n:
        p = t
        while p >= 1:
            for j in range(p % t, n - p, 2 * p):
                for i in range(min(p, n - j - p)):
                    if (i + j) // (2 * t) == (i + j + p) // (2 * t):
                        pairs.append((i + j, i + j + p))
            p //= 2
        t *= 2
    return pairs


def _peer_topk_kernel(q_ref, sk_ref, pw_ref, aux_ref, sv_ref, cand_ref):
    for hh in range(pw_ref.shape[0]):
        _peer_topk_head(q_ref.at[:, hh * 2 * LANES:(hh + 1) * 2 * LANES], sk_ref.at[hh], pw_ref.at[hh],
                        aux_ref.at[hh], sv_ref.at[hh], cand_ref.at[hh])


def _peer_topk_head(q_ref, sk_ref, pw_ref, aux_ref, sv_ref, cand_ref):
    K = PEER_TOPK
    q = q_ref[...]
    s = []
    for p in range(2):
        qp = q[:, p * LANES:(p + 1) * LANES]
        s.append(_dot_nt(sk_ref[p], qp, precision=lax.Precision.HIGHEST))
    for p in range(2):
        v = [s[p][g * SUBLANES:(g + 1) * SUBLANES, :] for g in range(PEER_NKEYS // SUBLANES)]
        for (a, b) in _sorting_network(len(v)):
            v[a], v[b] = jnp.maximum(v[a], v[b]), jnp.minimum(v[a], v[b])
        v.append(jnp.full_like(v[0], NEG_INF))
        for r in range(PEER_RANKS):
            m = jnp.max(v[0], axis=0, keepdims=True)
            sv_ref[p, r:r + 1, :] = m
            took = v[0] == m
            for k in range(PEER_RANKS - 1 - r):
                v[k] = jnp.where(took, v[k + 1], v[k])
    sv1 = sv_ref[0, 0:PEER_RANKS, :]
    sv2 = sv_ref[1, 0:PEER_RANKS, :]
    cand_ref[...] = jnp.full(cand_ref.shape, NEG_INF, F32)
    off = 0
    for a, nb in enumerate(_cand_counts()):
        cand_ref[off:off + nb, :] = sv1[a:a + 1, :] + sv2[0:nb, :]
        off += nb
    cur = cand_ref[...]
    tops = []
    for r in range(PEER_RANKS):
        m = jnp.max(cur, axis=0, keepdims=True)
        tops.append(m)
        cur = jnp.where(cur == m, NEG_INF, cur)
    z = jnp.ones_like(tops[0])
    for r in range(1, K):
        z = z + jnp.exp(tops[r] - tops[0])
    thr = 0.5 * (tops[K - 1] + tops[K])
    pw_ref[0] = thr - s[0]
    pw_ref[1] = s[1]
    pw_ref[2] = jnp.exp(s[1] - sv2[0:1, :])
    aux_ref[...] = jnp.broadcast_to(thr - sv1[0:1, :] - jnp.log(z), aux_ref.shape)


def peer_topk(q, subkeys, *, tt=512):
    T = q.shape[0]
    tt = min(tt, T)
    assert T % tt == 0
    H, NK = PEER_HEADS, PEER_NKEYS
    pad8 = lambda n: -(-n // SUBLANES) * SUBLANES
    hps = PEER_TOPK_HEADS_PER_STEP
    assert H % hps == 0
    return pl.pallas_call(
        _peer_topk_kernel,
        grid=(T // tt, H // hps),
        in_specs=[
            pl.BlockSpec((tt, hps * 2 * LANES), lambda i, h: (i, h)),
            pl.BlockSpec((hps, 2, NK, LANES), lambda i, h: (h, 0, 0, 0)),
        ],
        out_specs=[
            pl.BlockSpec((hps, 3, NK, tt), lambda i, h: (h, 0, 0, i)),
            pl.BlockSpec((hps, SUBLANES, tt), lambda i, h: (h, 0, i)),
        ],
        out_shape=[
            jax.ShapeDtypeStruct((H, 3, NK, T), F32),
            jax.ShapeDtypeStruct((H, SUBLANES, T), F32),
        ],
        scratch_shapes=[pltpu.VMEM((hps, 2, pad8(PEER_RANKS), tt), F32),
                        pltpu.VMEM((hps, pad8(sum(_cand_counts())), tt), F32)],
        compiler_params=_cparams(("parallel", "arbitrary")),
        name="peer_topk",
    )(q, subkeys)


def _peer_dense_kernel(*refs, ib, emit_tables):
    first = 0 if emit_tables else 1
    offs_ref = None if emit_tables else refs[0]
    x_ref, g_ref, pw_ref, aux_ref, u_ref, v_ref, o_ref = refs[first:first + 7]
    rest = refs[first + 7:]
    xnT_ref, acc_ref, w_ref = rest[-3:]
    e = pl.program_id(1)
    H, NK = PEER_HEADS, PEER_NKEYS
    tt = x_ref.shape[0]
    strip = min(LANES, tt)
    assert tt % strip == 0

    def gates(blk, ii, dst):
        i = blk * ib + ii
        thr1 = [pw_ref[h, 0, pl.ds(i, 1), :] for h in range(H)]
        e1 = [0.5 * jnp.exp(aux_ref[h, 0:1, :] - thr1[h]) for h in range(H)]
        for tc in range(tt // strip):
            ts_ = slice(tc * strip, (tc + 1) * strip)
            for j0 in range(0, NK, GATE_ROWS):
                js = slice(j0, j0 + GATE_ROWS)
                w = None
                for h in range(H):
                    sel = pw_ref[h, 1, js, ts_] >= thr1[h][:, ts_]
                    c = jnp.where(sel, pw_ref[h, 2, js, ts_], 0.0) * e1[h][:, ts_]
                    w = c if w is None else w + c
                dst[ii * NK + j0:ii * NK + j0 + GATE_ROWS, ts_] = w
        last = jnp.sum(w[0:1, 0:1])
        return last != last

    @pl.when(e == 0)
    def _():
        xn = _rms(x_ref[...], g_ref[...])
        xnT_ref[...] = xn.T.astype(BF16)
        acc_ref[...] = jnp.zeros_like(acc_ref)

    v16 = v_ref[...].astype(BF16)
    if emit_tables:
        u16 = u_ref[...].astype(BF16)
        rest[0][...] = u16
        rest[1][...] = v16
        hT = _dot(u16, xnT_ref[...])
        for ii in range(ib):
            gates(e, ii, w_ref)
    else:
        hs = []
        for ii in range(ib):
            bad = gates(e, ii, w_ref)
            off = pl.multiple_of(jnp.where(bad, offs_ref[ii], ii * NK), NK)
            hs.append(_dot(u_ref[pl.ds(off, NK), :], xnT_ref[...]))
        hT = jnp.concatenate(hs, axis=0)
    th = tt // 2 if tt % (2 * LANES) == 0 else tt
    for t0 in range(0, tt, th):
        aT = w_ref[:, t0:t0 + th] * _gelu_tanh_x2(hT[:, t0:t0 + th])
        acc_ref[t0:t0 + th, :] += _dot(aT.T.astype(BF16), v16)

    @pl.when(e == pl.num_programs(1) - 1)
    def _():
        o_ref[...] = x_ref[...] + acc_ref[...]


def peer_dense(x, g, pw, aux, u_tab, v_tab, *, layer=0, emit_tables=False, tt=512, ib=None):
    T, D = x.shape
    if ib is None:
        ib = 4 if emit_tables else 8
    boff = layer * (PEER_NKEYS // ib)
    tt = min(tt, T)
    assert T % tt == 0 and PEER_NKEYS % ib == 0 and not (emit_tables and T != tt)
    H, NK = PEER_HEADS, PEER_NKEYS
    eb = ib * NK
    once = {} if emit_tables else dict(pipeline_mode=pl.Buffered(1))
    out_specs = [pl.BlockSpec((tt, D), lambda i, e, *_: (i, 0))]
    out_shape = [jax.ShapeDtypeStruct((T, D), F32)]
    args = (x, g.reshape(1, D), pw, aux, u_tab, v_tab)
    if emit_tables:
        out_specs += [pl.BlockSpec((eb, D), lambda i, e, *_: (e, 0))] * 2
        out_shape += [jax.ShapeDtypeStruct((NK * NK, D), BF16)] * 2
    else:
        args = (jnp.arange(ib, dtype=jnp.int32) * NK,) + args
    outs = pl.pallas_call(
        functools.partial(_peer_dense_kernel, ib=ib, emit_tables=emit_tables),
        grid_spec=pltpu.PrefetchScalarGridSpec(
            num_scalar_prefetch=0 if emit_tables else 1,
            grid=(T // tt, NK // ib),
            in_specs=[
                pl.BlockSpec((tt, D), lambda i, e, *_: (i, 0), **once),
                pl.BlockSpec((1, D), lambda i, e, *_: (0, 0)),
                pl.BlockSpec((H, 3, NK, tt), lambda i, e, *_: (0, 0, 0, i), **once),
                pl.BlockSpec((H, SUBLANES, tt), lambda i, e, *_: (0, 0, i)),
                pl.BlockSpec((eb, D), lambda i, e, *_: (boff + e, 0)),
                pl.BlockSpec((eb, D), lambda i, e, *_: (boff + e, 0)),
            ],
            out_specs=out_specs,
            scratch_shapes=[
                pltpu.VMEM((D, tt), BF16),
                pltpu.VMEM((tt, D), F32),
                pltpu.VMEM((eb, tt), F32),
            ],
        ),
        out_shape=out_shape,
        compiler_params=_cparams(("parallel", "arbitrary")),
        name="peer_dense",
    )(*args)
    return tuple(outs) if emit_tables else outs[0]


def _prep_weights(w_in_a, w_in_b, w_out, w_mem_kv, peer_wq, peer_u, peer_v, lru_wr, lru_wi):
    H = GDN_HEADS
    qkvz = GDN_QKV + TM_WIDTH
    wb = jnp.concatenate(
        [
            w_in_b[:, :, :qkvz],
            w_in_b[:, :, qkvz + 2 * H:],
            w_in_b[:, :, qkvz:qkvz + 2 * H],
            jnp.zeros(w_in_b.shape[:2] + (B_IN_PAD - w_in_b.shape[2],), w_in_b.dtype),
        ],
        axis=-1,
    )
    return dict(
        w_in_a=w_in_a.astype(BF16),
        w_in_b=wb.astype(BF16),
        w_out_tm=w_out[:, :TM_WIDTH].astype(BF16),
        w_out_xa=w_out[:, TM_WIDTH:].astype(BF16),
        w_mem_kv=w_mem_kv.astype(BF16),
        peer_wq=peer_wq.astype(BF16),
        peer_u=peer_u.reshape(-1, peer_u.shape[-1]),
        peer_v=peer_v.reshape(-1, peer_v.shape[-1]),
        lru_wr=lru_wr.astype(BF16),
        lru_wi=lru_wi.astype(BF16),
    )


def _trunk(x, mem, lru_h0, lru_c0, gdn_S0, gdn_c0, w, wc, *, B, L, time_major, peer_tabs16=None):
    if time_major:
        G, R, tl = 1, B, L
    else:
        G, R, tl = B, 1, min(L, 256)
    Lp = -(-L // SUBLANES) * SUBLANES
    hist = CONV_W - 1

    def to_batch_major(a):
        c = a.shape[-1]
        a = a.reshape(L, B, c).transpose(1, 0, 2)
        return jnp.pad(a, ((0, 0), (0, Lp - L), (0, 0))).reshape(B * Lp, c)

    def to_time_major(a):
        c = a.shape[-1]
        return a.reshape(B, Lp, c)[:, :L].transpose(1, 0, 2).reshape(L * B, c)

    def last_rows(u, c0, c1):
        if time_major:
            return u.reshape(L, B, -1)[L - hist:, :, c0:c1].transpose(1, 0, 2)
        return u.reshape(B, L, -1)[:, L - hist:, c0:c1]

    n_b = gdn_S0.shape[0]
    S0_all = gdn_S0.reshape((n_b * B,) + gdn_S0.shape[2:])
    S_all = jnp.zeros(S0_all.shape, F32)
    lru_h, lru_c, gdn_c, tabs16 = [], [], [], []
    for l in range(DEPTH):
        j = l // 2
        if l % 2 == 0:
            u, qmem = norm_matmul(x, w['norm_mix'][l], wc['w_in_a'][j], side_cols=(2 * TM_WIDTH, XA_WIDTH))
            c0 = lru_c0[j]
            if time_major:
                c0k = c0.transpose(1, 0, 2).reshape(1, hist * B, TM_WIDTH)
                h0k = lru_h0[j].reshape(1, B, TM_WIDTH)
            else:
                c0k = c0
                h0k = lru_h0[j].reshape(B, 1, TM_WIDTH)
            tm, h = rglru(u, c0k, h0k, w['lru_conv_w'][j], w['lru_conv_b'][j], wc['lru_wr'][j], w['lru_br'][j],
                          wc['lru_wi'][j], w['lru_bi'][j], w['lru_lambda'][j], G=G, L=L, R=R, tl=tl)
            lru_h.append(h.reshape(B, TM_WIDTH))
            lru_c.append(last_rows(u, TM_WIDTH, 2 * TM_WIDTH))
        else:
            u, qmem = norm_matmul(x, w['norm_mix'][l], wc['w_in_b'][j], side_cols=(GDN_QKV + TM_WIDTH, XA_WIDTH))
            C = min(GDN_CHUNK, Lp)
            nb = GDN_SEQS_PER_STEP if (Lp == C and B % GDN_SEQS_PER_STEP == 0) else 1
            u_bm = to_batch_major(u) if time_major else u
            o_bm, S_all = gdn(u_bm, gdn_c0[j], S0_all, S_all, w['gdn_conv_w'][j], w['gdn_conv_b'][j],
                              w['gdn_a_log'][j], w['gdn_dt_bias'][j], w['gdn_o_norm'][j], layer=j, B=B, L=Lp, C=C,
                              n_valid=min(L, C), nb=nb, qkv_blk=0, z_blk=GDN_QKV // TM_WIDTH,
                              ba_blk=(GDN_QKV + TM_WIDTH + XA_WIDTH) // LANES)
            tm = to_time_major(o_bm) if time_major else o_bm
            gdn_c.append(last_rows(u, 0, GDN_QKV))
        if time_major:
            q_bm = to_batch_major(qmem)
            xa = to_time_major(xattn_cache(q_bm, mem[0], mem[1], layer=l, B=B, tq=Lp,
                                           nb=XATTN_SEQS_PER_STEP if B % XATTN_SEQS_PER_STEP == 0 else 1))
        else:
            xa = xattn(qmem, mem[l], mem[l], B=B, Lq=L, tq=min(L, 512), q_blk=0, k_blk=0, v_blk=1)
        x = out_proj(tm, xa, wc['w_out_tm'][l], wc['w_out_xa'][l], x)
        q = norm_matmul(x, w['norm_ffn'][l], wc['peer_wq'][l])
        pw, aux = peer_topk(q, w['peer_subkeys'][l])
        if peer_tabs16 is None:
            x, u16, v16 = peer_dense(x, w['norm_ffn'][l], pw, aux, wc['peer_u'], wc['peer_v'], layer=l,
                                     emit_tables=True)
            tabs16.append((u16, v16))
        else:
            x = peer_dense(x, w['norm_ffn'][l], pw, aux, peer_tabs16[l][0], peer_tabs16[l][1])
    y = rmsnorm_rows(x, w['norm_final'])
    return (y, jnp.stack(lru_h), jnp.stack(lru_c), S_all.reshape(gdn_S0.shape), jnp.stack(gdn_c)), tabs16


def kernel(x_prompt, x_sample, state_rglru_h, state_rglru_conv, state_gdn_S, state_gdn_conv, cache_mem_k, cache_mem_v, mem_prompt, norm_mix, norm_ffn, norm_final, w_in_a, w_in_b, w_out, lru_conv_w, lru_conv_b, lru_wr, lru_br, lru_wi, lru_bi, lru_lambda, gdn_conv_w, gdn_conv_b, gdn_a_log, gdn_dt_bias, gdn_o_norm, norm_mem, w_mem_kv, peer_wq, peer_subkeys, peer_u, peer_v):
    w = dict(norm_mix=norm_mix, norm_ffn=norm_ffn, norm_final=norm_final, lru_conv_w=lru_conv_w,
             lru_conv_b=lru_conv_b, lru_br=lru_br, lru_bi=lru_bi, lru_lambda=lru_lambda, gdn_conv_w=gdn_conv_w,
             gdn_conv_b=gdn_conv_b, gdn_a_log=gdn_a_log, gdn_dt_bias=gdn_dt_bias, gdn_o_norm=gdn_o_norm,
             peer_subkeys=peer_subkeys)
    wc = _prep_weights(w_in_a, w_in_b, w_out, w_mem_kv, peer_wq, peer_u, peer_v, lru_wr, lru_wi)
    Bp, Lp_, D = x_prompt.shape
    Bs, Ls, _ = x_sample.shape
    n_a, n_b = state_rglru_h.shape[0], state_gdn_S.shape[0]

    xs = x_sample.transpose(1, 0, 2).reshape(Ls * Bs, D)
    rows_hd = (DEPTH * Bs, N_MEM * XA_HEADS, HEAD_DIM)
    (y_s, s_h, s_ca, s_S, s_cb), tabs16 = _trunk(
        xs, (cache_mem_k.reshape(rows_hd), cache_mem_v.reshape(rows_hd)), state_rglru_h, state_rglru_conv,
        state_gdn_S, state_gdn_conv, w, wc, B=Bs, L=Ls, time_major=True)
    y_s = y_s.reshape(Ls, Bs, D).transpose(1, 0, 2)

    mem2 = mem_prompt.reshape(Bp * N_MEM, D)
    kv = [norm_matmul(mem2, norm_mem[l], wc['w_mem_kv'][l]) for l in range(DEPTH)]
    (y_p, p_h, p_ca, p_S, p_cb), _ = _trunk(
        x_prompt.reshape(Bp * Lp_, D), kv,
        jnp.zeros((n_a, Bp, TM_WIDTH), F32), jnp.zeros((n_a, Bp, CONV_W - 1, TM_WIDTH), F32),
        jnp.zeros((n_b, Bp, GDN_HEADS, HEAD_DIM, HEAD_DIM), F32), jnp.zeros((n_b, Bp, CONV_W - 1, GDN_QKV), F32),
        w, wc, B=Bp, L=Lp_, time_major=False, peer_tabs16=tabs16)
    p_mem_k = jnp.stack([a[:, :XA_WIDTH].reshape(Bp, N_MEM, XA_HEADS, HEAD_DIM) for a in kv])
    p_mem_v = jnp.stack([a[:, XA_WIDTH:].reshape(Bp, N_MEM, XA_HEADS, HEAD_DIM) for a in kv])
    return (y_p.reshape(Bp, Lp_, D), y_s, p_h, p_ca, p_S, p_cb, p_mem_k, p_mem_v, s_h, s_ca, s_S, s_cb)
```
